```python
import jax, jax.numpy as jnp
from jax import lax
import numpy as np

D_MODEL = 1024
BATCH = 1
SEQ = 16384
DEPTH = 1
DEC_BATCH = 16
DEC_SEQ = 32
PAST_LEN = 2048

CHUNK = 64
MIX_WIDTH = D_MODEL
GLA_HEADS = 4
GLA_DV = MIX_WIDTH // 2 // GLA_HEADS
GLA_DK = GLA_DV // 2
GLA_WIDTH = GLA_HEADS * GLA_DV
GLA_GATE_RANK = 16
GLA_GATE_TAU = 16.0
DSA_HEADS = 4
DSA_DH = MIX_WIDTH // 2 // DSA_HEADS
DSA_WIDTH = DSA_HEADS * DSA_DH
IDX_HEADS = 8
IDX_DIM = 64
IDX_TOPK_MAX = 256
Q_BLOCK = 128
N_EXPERTS = 32
TOP_K = 4
D_FF = D_MODEL
SWIGLU_ALPHA = 1.702
SWIGLU_LIMIT = 7.0
MOE_BLOCK = 128
EPS = 1e-6

PROJ_SIZES = (GLA_HEADS * GLA_DK,
              GLA_HEADS * GLA_DK,
              GLA_WIDTH,
              GLA_WIDTH,
              GLA_GATE_RANK,
              DSA_WIDTH,
              DSA_WIDTH,
              DSA_WIDTH,
              IDX_HEADS * IDX_DIM,
              IDX_DIM,
              IDX_HEADS)
PROJ_WIDTH = sum(PROJ_SIZES)
PROJ_SPLITS = tuple(sum(PROJ_SIZES[:i + 1]) for i in range(len(PROJ_SIZES) - 1))

kernel_name = "hymba_gla_dsa_moe_stream_step"


def rms_norm(x, w):
    xf = x.astype(jnp.float32)
    y = xf * lax.rsqrt(jnp.mean(xf * xf, axis=-1, keepdims=True) + EPS)
    return (y * w.astype(jnp.float32)).astype(x.dtype)


def project(hn, w_in, w_gla_a2, b_gla_a2):
    B, T, _ = hn.shape
    p = hn @ w_in
    gq, gk, gv, gg, ga, dq, dk, dv, iq, ik, iw = jnp.split(p, PROJ_SPLITS, axis=-1)
    gla_q = gq.reshape(B, T, GLA_HEADS, GLA_DK) * (GLA_DK ** -0.5)
    gla_k = gk.reshape(B, T, GLA_HEADS, GLA_DK)
    gla_v = gv.reshape(B, T, GLA_HEADS, GLA_DV)
    gate_logit = (ga @ w_gla_a2 + b_gla_a2).astype(jnp.float32)
    gla_lg = (jax.nn.log_sigmoid(gate_logit) / GLA_GATE_TAU).reshape(B, T, GLA_HEADS, GLA_DK)
    dsa_q = dq.reshape(B, T, DSA_HEADS, DSA_DH)
    dsa_k = dk.reshape(B, T, DSA_HEADS, DSA_DH)
    dsa_v = dv.reshape(B, T, DSA_HEADS, DSA_DH)
    idx_q = iq.reshape(B, T, IDX_HEADS, IDX_DIM)
    return gla_q, gla_k, gla_v, gla_lg, gg, dsa_q, dsa_k, dsa_v, idx_q, ik, iw


def gla_chunk(S, q, k, v, lg):
    f32 = jnp.float32
    q, k, v = q.astype(f32), k.astype(f32), v.astype(f32)
    b = jnp.cumsum(lg.astype(f32), axis=1)
    C = q.shape[1]
    causal = jnp.tril(jnp.ones((C, C), dtype=bool))
    rel = jnp.minimum(b[:, :, None] - b[:, None, :], 0.0)
    decay = jnp.where(causal[None, :, :, None, None], jnp.exp(rel), 0.0)
    attn = jnp.einsum('bihd,bjhd,bijhd->bhij', q, k, decay)
    o = (jnp.einsum('bhij,bjhv->bihv', attn, v)
         + jnp.einsum('bihd,bhdv->bihv', q * jnp.exp(b), S))
    b_last = b[:, -1]
    S_new = (jnp.exp(b_last)[..., None] * S
             + jnp.einsum('bjhd,bjhv->bhdv', k * jnp.exp(b_last[:, None] - b), v))
    return S_new, o


def gla_prompt(q, k, v, lg, s0):
    B, T = q.shape[:2]
    nc = T // CHUNK

    def to_chunks(a):
        return jnp.moveaxis(a.reshape(B, nc, CHUNK, *a.shape[2:]), 1, 0)

    def step(S, xs):
        return gla_chunk(S, *xs)

    S_fin, o = lax.scan(step, s0, (to_chunks(q), to_chunks(k), to_chunks(v), to_chunks(lg)))
    o = jnp.moveaxis(o, 0, 1).reshape(B, T, GLA_HEADS, GLA_DV)
    return o, S_fin


def dsa_attend(q, iq, iw, q_pos, k, v, ik, k_pos, topk):
    f32 = jnp.float32
    q_chunk = q_pos // CHUNK
    admissible = (k_pos // CHUNK)[None, :] <= q_chunk[:, None]
    rel = jax.nn.relu(jnp.einsum('bqhd,bld->bqhl', iq, ik).astype(f32) * (IDX_DIM ** -0.5))
    score = jnp.einsum('bqhl,bqh->bql', rel, iw.astype(f32) * (IDX_HEADS ** -0.5))
    score = jnp.where(admissible[None], score, -jnp.inf)
    _, idx = lax.top_k(score, topk)
    valid = (k_pos[idx] // CHUNK) <= q_chunk[None, :, None]
    gather = jax.vmap(lambda a, i: a[i])
    k_sel = gather(k, idx)
    v_sel = gather(v, idx)
    logits = jnp.einsum('bqhd,bqkhd->bhqk', q, k_sel).astype(f32) * (DSA_DH ** -0.5)
    logits = jnp.where(valid[:, None], logits, -jnp.inf)
    p = jax.nn.softmax(logits, axis=-1).astype(v.dtype)
    return jnp.einsum('bhqk,bqkhd->bqhd', p, v_sel)


def dsa_prompt(q, iq, iw, k, v, ik):
    B, T = q.shape[:2]
    topk = min(IDX_TOPK_MAX, T // 4)
    nb = T // Q_BLOCK
    k_pos = jnp.arange(T, dtype=jnp.int32)
    q_pos = k_pos.reshape(nb, Q_BLOCK)

    def blocks(a):
        return jnp.moveaxis(a.reshape(B, nb, Q_BLOCK, *a.shape[2:]), 1, 0)

    def one(xs):
        qb, iqb, iwb, pb = xs
        return dsa_attend(qb, iqb, iwb, pb, k, v, ik, k_pos, topk)

    o = lax.map(one, (blocks(q), blocks(iq), blocks(iw), q_pos))
    return jnp.moveaxis(o, 0, 1).reshape(B, T, DSA_HEADS, DSA_DH)


def mix_out(o_gla, g, o_dsa, gla_norm_w, w_out):
    B, T = o_gla.shape[:2]
    a = rms_norm(o_gla, gla_norm_w).reshape(B, T, GLA_WIDTH) * jax.nn.silu(g)
    mixed = jnp.concatenate([a, o_dsa.reshape(B, T, DSA_WIDTH)], axis=-1)
    return mixed @ w_out


def moe(h, router_w, router_b, w_gu, b_gu, w_down, b_down):
    shape = h.shape
    x = h.reshape(-1, shape[-1])
    n = x.shape[0]
    logits = (x @ router_w + router_b).astype(jnp.float32)
    top_logit, top_e = lax.top_k(logits, TOP_K)
    gate = jax.nn.softmax(top_logit, axis=-1)
    flat_e = top_e.reshape(-1)
    flat_tok = jnp.arange(n * TOP_K, dtype=jnp.int32) // TOP_K
    flat_gate = gate.reshape(-1)
    order = jnp.argsort(flat_e)
    e_sorted = flat_e[order]
    counts = jnp.bincount(flat_e, length=N_EXPERTS)
    padded = (counts + MOE_BLOCK - 1) // MOE_BLOCK * MOE_BLOCK
    pad_end = jnp.cumsum(padded)
    pad_start = pad_end - padded
    start = jnp.cumsum(counts) - counts
    dest = pad_start[e_sorted] + jnp.arange(n * TOP_K, dtype=jnp.int32) - start[e_sorted]
    n_blocks = -(-(n * TOP_K + N_EXPERTS * (MOE_BLOCK - 1)) // MOE_BLOCK)
    n_rows = n_blocks * MOE_BLOCK
    row_tok = jnp.full((n_rows,), n, jnp.int32).at[dest].set(flat_tok[order])
    row_gate = jnp.zeros((n_rows,), jnp.float32).at[dest].set(flat_gate[order])
    block_start = jnp.arange(n_blocks, dtype=jnp.int32) * MOE_BLOCK
    block_e = jnp.minimum(jnp.searchsorted(pad_end, block_start, side='right'), N_EXPERTS - 1)
    x_pad = jnp.concatenate([x, jnp.zeros((1, shape[-1]), x.dtype)], axis=0)
    x_rows = x_pad[row_tok].reshape(n_blocks, MOE_BLOCK, shape[-1])

    def expert_block(args):
        xb, e = args
        gu = xb @ w_gu[e] + b_gu[e]
        g_lin, u_lin = jnp.split(gu, 2, axis=-1)
        g_lin = jnp.minimum(g_lin, SWIGLU_LIMIT)
        u_lin = jnp.clip(u_lin, -SWIGLU_LIMIT, SWIGLU_LIMIT)
        act = g_lin * jax.nn.sigmoid(SWIGLU_ALPHA * g_lin) * (u_lin + 1.0)
        return act @ w_down[e] + b_down[e]

    y_rows = lax.map(expert_block, (x_rows, block_e)).reshape(n_rows, shape[-1])
    y_rows = y_rows * row_gate[:, None].astype(y_rows.dtype)
    y = jnp.zeros((n + 1, shape[-1]), y_rows.dtype).at[row_tok].add(y_rows)[:n]
    return y.reshape(shape)


def setup_inputs(seed: int = 0) -> dict:
    key = jax.random.key(seed)
    ks = jax.random.split(key, 20)
    f32 = jnp.float32

    def nrm(k, shape, scale=1.0):
        return jax.random.normal(k, shape, f32) * scale

    return {
        "x_prompt": nrm(ks[0], (BATCH, SEQ, D_MODEL)),
        "x_sample": nrm(ks[1], (DEC_BATCH, DEC_SEQ, D_MODEL)),
        "cache_k": nrm(ks[2], (DEPTH, DEC_BATCH, PAST_LEN, DSA_HEADS, DSA_DH)),
        "cache_v": nrm(ks[3], (DEPTH, DEC_BATCH, PAST_LEN, DSA_HEADS, DSA_DH)),
        "cache_k_idx": nrm(ks[4], (DEPTH, DEC_BATCH, PAST_LEN, IDX_DIM)),
        "state_gla": nrm(ks[5], (DEPTH, DEC_BATCH, GLA_HEADS, GLA_DK, GLA_DV)),
        "norm_mix_w": 1.0 + nrm(ks[6], (DEPTH, D_MODEL), 0.02),
        "w_in": nrm(ks[7], (DEPTH, D_MODEL, PROJ_WIDTH), D_MODEL ** -0.5),
        "w_gla_a2": nrm(ks[8], (DEPTH, GLA_GATE_RANK, GLA_HEADS * GLA_DK), GLA_GATE_RANK ** -0.5),
        "b_gla_a2": nrm(ks[9], (DEPTH, GLA_HEADS * GLA_DK), 0.1),
        "gla_norm_w": 1.0 + nrm(ks[10], (DEPTH, GLA_DV), 0.02),
        "w_out": nrm(ks[11], (DEPTH, MIX_WIDTH, D_MODEL), MIX_WIDTH ** -0.5),
        "norm_ffn_w": 1.0 + nrm(ks[12], (DEPTH, D_MODEL), 0.02),
        "router_w": nrm(ks[13], (DEPTH, D_MODEL, N_EXPERTS), D_MODEL ** -0.5),
        "router_b": nrm(ks[14], (DEPTH, N_EXPERTS), 0.01),
        "w_gu": nrm(ks[15], (DEPTH, N_EXPERTS, D_MODEL, 2 * D_FF), D_MODEL ** -0.5),
        "b_gu": nrm(ks[16], (DEPTH, N_EXPERTS, 2 * D_FF), 0.01),
        "w_down": nrm(ks[17], (DEPTH, N_EXPERTS, D_FF, D_MODEL), D_FF ** -0.5),
        "b_down": nrm(ks[18], (DEPTH, N_EXPERTS, D_MODEL), 0.01),
        "norm_final_w": 1.0 + nrm(ks[19], (D_MODEL,), 0.02),
    }


def reference(x_prompt, x_sample, cache_k, cache_v, cache_k_idx, state_gla,
              norm_mix_w, w_in, w_gla_a2, b_gla_a2, gla_norm_w, w_out,
              norm_ffn_w, router_w, router_b, w_gu, b_gu, w_down, b_down, norm_final_w):
    B = x_prompt.shape[0]
    P = cache_k.shape[2]
    T_s = x_sample.shape[1]
    hp, hs = x_prompt, x_sample
    kp_l, vp_l, ikp_l, sp_l = [], [], [], []
    ks_l, vs_l, iks_l, ss_l = [], [], [], []
    for l in range(DEPTH):
        hn = rms_norm(hp, norm_mix_w[l])
        gq, gk, gv, glg, gg, dq, dk, dv, iq, ik, iw = project(hn, w_in[l], w_gla_a2[l], b_gla_a2[l])
        s0 = jnp.zeros((B, GLA_HEADS, GLA_DK, GLA_DV), jnp.float32)
        o_gla, s_fin = gla_prompt(gq, gk, gv, glg, s0)
        o_dsa = dsa_prompt(dq, iq, iw, dk, dv, ik)
        hp = hp + mix_out(o_gla.astype(hn.dtype), gg, o_dsa, gla_norm_w[l], w_out[l])
        hp = hp + moe(rms_norm(hp, norm_ffn_w[l]), router_w[l], router_b[l],
                      w_gu[l], b_gu[l], w_down[l], b_down[l])
        kp_l.append(dk)
        vp_l.append(dv)
        ikp_l.append(ik)
        sp_l.append(s_fin.astype(state_gla.dtype))
        hn = rms_norm(hs, norm_mix_w[l])
        gq, gk, gv, glg, gg, dq, dk, dv, iq, ik, iw = project(hn, w_in[l], w_gla_a2[l], b_gla_a2[l])
        s_new, o_gla = gla_chunk(state_gla[l].astype(jnp.float32), gq, gk, gv, glg)
        k_all = jnp.concatenate([cache_k[l], dk], axis=1)
        v_all = jnp.concatenate([cache_v[l], dv], axis=1)
        ik_all = jnp.concatenate([cache_k_idx[l], ik], axis=1)
        k_pos = jnp.arange(P + T_s, dtype=jnp.int32)
        q_pos = P + jnp.arange(T_s, dtype=jnp.int32)
        topk = min(IDX_TOPK_MAX, (P + T_s) // 4)
        o_dsa = dsa_attend(dq, iq, iw, q_pos, k_all, v_all, ik_all, k_pos, topk)
        hs = hs + mix_out(o_gla.astype(hn.dtype), gg, o_dsa, gla_norm_w[l], w_out[l])
        hs = hs + moe(rms_norm(hs, norm_ffn_w[l]), router_w[l], router_b[l],
                      w_gu[l], b_gu[l], w_down[l], b_down[l])
        ks_l.append(dk)
        vs_l.append(dv)
        iks_l.append(ik)
        ss_l.append(s_new.astype(state_gla.dtype))
    y_prompt = rms_norm(hp, norm_final_w)
    y_sample = rms_norm(hs, norm_final_w)
    return (y_prompt, y_sample,
            jnp.stack(kp_l), jnp.stack(vp_l), jnp.stack(ikp_l), jnp.stack(sp_l),
            jnp.stack(ks_l), jnp.stack(vs_l), jnp.stack(iks_l), jnp.stack(ss_l))
```

```python
import functools

import jax
import jax.numpy as jnp
from jax import lax
from jax.experimental import pallas as pl
from jax.experimental.pallas import tpu as pltpu

F32 = jnp.float32
BF16 = jnp.bfloat16
I32 = jnp.int32

D_MODEL = 1024
CHUNK = 64
GLA_HEADS = 4
GLA_DK = 64
GLA_DV = 128
GLA_QK = GLA_HEADS * GLA_DK
GLA_WIDTH = GLA_HEADS * GLA_DV
GLA_GATE_RANK = 16
GLA_GATE_TAU = 16.0
DSA_HEADS = 4
DSA_DH = 128
DSA_WIDTH = DSA_HEADS * DSA_DH
IDX_HEADS = 8
IDX_DIM = 64
IDX_TOPK_MAX = 256
N_EXPERTS = 32
TOP_K = 4
D_FF = 1024
SWIGLU_ALPHA = 1.702
SWIGLU_LIMIT = 7.0
EPS = 1e-6

_OFF_GLA = 0
_W_GLA = 2 * GLA_QK + 2 * GLA_WIDTH
_OFF_GA = _OFF_GLA + _W_GLA
_OFF_DSA = _OFF_GA + GLA_GATE_RANK
_W_DSA = 3 * DSA_WIDTH + IDX_HEADS * IDX_DIM
_OFF_IK = _OFF_DSA + _W_DSA
_OFF_IW = _OFF_IK + IDX_DIM
LANE = 128
SUBLANE = 8
ROW_TILE = D_MODEL // LANE
assert ROW_TILE == SUBLANE
LOG2E = 1.4426950408889634
NEG_BIG = -1e30
VMEM_LIMIT = 56 * 1024 * 1024


def _cparams(sem):
    return pltpu.CompilerParams(dimension_semantics=sem, vmem_limit_bytes=VMEM_LIMIT)


def _dot(a, b):
    return jnp.dot(a, b, preferred_element_type=F32)


def _dot_nt(a, b):
    return lax.dot_general(a, b, (((1,), (1,)), ((), ())), preferred_element_type=F32)


def _dot_tn(a, b):
    return lax.dot_general(a, b, (((0,), (0,)), ((), ())), preferred_element_type=F32)


def _split3(x):
    hi = x.astype(BF16)
    r1 = x - hi.astype(F32)
    mid = r1.astype(BF16)
    lo = (r1 - mid.astype(F32)).astype(BF16)
    return hi, mid, lo


def _proj_kernel(x_ref, nw_ref, wg_ref, wd_ref, ws_ref,
                 gla_ref, q_ref, kf_ref, vf_ref, kb_ref, vb_ref, iq_ref,
                 iklo_ref, ikhi_ref, ikf_ref, g2_ref):
    x = x_ref[...]
    ms = jnp.mean(x * x, axis=-1, keepdims=True)
    xn = (x * lax.rsqrt(ms + EPS) * nw_ref[...]).astype(BF16)
    gla_ref[...] = _dot(xn, wg_ref[...]).astype(BF16)
    W = DSA_WIDTH
    dq = _dot(xn, wd_ref[:, 0:W])
    q_ref[...] = (dq * (DSA_DH ** -0.5 * LOG2E)).astype(BF16)
    dk = _dot(xn, wd_ref[:, W:2 * W])
    kf_ref[...] = dk
    kb_ref[...] = dk.astype(BF16)
    dv = _dot(xn, wd_ref[:, 2 * W:3 * W])
    vf_ref[...] = dv
    vb_ref[...] = dv.astype(BF16)
    iq_ref[...] = _dot(xn, wd_ref[:, 3 * W:4 * W]).astype(BF16)
    sm = _dot(xn, ws_ref[...])
    iklo_ref[...] = sm[:, 0:LANE].astype(BF16)
    ikhi_ref[...] = sm[:, LANE:2 * LANE].astype(BF16)
    ikf_ref[...] = sm[:, 0:IDX_DIM]
    g2_ref[...] = sm[:, 2 * LANE:3 * LANE]


def _proj_weights(norm_w, w_in):
    wg = w_in[:, _OFF_GLA:_OFF_GLA + _W_GLA].astype(BF16)
    wd = w_in[:, _OFF_DSA:_OFF_DSA + _W_DSA].astype(BF16)
    ik = w_in[:, _OFF_IK:_OFF_IK + IDX_DIM]
    z64 = jnp.zeros((D_MODEL, IDX_DIM), F32)
    ga = w_in[:, _OFF_GA:_OFF_GA + GLA_GATE_RANK]
    iw = w_in[:, _OFF_IW:_OFF_IW + IDX_HEADS]
    zpad = jnp.zeros((D_MODEL, LANE - GLA_GATE_RANK - IDX_HEADS), F32)
    ws = jnp.concatenate([ik, z64, z64, ik, ga, iw, zpad], axis=1).astype(BF16)
    return norm_w.reshape(1, D_MODEL), wg, wd, ws


def _project(x2d, pw, bm):
    n = x2d.shape[0]
    nw, wg, wd, ws = pw
    row = lambda w: pl.BlockSpec((bm, w), lambda i: (i, 0))
    full = lambda a: pl.BlockSpec(a.shape, lambda i: (0, 0))
    outs = [(_W_GLA, BF16), (DSA_WIDTH, BF16), (DSA_WIDTH, F32), (DSA_WIDTH, F32),
            (DSA_WIDTH, BF16), (DSA_WIDTH, BF16), (IDX_HEADS * IDX_DIM, BF16),
            (LANE, BF16), (LANE, BF16), (IDX_DIM, F32), (LANE, F32)]
    return pl.pallas_call(
        _proj_kernel,
        grid=(n // bm,),
        in_specs=[row(D_MODEL), full(nw), full(wg), full(wd), full(ws)],
        out_specs=[row(w) for w, _ in outs],
        out_shape=[jax.ShapeDtypeStruct((n, w), dt) for w, dt in outs],
        compiler_params=_cparams(("parallel",)),
        name="proj",
    )(x2d, nw, wg, wd, ws)


def _log_sigmoid(x):
    return jnp.minimum(x, 0.0) - jnp.log1p(jnp.exp(-jnp.abs(x)))


def _gla_kernel(gla_ref, g2_ref, w2_ref, b2_ref, seg_ref, bd_ref, s0_ref,
                o_ref, sT_ref,
                st_scr, kpad, bpad, vpad, qt_scr, kt_scr, dec_scr, oi_scr,
                *, bt, c):
    t = pl.program_id(1)

    @pl.when(t == 0)
    def _():
        st_scr[...] = s0_ref[0]

    q = gla_ref[:, 0:GLA_QK].astype(F32) * (GLA_DK ** -0.5)
    k = gla_ref[:, GLA_QK:2 * GLA_QK].astype(F32)
    vb = gla_ref[:, 2 * GLA_QK:2 * GLA_QK + GLA_WIDTH]
    v = vb.astype(F32)

    ga = g2_ref[...]
    ga_hi = ga.astype(BF16)
    ga_lo = (ga - ga_hi.astype(F32)).astype(BF16)
    logit = (_dot(ga_hi, w2_ref[0]) + _dot(ga_hi, w2_ref[1]) + _dot(ga_lo, w2_ref[0])
             + b2_ref[...])
    lg = _log_sigmoid(logit) * (1.0 / GLA_GATE_TAU)

    shift = c.bit_length() - 1
    row = lax.broadcasted_iota(I32, (bt, bt), 0)
    col = lax.broadcasted_iota(I32, (bt, bt), 1)
    same = (row >> shift) == (col >> shift)
    tri = jnp.where(same & (col <= row), 1.0, 0.0).astype(BF16)
    last = jnp.where(same & ((col & (c - 1)) == c - 1), 1.0, 0.0).astype(BF16)
    l0, l1, l2 = _split3(lg)
    b = _dot(tri, l0) + _dot(tri, l1) + _dot(tri, l2)
    b0, b1, b2s = _split3(b)
    bl = _dot(last, b0) + _dot(last, b1) + _dot(last, b2s)

    zpad = jnp.zeros((c, GLA_QK), F32)
    kpad[0:c, :] = zpad
    bpad[0:c, :] = zpad
    vpad[0:c, :] = jnp.zeros((c, GLA_WIDTH), F32)
    kpad[c:c + bt, :] = k
    bpad[c:c + bt, :] = b
    vpad[c:c + bt, :] = v
    pos = lax.broadcasted_iota(I32, (bt, GLA_QK), 0) & (c - 1)
    seg = seg_ref[...]
    o_intra = jnp.zeros((bt, GLA_WIDTH), F32)
    for d in range(c):
        ks = kpad[c - d:c - d + bt, :]
        bs = bpad[c - d:c - d + bt, :]
        vs = vpad[c - d:c - d + bt, :]
        z = q * ks * jnp.exp(jnp.minimum(b - bs, 0.0))
        z = jnp.where(pos >= d, z, 0.0)
        o_intra = o_intra + _dot(z.astype(BF16), seg) * vs
    oi_scr[...] = o_intra

    qt_scr[...] = q * jnp.exp(b)
    kt_scr[...] = k * jnp.exp(bl - b)
    dec_scr[...] = jnp.exp(bl)
    bd = bd_ref[...]

    def step(ci, carry):
        r0 = pl.multiple_of(ci * c, c)
        qc = qt_scr[pl.ds(r0, c), :].astype(BF16)
        kc = kt_scr[pl.ds(r0, c), :].astype(BF16)
        vc = vpad[pl.ds(r0 + c, c), :].astype(BF16)
        st = st_scr[...]
        o_ref[pl.ds(r0, c), :] = oi_scr[pl.ds(r0, c), :] + _dot_nt(qc, st.astype(BF16))
        dec = dec_scr[pl.ds(r0, 1), :]
        st_scr[...] = st * dec + _dot_tn(vc, kc) * bd
        return carry

    lax.fori_loop(0, bt // c, step, 0)

    @pl.when(t == pl.num_programs(1) - 1)
    def _():
        sT_ref[0] = st_scr[...]


def _gla_consts(w_gla_a2, b_gla_a2):
    w2 = jnp.zeros((LANE, GLA_QK), F32).at[0:GLA_GATE_RANK].set(w_gla_a2)
    w2_hi = w2.astype(BF16)
    w2_lo = (w2 - w2_hi.astype(F32)).astype(BF16)
    w2s = jnp.stack([w2_hi, w2_lo])
    hq = jnp.arange(GLA_QK) // GLA_DK
    hv = jnp.arange(GLA_WIDTH) // GLA_DV
    seg = (hq[:, None] == hv[None, :]).astype(BF16)
    bd = (hv[:, None] == hq[None, :]).astype(F32)
    return w2s, b_gla_a2.reshape(1, GLA_QK), seg, bd


def _state_to_kernel(s):
    n = s.shape[0]
    eye = jnp.eye(GLA_HEADS, dtype=s.dtype)
    t = jnp.swapaxes(s, 2, 3)[:, :, :, None, :] * eye[None, :, None, :, None]
    return t.reshape(n, GLA_WIDTH, GLA_QK)


def _state_from_kernel(t):
    n = t.shape[0]
    t5 = t.reshape(n, GLA_HEADS, GLA_DV, GLA_HEADS, GLA_DK)
    return jnp.einsum("shehd->shde", t5)


def _gla(gla, g2, s0, consts, n_streams, t_len, bt, c):
    w2s, b2, seg, bd = consts
    nb = t_len // bt
    kern = functools.partial(_gla_kernel, bt=bt, c=c)
    const = lambda a: pl.BlockSpec(a.shape, lambda s, t: (0,) * a.ndim)
    o, sT = pl.pallas_call(
        kern,
        grid=(n_streams, nb),
        in_specs=[
            pl.BlockSpec((bt, 2 * GLA_QK + GLA_WIDTH), lambda s, t: (s * nb + t, 0)),
            pl.BlockSpec((bt, LANE), lambda s, t: (s * nb + t, 0)),
            const(w2s), const(b2), const(seg), const(bd),
            pl.BlockSpec((1, GLA_WIDTH, GLA_QK), lambda s, t: (s, 0, 0)),
        ],
        out_specs=[
            pl.BlockSpec((bt, GLA_WIDTH), lambda s, t: (s * nb + t, 0)),
            pl.BlockSpec((1, GLA_WIDTH, GLA_QK), lambda s, t: (s, 0, 0)),
        ],
        out_shape=[
            jax.ShapeDtypeStruct((n_streams * t_len, GLA_WIDTH), F32),
            jax.ShapeDtypeStruct((n_streams, GLA_WIDTH, GLA_QK), F32),
        ],
        scratch_shapes=[
            pltpu.VMEM((GLA_WIDTH, GLA_QK), F32),
            pltpu.VMEM((bt + c, GLA_QK), F32),
            pltpu.VMEM((bt + c, GLA_QK), F32),
            pltpu.VMEM((bt + c, GLA_WIDTH), F32),
            pltpu.VMEM((bt, GLA_QK), F32),
            pltpu.VMEM((bt, GLA_QK), F32),
            pltpu.VMEM((bt, GLA_QK), F32),
            pltpu.VMEM((bt, GLA_WIDTH), F32),
        ],
        compiler_params=_cparams(("arbitrary", "arbitrary")),
        name="gla",
    )(gla, g2, w2s, b2, seg, bd, s0)
    return o, sT


INT_MIN = -2 ** 31
IDX_SCALE = IDX_DIM ** -0.5 * IDX_HEADS ** -0.5


def _fill_head_weights(wb_scr, g2_ref, bq):
    for h in range(IDX_HEADS):
        c0 = GLA_GATE_RANK + h
        col = g2_ref[:, c0:c0 + 1] * IDX_SCALE
        wb_scr[h] = jnp.broadcast_to(col, (bq, LANE))


def _index_keys(iq_ref, wb_scr, iklo, ikhi, bk):
    parts = [None] * (bk // LANE)
    for p in range(IDX_HEADS // 2):
        iqp = iq_ref[:, p * LANE:(p + 1) * LANE]
        y0 = jnp.maximum(_dot_nt(iqp, iklo), 0.0)
        y1 = jnp.maximum(_dot_nt(iqp, ikhi), 0.0)
        w0 = wb_scr[2 * p]
        w1 = wb_scr[2 * p + 1]
        for c in range(bk // LANE):
            t = y0[:, c * LANE:(c + 1) * LANE] * w0 + y1[:, c * LANE:(c + 1) * LANE] * w1
            parts[c] = t if parts[c] is None else parts[c] + t
    s = jnp.concatenate(parts, axis=1) + 0.0
    bits = pltpu.bitcast(s, I32)
    return bits ^ ((bits >> 31) & 0x7FFFFFFF)


def _count_ge(load_tile, n_tiles, cand, bq, bk):
    cand_b = jnp.broadcast_to(cand, (bq, LANE))

    def body(kt, acc):
        keys = load_tile(kt)
        for c in range(bk // LANE):
            acc = acc + jnp.where(keys[:, c * LANE:(c + 1) * LANE] >= cand_b, 1, 0)
        return acc

    acc = lax.fori_loop(0, n_tiles, body, jnp.zeros((bq, LANE), I32))
    return jnp.sum(acc, axis=1, keepdims=True)


def _kth_largest(load_tile, n_tiles, topk, bq, bk):
    total = _count_ge(load_tile, n_tiles, jnp.full((bq, 1), INT_MIN, I32), bq, bk)

    def bit_body(i, carry):
        prefix, n_ge = carry
        bit = jnp.left_shift(jnp.int32(1), 31 - i)
        cand_u = prefix | bit
        cnt = _count_ge(load_tile, n_tiles, cand_u ^ INT_MIN, bq, bk)
        ok = cnt >= topk
        return jnp.where(ok, cand_u, prefix), jnp.where(ok, cnt, n_ge)

    prefix, n_ge = lax.fori_loop(0, 32, bit_body, (jnp.zeros((bq, 1), I32), total))
    t = prefix ^ INT_MIN
    t_next = jnp.where(t == 2 ** 31 - 1, t, t + 1)
    n_gt = jnp.where(t == 2 ** 31 - 1, 0, _count_ge(load_tile, n_tiles, t_next, bq, bk))
    return t, n_ge, n_gt


def _attend_tile(sel, q_ref, kv_tile, m_scr, l_scr, acc_scr):
    for h in range(DSA_HEADS):
        hs = slice(h * DSA_DH, (h + 1) * DSA_DH)
        kh, vh = kv_tile(h)
        lg = jnp.where(sel, _dot_nt(q_ref[:, hs], kh), NEG_BIG)
        m_old = m_scr[h]
        m_new = jnp.maximum(m_old, jnp.max(lg, axis=1, keepdims=True))
        alpha = jnp.exp2(m_old - m_new)
        p = jnp.exp2(lg - m_new)
        l_scr[h] = alpha * l_scr[h] + jnp.sum(p, axis=1, keepdims=True)
        acc_scr[:, hs] = alpha * acc_scr[:, hs] + _dot(p.astype(BF16), vh)
        m_scr[h] = m_new


def _select(keys, t, n_gt, eq_before, topk, upper, ties):
    if not ties:
        return keys >= jnp.maximum(t, INT_MIN + 1), eq_before
    eq = keys == t
    eq_f = jnp.where(eq, 1.0, 0.0).astype(BF16)
    rank = eq_before + _dot(eq_f, upper)
    need = (topk - n_gt).astype(F32)
    sel = ((keys > t) | (eq & (rank < need))) & (keys > INT_MIN)
    return sel, eq_before + jnp.sum(eq_f.astype(F32), axis=1, keepdims=True)


def _init_softmax(m_scr, l_scr, acc_scr):
    m_scr[...] = jnp.full(m_scr.shape, NEG_BIG, F32)
    l_scr[...] = jnp.zeros(l_scr.shape, F32)
    acc_scr[...] = jnp.zeros(acc_scr.shape, F32)


def _finish_softmax(o_ref, l_scr, acc_scr):
    for h in range(DSA_HEADS):
        hs = slice(h * DSA_DH, (h + 1) * DSA_DH)
        o_ref[:, hs] = (acc_scr[:, hs] / l_scr[h]).astype(o_ref.dtype)


def _dsa_prompt_kernel(q_ref, iq_ref, g2_ref, k_ref, v_ref, iklo_ref, ikhi_ref, upper_ref,
                       o_ref, key_scr, wb_scr, m_scr, l_scr, acc_scr, *, bq, bk, topk):
    qb = pl.program_id(0)
    q0 = qb * bq
    n_tiles = (q0 + bq + bk - 1) // bk
    _fill_head_weights(wb_scr, g2_ref, bq)

    def tile_keys(kt):
        k0 = pl.multiple_of(kt * bk, bk)
        return _index_keys(iq_ref, wb_scr, iklo_ref[pl.ds(k0, bk), :],
                           ikhi_ref[pl.ds(k0, bk), :], bk)

    def score_body(kt, carry):
        key_scr[kt] = tile_keys(kt)
        return carry

    lax.fori_loop(0, n_tiles - 1, score_body, 0)
    last = n_tiles - 1
    qpos = q0 + lax.broadcasted_iota(I32, (bq, bk), 0)
    kpos = last * bk + lax.broadcasted_iota(I32, (bq, bk), 1)
    shift = CHUNK.bit_length() - 1
    admissible = (kpos >> shift) <= (qpos >> shift)
    key_scr[last] = jnp.where(admissible, tile_keys(last), INT_MIN)

    load_tile = lambda kt: key_scr[kt]
    t, n_ge, n_gt = _kth_largest(load_tile, n_tiles, topk, bq, bk)
    tie_rows = jnp.where((n_ge > topk) & (t > INT_MIN), 1, 0)
    any_tie = jnp.max(tie_rows) > 0

    def attend(ties):
        _init_softmax(m_scr, l_scr, acc_scr)

        def body(kt, eq_before):
            k0 = pl.multiple_of(kt * bk, bk)
            upper = upper_ref[...] if ties else None
            sel, eq_before = _select(key_scr[kt], t, n_gt, eq_before, topk, upper, ties)
            kv_tile = lambda h: (k_ref[pl.ds(k0, bk), h * DSA_DH:(h + 1) * DSA_DH],
                                 v_ref[pl.ds(k0, bk), h * DSA_DH:(h + 1) * DSA_DH])
            _attend_tile(sel, q_ref, kv_tile, m_scr, l_scr, acc_scr)
            return eq_before

        lax.fori_loop(0, n_tiles, body, jnp.zeros((bq, 1), F32))
        _finish_softmax(o_ref, l_scr, acc_scr)

    @pl.when(jnp.logical_not(any_tie))
    def _():
        attend(False)

    @pl.when(any_tie)
    def _():
        attend(True)


def _upper_ones(n):
    r = jnp.arange(n)
    return (r[:, None] < r[None, :]).astype(BF16)


def _dsa_prompt(q, iq, g2, kb, vb, iklo, ikhi, bq, bk):
    t_len = q.shape[0]
    topk = min(IDX_TOPK_MAX, t_len // 4)
    upper = _upper_ones(bk)
    kern = functools.partial(_dsa_prompt_kernel, bq=bq, bk=bk, topk=topk)
    row = lambda w: pl.BlockSpec((bq, w), lambda i: (i, 0))
    full = lambda a: pl.BlockSpec(a.shape, lambda i: (0, 0))
    return pl.pallas_call(
        kern,
        grid=(t_len // bq,),
        in_specs=[row(DSA_WIDTH), row(IDX_HEADS * IDX_DIM), row(LANE),
                  full(kb), full(vb), full(iklo), full(ikhi), full(upper)],
        out_specs=row(DSA_WIDTH),
        out_shape=jax.ShapeDtypeStruct((t_len, DSA_WIDTH), BF16),
        scratch_shapes=[
            pltpu.VMEM((t_len // bk, bq, bk), I32),
            pltpu.VMEM((IDX_HEADS, bq, LANE), F32),
            pltpu.VMEM((DSA_HEADS, bq, 1), F32),
            pltpu.VMEM((DSA_HEADS, bq, 1), F32),
            pltpu.VMEM((bq, DSA_WIDTH), F32),
        ],
        compiler_params=_cparams(("arbitrary",)),
        name="dsa_prompt",
    )(q, iq, g2, kb, vb, iklo, ikhi, upper)


def _dsa_sample_kernel(q_ref, iq_ref, g2_ref, ck_ref, cv_ref, cilo_ref, cihi_ref,
                       nk_ref, nv_ref, nilo_ref, nihi_ref, upper_ref, o_ref,
                       key_scr, wb_scr, m_scr, l_scr, acc_scr, pk_scr, pv_scr, plo_scr, phi_scr,
                       *, bq, bk, past, topk):
    n_cache = past // bk
    sub = bk // LANE
    _fill_head_weights(wb_scr, g2_ref, bq)
    pk_scr[...] = jnp.zeros(pk_scr.shape, BF16)
    pv_scr[...] = jnp.zeros(pv_scr.shape, BF16)
    plo_scr[...] = jnp.zeros(plo_scr.shape, BF16)
    phi_scr[...] = jnp.zeros(phi_scr.shape, BF16)
    pk_scr[0:bq, :] = nk_ref[...]
    pv_scr[0:bq, :] = nv_ref[...]
    plo_scr[0:bq, :] = nilo_ref[...]
    phi_scr[0:bq, :] = nihi_ref[...]

    for kt in range(n_cache):
        keys = _index_keys(iq_ref, wb_scr, cilo_ref[0, kt * bk:(kt + 1) * bk, :],
                           cihi_ref[0, kt * bk:(kt + 1) * bk, :], bk)
        for c in range(sub):
            key_scr[kt * sub + c] = keys[:, c * LANE:(c + 1) * LANE]
    keys = _index_keys(iq_ref, wb_scr, plo_scr[...], phi_scr[...], LANE)
    real = lax.broadcasted_iota(I32, (bq, LANE), 1) < bq
    key_scr[n_cache * sub] = jnp.where(real, keys, INT_MIN)

    n_sub = n_cache * sub + 1
    t, n_ge, n_gt = _kth_largest(lambda i: key_scr[i], n_sub, topk, bq, LANE)
    tie_rows = jnp.where((n_ge > topk) & (t > INT_MIN), 1, 0)
    any_tie = jnp.max(tie_rows) > 0

    def attend(ties):
        _init_softmax(m_scr, l_scr, acc_scr)
        eq_before = jnp.zeros((bq, 1), F32)
        for kt in range(n_cache):
            keys = jnp.concatenate([key_scr[kt * sub + c] for c in range(sub)], axis=1)
            upper = upper_ref[...] if ties else None
            sel, eq_before = _select(keys, t, n_gt, eq_before, topk, upper, ties)
            kv_tile = lambda h, kt=kt: (
                ck_ref[0, kt * bk:(kt + 1) * bk, h * DSA_DH:(h + 1) * DSA_DH].astype(BF16),
                cv_ref[0, kt * bk:(kt + 1) * bk, h * DSA_DH:(h + 1) * DSA_DH].astype(BF16))
            _attend_tile(sel, q_ref, kv_tile, m_scr, l_scr, acc_scr)
        upper = upper_ref[0:LANE, 0:LANE] if ties else None
        sel, eq_before = _select(key_scr[n_cache * sub], t, n_gt, eq_before, topk, upper, ties)
        kv_tile = lambda h: (pk_scr[:, h * DSA_DH:(h + 1) * DSA_DH],
                             pv_scr[:, h * DSA_DH:(h + 1) * DSA_DH])
        _attend_tile(sel, q_ref, kv_tile, m_scr, l_scr, acc_scr)
        _finish_softmax(o_ref, l_scr, acc_scr)

    @pl.when(jnp.logical_not(any_tie))
    def _():
        attend(False)

    @pl.when(any_tie)
    def _():
        attend(True)


def _dsa_sample(q, iq, g2, cache_k, cache_v, cache_ilo, cache_ihi, kb, vb, iklo, ikhi, bq, bk):
    n_streams, past, _ = cache_k.shape
    assert past % CHUNK == 0 and bq <= CHUNK and past % bk == 0
    topk = min(IDX_TOPK_MAX, (past + bq) // 4)
    upper = _upper_ones(bk)
    kern = functools.partial(_dsa_sample_kernel, bq=bq, bk=bk, past=past, topk=topk)
    row = lambda w: pl.BlockSpec((bq, w), lambda s: (s, 0))
    cache = lambda w: pl.BlockSpec((1, past, w), lambda s: (s, 0, 0))
    n_sub = past // LANE + 1
    return pl.pallas_call(
        kern,
        grid=(n_streams,),
        in_specs=[row(DSA_WIDTH), row(IDX_HEADS * IDX_DIM), row(LANE),
                  cache(DSA_WIDTH), cache(DSA_WIDTH), cache(LANE), cache(LANE),
                  row(DSA_WIDTH), row(DSA_WIDTH), row(LANE), row(LANE),
                  pl.BlockSpec(upper.shape, lambda s: (0, 0))],
        out_specs=row(DSA_WIDTH),
        out_shape=jax.ShapeDtypeStruct((n_streams * bq, DSA_WIDTH), BF16),
        scratch_shapes=[
            pltpu.VMEM((n_sub, bq, LANE), I32),
            pltpu.VMEM((IDX_HEADS, bq, LANE), F32),
            pltpu.VMEM((DSA_HEADS, bq, 1), F32),
            pltpu.VMEM((DSA_HEADS, bq, 1), F32),
            pltpu.VMEM((bq, DSA_WIDTH), F32),
            pltpu.VMEM((LANE, DSA_WIDTH), BF16),
            pltpu.VMEM((LANE, DSA_WIDTH), BF16),
            pltpu.VMEM((LANE, LANE), BF16),
            pltpu.VMEM((LANE, LANE), BF16),
        ],
        compiler_params=_cparams(("parallel",)),
        name="dsa_sample",
    )(q, iq, g2, cache_k, cache_v, cache_ilo, cache_ihi, kb, vb, iklo, ikhi, upper)


def _mix_kernel(og_ref, gg_ref, od_ref, x_ref, gnw_ref, wo_ref, nfw_ref, rw_ref, rb_ref,
                h1_ref, xn_ref, re_ref, rg_ref):
    g = gg_ref[...].astype(F32)
    gate = g / (1.0 + jnp.exp(-g))
    acc = _dot(od_ref[...], wo_ref[GLA_WIDTH:GLA_WIDTH + DSA_WIDTH, :])
    for h in range(GLA_HEADS):
        hs = slice(h * GLA_DV, (h + 1) * GLA_DV)
        oh = og_ref[:, hs]
        ms = jnp.mean(oh * oh, axis=-1, keepdims=True)
        a = oh * lax.rsqrt(ms + EPS) * gnw_ref[...] * gate[:, hs]
        acc = acc + _dot(a.astype(BF16), wo_ref[hs, :])
    h1 = x_ref[...] + acc
    h1_ref[...] = h1
    ms = jnp.mean(h1 * h1, axis=-1, keepdims=True)
    xn = h1 * lax.rsqrt(ms + EPS) * nfw_ref[...]
    _rows_to_tiles(xn_ref, xn)
    xn_hi = xn.astype(BF16)
    xn_lo = (xn - xn_hi.astype(F32)).astype(BF16)
    logits = (_dot(xn_hi, rw_ref[0]) + _dot(xn_hi, rw_ref[1]) + _dot(xn_lo, rw_ref[0])
              + rb_ref[...])
    lane = lax.broadcasted_iota(I32, logits.shape, 1)
    lane_f = lane.astype(F32)
    work = logits
    e_out = jnp.zeros(logits.shape, I32)
    tops = []
    for k in range(TOP_K):
        mx = jnp.max(work, axis=1, keepdims=True)
        idx = jnp.min(jnp.where(work == mx, lane_f, float(LANE)), axis=1, keepdims=True)
        idx = idx.astype(I32)
        e_out = jnp.where(lane == k, idx, e_out)
        tops.append(mx)
        work = jnp.where(lane == idx, NEG_BIG, work)
    ex = [jnp.exp(v - tops[0]) for v in tops]
    den = ex[0] + ex[1] + ex[2] + ex[3]
    g_out = jnp.zeros(logits.shape, F32)
    for k in range(TOP_K):
        g_out = jnp.where(lane == k, ex[k] / den, g_out)
    re_ref[...] = e_out
    rg_ref[...] = g_out


def _mix_consts(gla_norm_w, w_out, norm_ffn_w, router_w, router_b):
    rw = jnp.zeros((D_MODEL, LANE), F32).at[:, 0:N_EXPERTS].set(router_w)
    rw_hi = rw.astype(BF16)
    rw_lo = (rw - rw_hi.astype(F32)).astype(BF16)
    rb = jnp.full((1, LANE), NEG_BIG, F32).at[0, 0:N_EXPERTS].set(router_b)
    return (gla_norm_w.reshape(1, GLA_DV), w_out.astype(BF16), norm_ffn_w.reshape(1, D_MODEL),
            jnp.stack([rw_hi, rw_lo]), rb)


def _mix(og, gla, od, x2d, consts, bm):
    n = x2d.shape[0]
    gnw, wo, nfw, rws, rb = consts
    row = lambda w: pl.BlockSpec((bm, w), lambda i: (i, 0))
    const = lambda a: pl.BlockSpec(a.shape, lambda i: (0,) * a.ndim)
    gg_col = (2 * GLA_QK + GLA_WIDTH) // GLA_WIDTH
    return pl.pallas_call(
        _mix_kernel,
        grid=(n // bm,),
        in_specs=[row(GLA_WIDTH), pl.BlockSpec((bm, GLA_WIDTH), lambda i: (i, gg_col)),
                  row(DSA_WIDTH), row(D_MODEL),
                  const(gnw), const(wo), const(nfw), const(rws), const(rb)],
        out_specs=[row(D_MODEL), pl.BlockSpec((bm, ROW_TILE, LANE), lambda i: (i, 0, 0)),
                   row(LANE), row(LANE)],
        out_shape=[jax.ShapeDtypeStruct((n, D_MODEL), F32),
                   jax.ShapeDtypeStruct((n, ROW_TILE, LANE), F32),
                   jax.ShapeDtypeStruct((n, LANE), I32), jax.ShapeDtypeStruct((n, LANE), F32)],
        compiler_params=_cparams(("parallel",)),
        name="mix",
    )(og, gla, od, x2d, gnw, wo, nfw, rws, rb)


MOE_ROWS = 256


def _route(top_e, br):
    flat_e = top_e.reshape(-1)
    n_pairs = flat_e.shape[0]
    order = jnp.argsort(flat_e)
    e_sorted = flat_e[order]
    counts = jnp.bincount(flat_e, length=N_EXPERTS)
    padded = (counts + br - 1) // br * br
    pad_end = jnp.cumsum(padded)
    pad_start = pad_end - padded
    start = jnp.cumsum(counts) - counts
    dest = (pad_start[e_sorted] + jnp.arange(n_pairs, dtype=I32) - start[e_sorted]).astype(I32)
    nb = -(-(n_pairs + N_EXPERTS * (br - 1)) // br)
    row_tok = jnp.zeros((nb * br,), I32).at[dest].set((order // TOP_K).astype(I32))
    pos = jnp.zeros((n_pairs,), I32).at[order].set(dest)
    n_used = (pad_end[-1] // br).astype(I32)
    blk = jnp.arange(nb, dtype=I32)
    block_e = jnp.searchsorted(pad_end, blk * br, side="right").astype(I32)
    block_e = jnp.minimum(block_e, N_EXPERTS - 1)
    block_e = jnp.where(blk < n_used, block_e, block_e[jnp.maximum(n_used - 1, 0)])
    return row_tok.reshape(nb, 1, br), pos, block_e, n_used.reshape(1)


def _rows_to_tiles(ref, x):
    for j in range(ROW_TILE):
        ref[:, j, :] = x[:, j * LANE:(j + 1) * LANE]


def _rows_from_tiles(ref):
    return jnp.concatenate([ref[:, j, :] for j in range(ROW_TILE)], axis=1)


def _moe_kernel(be_ref, nu_ref, tok_ref, tokn_ref, x_hbm, wgu_ref, bgu_ref, wd_ref, bdn_ref,
                y_ref, xg, sem, wgu_bf, wd_bf, *, br):
    i = pl.program_id(0)
    n_used = nu_ref[0]
    slot = lax.rem(i, 2)

    def issue(tref, s):
        def body(r, c):
            tok = tref[0, 0, r]
            pltpu.make_async_copy(x_hbm.at[tok], xg.at[s, r], sem.at[s]).start()
            return c
        lax.fori_loop(0, br, body, 0)

    @pl.when(i == 0)
    def _():
        issue(tok_ref, 0)

    @pl.when(i + 1 < n_used)
    def _():
        issue(tokn_ref, 1 - slot)

    @pl.when(i < n_used)
    def _():
        pltpu.make_async_copy(xg.at[slot], xg.at[slot], sem.at[slot]).wait()
        e = be_ref[i]
        prev = be_ref[jnp.maximum(i - 1, 0)]

        @pl.when((i == 0) | (e != prev))
        def _():
            wgu_bf[...] = wgu_ref[0].astype(BF16)
            wd_bf[...] = wd_ref[0].astype(BF16)

        xb = _rows_from_tiles(xg.at[slot]).astype(BF16)
        gu = _dot(xb, wgu_bf[...]) + bgu_ref[0]
        g_lin = jnp.minimum(gu[:, 0:D_FF], SWIGLU_LIMIT)
        u_lin = jnp.clip(gu[:, D_FF:2 * D_FF], -SWIGLU_LIMIT, SWIGLU_LIMIT)
        act = g_lin / (1.0 + jnp.exp(-SWIGLU_ALPHA * g_lin)) * (u_lin + 1.0)
        y = _dot(act.astype(BF16), wd_bf[...]) + bdn_ref[0]
        _rows_to_tiles(y_ref, y)

    @pl.when(i >= n_used)
    def _():
        y_ref[...] = jnp.zeros(y_ref.shape, y_ref.dtype)


def _moe(xn, row_tok, block_e, n_used, w_gu, b_gu, w_down, b_down, br):
    nb = row_tok.shape[0]
    kern = functools.partial(_moe_kernel, br=br)
    nxt = lambda i, be, nu: (jnp.minimum(i + 1, nb - 1), 0, 0)
    grid_spec = pltpu.PrefetchScalarGridSpec(
        num_scalar_prefetch=2,
        grid=(nb,),
        in_specs=[
            pl.BlockSpec((1, 1, br), lambda i, be, nu: (i, 0, 0), memory_space=pltpu.SMEM),
            pl.BlockSpec((1, 1, br), nxt, memory_space=pltpu.SMEM),
            pl.BlockSpec(memory_space=pl.ANY),
            pl.BlockSpec((1, D_MODEL, 2 * D_FF), lambda i, be, nu: (be[i], 0, 0)),
            pl.BlockSpec((1, 1, 2 * D_FF), lambda i, be, nu: (be[i], 0, 0)),
            pl.BlockSpec((1, D_FF, D_MODEL), lambda i, be, nu: (be[i], 0, 0)),
            pl.BlockSpec((1, 1, D_MODEL), lambda i, be, nu: (be[i], 0, 0)),
        ],
        out_specs=pl.BlockSpec((br, ROW_TILE, LANE), lambda i, be, nu: (i, 0, 0)),
        scratch_shapes=[
            pltpu.VMEM((2, br, ROW_TILE, LANE), F32),
            pltpu.SemaphoreType.DMA((2,)),
            pltpu.VMEM((D_MODEL, 2 * D_FF), BF16),
            pltpu.VMEM((D_FF, D_MODEL), BF16),
        ],
    )
    return pl.pallas_call(
        kern,
        grid_spec=grid_spec,
        out_shape=jax.ShapeDtypeStruct((nb * br, ROW_TILE, LANE), F32),
        compiler_params=_cparams(("arbitrary",)),
        name="moe",
    )(block_e, n_used, row_tok, row_tok, xn, w_gu, b_gu.reshape(N_EXPERTS, 1, 2 * D_FF),
      w_down, b_down.reshape(N_EXPERTS, 1, D_MODEL))


def _combine_kernel(pos_ref, posn_ref, y_hbm, h1_ref, rg_ref, fw_ref, o_ref, yg, sem, *, bt):
    i = pl.program_id(0)
    nsteps = pl.num_programs(0)
    slot = lax.rem(i, 2)

    def issue(pref, s):
        for k in range(TOP_K):
            def body(r, c):
                p = pref[0, 0, r * TOP_K + k]
                pltpu.make_async_copy(y_hbm.at[p], yg.at[s, k * bt + r], sem.at[s]).start()
                return c
            lax.fori_loop(0, bt, body, 0)

    @pl.when(i == 0)
    def _():
        issue(pos_ref, 0)

    @pl.when(i + 1 < nsteps)
    def _():
        issue(posn_ref, 1 - slot)

    pltpu.make_async_copy(yg.at[slot], yg.at[slot], sem.at[slot]).wait()
    acc = h1_ref[...]
    for k in range(TOP_K):
        acc = acc + rg_ref[:, k:k + 1] * _rows_from_tiles(yg.at[slot, k * bt:(k + 1) * bt])
    ms = jnp.mean(acc * acc, axis=-1, keepdims=True)
    o_ref[...] = acc * lax.rsqrt(ms + EPS) * fw_ref[...]


def _combine(y_rows, pos, h1, rg, final_w, bt):
    n = h1.shape[0]
    nsteps = n // bt
    pos3 = pos.reshape(nsteps, 1, bt * TOP_K)
    kern = functools.partial(_combine_kernel, bt=bt)
    row = lambda w: pl.BlockSpec((bt, w), lambda i: (i, 0))
    return pl.pallas_call(
        kern,
        grid=(nsteps,),
        in_specs=[
            pl.BlockSpec((1, 1, bt * TOP_K), lambda i: (i, 0, 0), memory_space=pltpu.SMEM),
            pl.BlockSpec((1, 1, bt * TOP_K), lambda i: (jnp.minimum(i + 1, nsteps - 1), 0, 0),
                         memory_space=pltpu.SMEM),
            pl.BlockSpec(memory_space=pl.ANY),
            row(D_MODEL), row(LANE),
            pl.BlockSpec((1, D_MODEL), lambda i: (0, 0)),
        ],
        out_specs=row(D_MODEL),
        out_shape=jax.ShapeDtypeStruct((n, D_MODEL), F32),
        scratch_shapes=[pltpu.VMEM((2, TOP_K * bt, ROW_TILE, LANE), F32),
                        pltpu.SemaphoreType.DMA((2,))],
        compiler_params=_cparams(("arbitrary",)),
        name="combine",
    )(pos3, pos3, y_rows, h1, rg, final_w.reshape(1, D_MODEL))


def kernel(x_prompt, x_sample, cache_k, cache_v, cache_k_idx, state_gla, norm_mix_w, w_in,
           w_gla_a2, b_gla_a2, gla_norm_w, w_out, norm_ffn_w, router_w, router_b, w_gu, b_gu,
           w_down, b_down, norm_final_w):
    b_p, t_p, _ = x_prompt.shape
    b_s, t_s, _ = x_sample.shape
    past = cache_k.shape[2]
    assert b_p == 1 and norm_mix_w.shape[0] == 1
    l = 0
    pw = _proj_weights(norm_mix_w[l], w_in[l])
    gconsts = _gla_consts(w_gla_a2[l], b_gla_a2[l])

    pp = _project(x_prompt.reshape(t_p, D_MODEL), pw, 512)
    gla_p, q_p, kf_p, vf_p, kb_p, vb_p, iq_p, iklo_p, ikhi_p, ikf_p, g2_p = pp
    s0 = jnp.zeros((1, GLA_WIDTH, GLA_QK), F32)
    og_p, st_p = _gla(gla_p, g2_p, s0, gconsts, 1, t_p, 512, 16)
    od_p = _dsa_prompt(q_p, iq_p, g2_p, kb_p, vb_p, iklo_p, ikhi_p, 128, 512)

    ps = _project(x_sample.reshape(b_s * t_s, D_MODEL), pw, 512)
    gla_s, q_s, kf_s, vf_s, kb_s, vb_s, iq_s, iklo_s, ikhi_s, ikf_s, g2_s = ps
    og_s, st_s = _gla(gla_s, g2_s, _state_to_kernel(state_gla[l]), gconsts, b_s, t_s, t_s, 16)
    zc = jnp.zeros((b_s, past, IDX_DIM), BF16)
    cik = cache_k_idx[l].astype(BF16)
    cilo = jnp.concatenate([cik, zc], axis=2)
    cihi = jnp.concatenate([zc, cik], axis=2)
    od_s = _dsa_sample(q_s, iq_s, g2_s, cache_k[l].reshape(b_s, past, DSA_WIDTH),
                       cache_v[l].reshape(b_s, past, DSA_WIDTH), cilo, cihi,
                       kb_s, vb_s, iklo_s, ikhi_s, t_s, 512)

    mconsts = _mix_consts(gla_norm_w[l], w_out[l], norm_ffn_w[l], router_w[l], router_b[l])
    n_s = b_s * t_s
    h1_p, xn_p, re_p, rg_p = _mix(og_p, gla_p, od_p, x_prompt.reshape(t_p, D_MODEL), mconsts, 512)
    h1_s, xn_s, re_s, rg_s = _mix(og_s, gla_s, od_s, x_sample.reshape(n_s, D_MODEL), mconsts, 512)
    xn = jnp.concatenate([xn_p, xn_s], axis=0)
    top_e = jnp.concatenate([re_p[:, 0:TOP_K], re_s[:, 0:TOP_K]], axis=0)
    row_tok, pos, block_e, n_used = _route(top_e, MOE_ROWS)
    y_rows = _moe(xn, row_tok, block_e, n_used, w_gu[l], b_gu[l], w_down[l], b_down[l], MOE_ROWS)
    y_p = _combine(y_rows, pos[0:t_p * TOP_K], h1_p, rg_p, norm_final_w, 256)
    y_s = _combine(y_rows, pos[t_p * TOP_K:], h1_s, rg_s, norm_final_w, 256)
    y_p = y_p.reshape(x_prompt.shape)
    y_s = y_s.reshape(x_sample.shape)
    return (y_p, y_s,
            kf_p.reshape(1, 1, t_p, DSA_HEADS, DSA_DH), vf_p.reshape(1, 1, t_p, DSA_HEADS, DSA_DH),
            ikf_p.reshape(1, 1, t_p, IDX_DIM), _state_from_kernel(st_p)[None],
            kf_s.reshape(1, b_s, t_s, DSA_HEADS, DSA_DH), vf_s.reshape(1, b_s, t_s, DSA_HEADS, DSA_DH),
            ikf_s.reshape(1, b_s, t_s, IDX_DIM), _state_from_kernel(st_s)[None])
```

```python
import functools

import jax
import jax.numpy as jnp
from jax import lax
from jax.experimental import pallas as pl
from jax.experimental.pallas import tpu as pltpu

F32 = jnp.float32
BF16 = jnp.bfloat16
I32 = jnp.int32

D_MODEL = 1024
CHUNK = 64
GLA_HEADS = 4
GLA_DK = 64
GLA_DV = 128
GLA_QK = GLA_HEADS * GLA_DK
GLA_WIDTH = GLA_HEADS * GLA_DV
GLA_GATE_RANK = 16
GLA_GATE_TAU = 16.0
DSA_HEADS = 4
DSA_DH = 128
DSA_WIDTH = DSA_HEADS * DSA_DH
IDX_HEADS = 8
IDX_DIM = 64
IDX_TOPK_MAX = 256
N_EXPERTS = 32
TOP_K = 4
D_FF = 1024
SWIGLU_ALPHA = 1.702
SWIGLU_LIMIT = 7.0
EPS = 1e-6

_OFF_GLA = 0
_W_GLA = 2 * GLA_QK + 2 * GLA_WIDTH
_OFF_GA = _OFF_GLA + _W_GLA
_OFF_DSA = _OFF_GA + GLA_GATE_RANK
_W_DSA = 3 * DSA_WIDTH + IDX_HEADS * IDX_DIM
_OFF_IK = _OFF_DSA + _W_DSA
_OFF_IW = _OFF_IK + IDX_DIM
LANE = 128
SUBLANE = 8
ROW_TILE = D_MODEL // LANE
assert ROW_TILE == SUBLANE
LOG2E = 1.4426950408889634
NEG_BIG = -1e30
VMEM_LIMIT = 56 * 1024 * 1024


def _cparams(sem):
    return pltpu.CompilerParams(dimension_semantics=sem, vmem_limit_bytes=VMEM_LIMIT)


def _dot(a, b):
    return jnp.dot(a, b, preferred_element_type=F32)


def _dot_nt(a, b):
    return lax.dot_general(a, b, (((1,), (1,)), ((), ())), preferred_element_type=F32)


def _dot_tn(a, b):
    return lax.dot_general(a, b, (((0,), (0,)), ((), ())), preferred_element_type=F32)


def _split3(x):
    hi = x.astype(BF16)
    r1 = x - hi.astype(F32)
    mid = r1.astype(BF16)
    lo = (r1 - mid.astype(F32)).astype(BF16)
    return hi, mid, lo


def _proj_kernel(x_ref, nw_ref, wg_ref, wd_ref, ws_ref,
                 gla_ref, q_ref, kf_ref, vf_ref, kb_ref, vb_ref, iq_ref,
                 iklo_ref, ikhi_ref, ikf_ref, g2_ref):
    x = x_ref[...]
    ms = jnp.mean(x * x, axis=-1, keepdims=True)
    xn = (x * lax.rsqrt(ms + EPS) * nw_ref[...]).astype(BF16)
    gla_ref[...] = _dot(xn, wg_ref[...]).astype(BF16)
    W = DSA_WIDTH
    dq = _dot(xn, wd_ref[:, 0:W])
    q_ref[...] = (dq * (DSA_DH ** -0.5 * LOG2E)).astype(BF16)
    dk = _dot(xn, wd_ref[:, W:2 * W])
    kf_ref[...] = dk
    kb_ref[...] = dk.astype(BF16)
    dv = _dot(xn, wd_ref[:, 2 * W:3 * W])
    vf_ref[...] = dv
    vb_ref[...] = dv.astype(BF16)
    iq_ref[...] = _dot(xn, wd_ref[:, 3 * W:4 * W]).astype(BF16)
    sm = _dot(xn, ws_ref[...])
    iklo_ref[...] = sm[:, 0:LANE].astype(BF16)
    ikhi_ref[...] = sm[:, LANE:2 * LANE].astype(BF16)
    ikf_ref[...] = sm[:, 0:IDX_DIM]
    g2_ref[...] = sm[:, 2 * LANE:3 * LANE]


def _proj_weights(norm_w, w_in):
    wg = w_in[:, _OFF_GLA:_OFF_GLA + _W_GLA].astype(BF16)
    wd = w_in[:, _OFF_DSA:_OFF_DSA + _W_DSA].astype(BF16)
    ik = w_in[:, _OFF_IK:_OFF_IK + IDX_DIM]
    z64 = jnp.zeros((D_MODEL, IDX_DIM), F32)
    ga = w_in[:, _OFF_GA:_OFF_GA + GLA_GATE_RANK]
    iw = w_in[:, _OFF_IW:_OFF_IW + IDX_HEADS]
    zpad = jnp.zeros((D_MODEL, LANE - GLA_GATE_RANK - IDX_HEADS), F32)
    ws = jnp.concatenate([ik, z64, z64, ik, ga, iw, zpad], axis=1).astype(BF16)
    return norm_w.reshape(1, D_MODEL), wg, wd, ws


def _project(x2d, pw, bm):
    n = x2d.shape[0]
    nw, wg, wd, ws = pw
    row = lambda w: pl.BlockSpec((bm, w), lambda i: (i, 0))
    full = lambda a: pl.BlockSpec(a.shape, lambda i: (0, 0))
    outs = [(_W_GLA, BF16), (DSA_WIDTH, BF16), (DSA_WIDTH, F32), (DSA_WIDTH, F32),
            (DSA_WIDTH, BF16), (DSA_WIDTH, BF16), (IDX_HEADS * IDX_DIM, BF16),
            (LANE, BF16), (LANE, BF16), (IDX_DIM, F32), (LANE, F32)]
    return pl.pallas_call(
        _proj_kernel,
        grid=(n // bm,),
        in_specs=[row(D_MODEL), full(nw), full(wg), full(wd), full(ws)],
        out_specs=[row(w) for w, _ in outs],
        out_shape=[jax.ShapeDtypeStruct((n, w), dt) for w, dt in outs],
        compiler_params=_cparams(("parallel",)),
        name="proj",
    )(x2d, nw, wg, wd, ws)


def _proj_t_kernel(x_ref, nw_ref, wg_ref, wkv_ref, ws_ref, wt_ref, wst_ref,
                   gla_ref, kf_ref, vf_ref, kb_ref, ikf_ref, ikb_ref, g2_ref,
                   qt_ref, iqt_ref, vt_ref, g2t_ref):
    x = x_ref[...]
    ms = jnp.mean(x * x, axis=-1, keepdims=True)
    xn = (x * lax.rsqrt(ms + EPS) * nw_ref[...]).astype(BF16)
    gla_ref[...] = _dot(xn, wg_ref[...]).astype(BF16)
    W = DSA_WIDTH
    dk = _dot(xn, wkv_ref[:, 0:W])
    dv = _dot(xn, wkv_ref[:, W:2 * W])
    kb_ref[...] = dk.astype(BF16)
    for h in range(DSA_HEADS):
        kf_ref[:, h, :] = dk[:, h * DSA_DH:(h + 1) * DSA_DH]
        vf_ref[:, h, :] = dv[:, h * DSA_DH:(h + 1) * DSA_DH]
    sm = _dot(xn, ws_ref[...])
    ikb_ref[...] = sm[:, 0:LANE].astype(BF16)
    ikf_ref[...] = sm[:, 0:IDX_DIM]
    g2_ref[...] = sm[:, LANE:2 * LANE]
    qt_ref[...] = (_dot_nt(wt_ref[0:W, :], xn) * (DSA_DH ** -0.5 * LOG2E)).astype(BF16)
    vt_ref[0] = _dot_nt(wt_ref[W:2 * W, :], xn).astype(BF16)
    iqt_ref[...] = _dot_nt(wt_ref[2 * W:3 * W, :], xn).astype(BF16)
    g2t_ref[...] = _dot_nt(wst_ref[...], xn)


def _proj_t_weights(norm_w, w_in):
    wg = w_in[:, _OFF_GLA:_OFF_GLA + _W_GLA].astype(BF16)
    W = DSA_WIDTH
    wq = w_in[:, _OFF_DSA:_OFF_DSA + W]
    wkv = w_in[:, _OFF_DSA + W:_OFF_DSA + 3 * W]
    wv = w_in[:, _OFF_DSA + 2 * W:_OFF_DSA + 3 * W]
    wiq = w_in[:, _OFF_DSA + 3 * W:_OFF_DSA + 4 * W]
    ik = w_in[:, _OFF_IK:_OFF_IK + IDX_DIM]
    z64 = jnp.zeros((D_MODEL, IDX_DIM), F32)
    ga = w_in[:, _OFF_GA:_OFF_GA + GLA_GATE_RANK]
    iw = w_in[:, _OFF_IW:_OFF_IW + IDX_HEADS]
    zpad = jnp.zeros((D_MODEL, LANE - GLA_GATE_RANK - IDX_HEADS), F32)
    g2w = jnp.concatenate([ga, iw, zpad], axis=1)
    ws = jnp.concatenate([ik, z64, g2w], axis=1).astype(BF16)
    wt = jnp.concatenate([wq, wv, wiq], axis=1).T.astype(BF16)
    return norm_w.reshape(1, D_MODEL), wg, wkv.astype(BF16), ws, wt, g2w.T.astype(BF16)


def _project_t(x2d, pw, bm):
    n = x2d.shape[0]
    row = lambda w: pl.BlockSpec((bm, w), lambda i: (i, 0))
    row3 = pl.BlockSpec((bm, DSA_HEADS, DSA_DH), lambda i: (i, 0, 0))
    col = lambda h: pl.BlockSpec((h, bm), lambda i: (0, i))
    full = lambda a: pl.BlockSpec(a.shape, lambda i: (0, 0))
    out_specs = [row(_W_GLA), row3, row3, row(DSA_WIDTH), row(IDX_DIM), row(LANE), row(LANE),
                 col(DSA_WIDTH), col(IDX_HEADS * IDX_DIM),
                 pl.BlockSpec((1, DSA_WIDTH, bm), lambda i: (i, 0, 0)), col(LANE)]
    out_shape = [jax.ShapeDtypeStruct((n, _W_GLA), BF16),
                 jax.ShapeDtypeStruct((n, DSA_HEADS, DSA_DH), F32),
                 jax.ShapeDtypeStruct((n, DSA_HEADS, DSA_DH), F32),
                 jax.ShapeDtypeStruct((n, DSA_WIDTH), BF16),
                 jax.ShapeDtypeStruct((n, IDX_DIM), F32),
                 jax.ShapeDtypeStruct((n, LANE), BF16),
                 jax.ShapeDtypeStruct((n, LANE), F32),
                 jax.ShapeDtypeStruct((DSA_WIDTH, n), BF16),
                 jax.ShapeDtypeStruct((IDX_HEADS * IDX_DIM, n), BF16),
                 jax.ShapeDtypeStruct((n // bm, DSA_WIDTH, bm), BF16),
                 jax.ShapeDtypeStruct((LANE, n), F32)]
    return pl.pallas_call(
        _proj_t_kernel,
        grid=(n // bm,),
        in_specs=[row(D_MODEL)] + [full(a) for a in pw],
        out_specs=out_specs,
        out_shape=out_shape,
        compiler_params=_cparams(("parallel",)),
        name="proj_t",
    )(x2d, *pw)


def _log_sigmoid(x):
    return jnp.minimum(x, 0.0) - jnp.log1p(jnp.exp(-jnp.abs(x)))


def _gla_kernel(gla_ref, g2_ref, w2_ref, b2_ref, seg_ref, bd_ref, s0_ref,
                o_ref, sT_ref,
                st_scr, kpad, bpad, vpad, qt_scr, kt_scr, dec_scr, oi_scr,
                *, bt, c):
    t = pl.program_id(1)

    @pl.when(t == 0)
    def _():
        st_scr[...] = s0_ref[0]

    q = gla_ref[:, 0:GLA_QK].astype(F32) * (GLA_DK ** -0.5)
    k = gla_ref[:, GLA_QK:2 * GLA_QK].astype(F32)
    vb = gla_ref[:, 2 * GLA_QK:2 * GLA_QK + GLA_WIDTH]
    v = vb.astype(F32)

    ga = g2_ref[...]
    ga_hi = ga.astype(BF16)
    ga_lo = (ga - ga_hi.astype(F32)).astype(BF16)
    logit = (_dot(ga_hi, w2_ref[0]) + _dot(ga_hi, w2_ref[1]) + _dot(ga_lo, w2_ref[0])
             + b2_ref[...])
    lg = _log_sigmoid(logit) * (1.0 / GLA_GATE_TAU)

    shift = c.bit_length() - 1
    row = lax.broadcasted_iota(I32, (bt, bt), 0)
    col = lax.broadcasted_iota(I32, (bt, bt), 1)
    same = (row >> shift) == (col >> shift)
    tri = jnp.where(same & (col <= row), 1.0, 0.0).astype(BF16)
    last = jnp.where(same & ((col & (c - 1)) == c - 1), 1.0, 0.0).astype(BF16)
    l0, l1, l2 = _split3(lg)
    b = _dot(tri, l0) + _dot(tri, l1) + _dot(tri, l2)
    b0, b1, b2s = _split3(b)
    bl = _dot(last, b0) + _dot(last, b1) + _dot(last, b2s)

    zpad = jnp.zeros((c, GLA_QK), F32)
    kpad[0:c, :] = zpad
    bpad[0:c, :] = zpad
    vpad[0:c, :] = jnp.zeros((c, GLA_WIDTH), F32)
    kpad[c:c + bt, :] = k
    bpad[c:c + bt, :] = b
    vpad[c:c + bt, :] = v
    pos = lax.broadcasted_iota(I32, (bt, GLA_QK), 0) & (c - 1)
    seg = seg_ref[...]
    o_intra = jnp.zeros((bt, GLA_WIDTH), F32)
    for d in range(c):
        ks = kpad[c - d:c - d + bt, :]
        bs = bpad[c - d:c - d + bt, :]
        vs = vpad[c - d:c - d + bt, :]
        z = q * ks * jnp.exp(jnp.minimum(b - bs, 0.0))
        z = jnp.where(pos >= d, z, 0.0)
        o_intra = o_intra + _dot(z.astype(BF16), seg) * vs
    oi_scr[...] = o_intra

    qt_scr[...] = q * jnp.exp(b)
    kt_scr[...] = k * jnp.exp(bl - b)
    dec_scr[...] = jnp.exp(bl)
    bd = bd_ref[...]

    def step(ci, carry):
        r0 = pl.multiple_of(ci * c, c)
        qc = qt_scr[pl.ds(r0, c), :].astype(BF16)
        kc = kt_scr[pl.ds(r0, c), :].astype(BF16)
        vc = vpad[pl.ds(r0 + c, c), :].astype(BF16)
        st = st_scr[...]
        o_ref[pl.ds(r0, c), :] = oi_scr[pl.ds(r0, c), :] + _dot_nt(qc, st.astype(BF16))
        dec = dec_scr[pl.ds(r0, 1), :]
        st_scr[...] = st * dec + _dot_tn(vc, kc) * bd
        return carry

    lax.fori_loop(0, bt // c, step, 0)

    @pl.when(t == pl.num_programs(1) - 1)
    def _():
        sT_ref[0] = st_scr[...]


def _gla_consts(w_gla_a2, b_gla_a2):
    w2 = jnp.zeros((LANE, GLA_QK), F32).at[0:GLA_GATE_RANK].set(w_gla_a2)
    w2_hi = w2.astype(BF16)
    w2_lo = (w2 - w2_hi.astype(F32)).astype(BF16)
    w2s = jnp.stack([w2_hi, w2_lo])
    hq = jnp.arange(GLA_QK) // GLA_DK
    hv = jnp.arange(GLA_WIDTH) // GLA_DV
    seg = (hq[:, None] == hv[None, :]).astype(BF16)
    bd = (hv[:, None] == hq[None, :]).astype(F32)
    return w2s, b_gla_a2.reshape(1, GLA_QK), seg, bd


def _state_to_kernel(s):
    n = s.shape[0]
    eye = jnp.eye(GLA_HEADS, dtype=s.dtype)
    t = jnp.swapaxes(s, 2, 3)[:, :, :, None, :] * eye[None, :, None, :, None]
    return t.reshape(n, GLA_WIDTH, GLA_QK)


def _state_from_kernel(t):
    n = t.shape[0]
    t5 = t.reshape(n, GLA_HEADS, GLA_DV, GLA_HEADS, GLA_DK)
    return jnp.einsum("shehd->shde", t5)


def _gla(gla, g2, s0, consts, n_streams, t_len, bt, c):
    w2s, b2, seg, bd = consts
    nb = t_len // bt
    kern = functools.partial(_gla_kernel, bt=bt, c=c)
    const = lambda a: pl.BlockSpec(a.shape, lambda s, t: (0,) * a.ndim)
    o, sT = pl.pallas_call(
        kern,
        grid=(n_streams, nb),
        in_specs=[
            pl.BlockSpec((bt, 2 * GLA_QK + GLA_WIDTH), lambda s, t: (s * nb + t, 0)),
            pl.BlockSpec((bt, LANE), lambda s, t: (s * nb + t, 0)),
            const(w2s), const(b2), const(seg), const(bd),
            pl.BlockSpec((1, GLA_WIDTH, GLA_QK), lambda s, t: (s, 0, 0)),
        ],
        out_specs=[
            pl.BlockSpec((bt, GLA_WIDTH), lambda s, t: (s * nb + t, 0)),
            pl.BlockSpec((1, GLA_WIDTH, GLA_QK), lambda s, t: (s, 0, 0)),
        ],
        out_shape=[
            jax.ShapeDtypeStruct((n_streams * t_len, GLA_WIDTH), F32),
            jax.ShapeDtypeStruct((n_streams, GLA_WIDTH, GLA_QK), F32),
        ],
        scratch_shapes=[
            pltpu.VMEM((GLA_WIDTH, GLA_QK), F32),
            pltpu.VMEM((bt + c, GLA_QK), F32),
            pltpu.VMEM((bt + c, GLA_QK), F32),
            pltpu.VMEM((bt + c, GLA_WIDTH), F32),
            pltpu.VMEM((bt, GLA_QK), F32),
            pltpu.VMEM((bt, GLA_QK), F32),
            pltpu.VMEM((bt, GLA_QK), F32),
            pltpu.VMEM((bt, GLA_WIDTH), F32),
        ],
        compiler_params=_cparams(("arbitrary", "arbitrary")),
        name="gla",
    )(gla, g2, w2s, b2, seg, bd, s0)
    return o, sT


INT_MIN = -2 ** 31
NEG_INF = float("-inf")
KEY_NEG_INF = -2139095041
F32_LOWEST = -3.4028234663852886e38
IDX_SCALE = IDX_DIM ** -0.5 * IDX_HEADS ** -0.5


def _fill_head_weights(wb_scr, g2_ref, bq):
    for h in range(IDX_HEADS):
        c0 = GLA_GATE_RANK + h
        col = g2_ref[:, c0:c0 + 1] * IDX_SCALE
        wb_scr[h] = jnp.broadcast_to(col, (bq, LANE))


def _index_scores(iq_ref, wb_scr, iklo, ikhi, bk):
    parts = [None] * (bk // LANE)
    for p in range(IDX_HEADS // 2):
        iqp = iq_ref[:, p * LANE:(p + 1) * LANE]
        y0 = jnp.maximum(_dot_nt(iqp, iklo), 0.0)
        y1 = jnp.maximum(_dot_nt(iqp, ikhi), 0.0)
        w0 = wb_scr[2 * p]
        w1 = wb_scr[2 * p + 1]
        for c in range(bk // LANE):
            t = y0[:, c * LANE:(c + 1) * LANE] * w0 + y1[:, c * LANE:(c + 1) * LANE] * w1
            parts[c] = t if parts[c] is None else parts[c] + t
    return jnp.concatenate(parts, axis=1)


def _key_to_float(key):
    key = jnp.maximum(key, KEY_NEG_INF)
    return pltpu.bitcast(key ^ ((key >> 31) & 0x7FFFFFFF), F32)


def _kth_largest(count, total, topk, shape):
    def bit_body(i, carry):
        prefix, n_ge = carry
        cand_u = prefix | jnp.left_shift(jnp.int32(1), 31 - i)
        cnt = count(_key_to_float(cand_u ^ INT_MIN), False)
        ok = cnt >= topk
        return jnp.where(ok, cand_u, prefix), jnp.where(ok, cnt, n_ge)

    init = (jnp.zeros(shape, I32), jnp.zeros(shape, I32) + total)
    prefix, n_ge = lax.fori_loop(0, 32, bit_body, init)
    t = _key_to_float(prefix ^ INT_MIN)
    return t, n_ge, count(t, True)


def _count_rows(load_tile, n_tiles, cand, strict, bq, bk):
    cand_b = jnp.broadcast_to(cand, (bq, LANE))

    def body(kt, acc):
        s = load_tile(kt)
        for c in range(bk // LANE):
            sc = s[:, c * LANE:(c + 1) * LANE]
            acc = acc + jnp.where(sc > cand_b if strict else sc >= cand_b, 1, 0)
        return acc

    acc = lax.fori_loop(0, n_tiles, body, jnp.zeros((bq, LANE), I32))
    return jnp.sum(acc, axis=1, keepdims=True)


def _attend_tile(sel, q_ref, kv_tile, m_scr, l_scr, acc_scr):
    for h in range(DSA_HEADS):
        hs = slice(h * DSA_DH, (h + 1) * DSA_DH)
        kh, vh = kv_tile(h)
        lg = jnp.where(sel, _dot_nt(q_ref[:, hs], kh), NEG_BIG)
        m_old = m_scr[h]
        m_new = jnp.maximum(m_old, jnp.max(lg, axis=1, keepdims=True))
        alpha = jnp.exp2(m_old - m_new)
        p = jnp.exp2(lg - m_new)
        l_scr[h] = alpha * l_scr[h] + jnp.sum(p, axis=1, keepdims=True)
        acc_scr[:, hs] = alpha * acc_scr[:, hs] + _dot(p.astype(BF16), vh)
        m_scr[h] = m_new


def _select(s, t, n_gt, eq_before, topk, upper, ties):
    if not ties:
        return s >= jnp.maximum(t, F32_LOWEST), eq_before
    eq = s == t
    eq_f = jnp.where(eq, 1.0, 0.0).astype(BF16)
    rank = eq_before + _dot(eq_f, upper)
    need = (topk - n_gt).astype(F32)
    sel = ((s > t) | (eq & (rank < need))) & (s > NEG_INF)
    return sel, eq_before + jnp.sum(eq_f.astype(F32), axis=1, keepdims=True)


def _init_softmax(m_scr, l_scr, acc_scr):
    m_scr[...] = jnp.full(m_scr.shape, NEG_BIG, F32)
    l_scr[...] = jnp.zeros(l_scr.shape, F32)
    acc_scr[...] = jnp.zeros(acc_scr.shape, F32)


def _finish_softmax(o_ref, l_scr, acc_scr):
    for h in range(DSA_HEADS):
        hs = slice(h * DSA_DH, (h + 1) * DSA_DH)
        o_ref[:, hs] = (acc_scr[:, hs] / l_scr[h]).astype(o_ref.dtype)


def _upper_ones(n):
    r = jnp.arange(n)
    return (r[:, None] < r[None, :]).astype(BF16)


DSA_BQ = 256
DSA_BK = 512

def _count_cols(sc_scr, n_tiles, cand, strict, bq, bk):
    cand_b = jnp.broadcast_to(cand, (SUBLANE, bq))
    n_acc = 4

    def body(kt, accs):
        accs = list(accs)
        for r in range(bk // SUBLANE):
            s = sc_scr[kt, r * SUBLANE:(r + 1) * SUBLANE, :]
            accs[r % n_acc] = accs[r % n_acc] + jnp.where(s > cand_b if strict else s >= cand_b, 1, 0)
        return tuple(accs)

    zero = jnp.zeros((SUBLANE, bq), I32)
    accs = lax.fori_loop(0, n_tiles, body, (zero,) * n_acc)
    return jnp.sum((accs[0] + accs[1]) + (accs[2] + accs[3]), axis=0, keepdims=True)


def _dsa_prompt_t_kernel(qt_ref, iqt_ref, g2t_ref, ik_ref, lower_ref, k_hbm, vt_hbm,
                         o_ref, sc_scr, kbuf, vbuf, sem, acc_scr,
                         *, bq, bk, topk):
    qb = pl.program_id(0)
    q0 = qb * bq
    n_tiles = (q0 + bq + bk - 1) // bk
    sub = LANE

    def tile_scores(kt, masked):
        for s in range(bk // sub):
            ik_s = ik_ref[kt, s * sub:(s + 1) * sub, 0:IDX_DIM]
            acc = None
            for h in range(IDX_HEADS):
                y = _dot(ik_s, iqt_ref[h * IDX_DIM:(h + 1) * IDX_DIM, :])
                w = g2t_ref[GLA_GATE_RANK + h:GLA_GATE_RANK + h + 1, :] * IDX_SCALE
                t = jnp.maximum(y, 0.0) * w
                acc = t if acc is None else acc + t
            if masked:
                kpos = kt * bk + s * sub + lax.broadcasted_iota(I32, (sub, bq), 0)
                qpos = q0 + lax.broadcasted_iota(I32, (sub, bq), 1)
                shift = CHUNK.bit_length() - 1
                acc = jnp.where((kpos >> shift) <= (qpos >> shift), acc, NEG_INF)
            sc_scr[kt, s * sub:(s + 1) * sub, :] = acc

    def score_body(kt, carry):
        tile_scores(kt, False)
        return carry

    lax.fori_loop(0, n_tiles - 1, score_body, 0)
    tile_scores(n_tiles - 1, True)

    count = lambda cand, strict: _count_cols(sc_scr, n_tiles, cand, strict, bq, bk)
    t, n_ge, n_gt = _kth_largest(count, n_tiles * bk, topk, (1, bq))
    tie_q = jnp.where((n_ge > topk) & (t > NEG_INF), 1, 0)
    any_tie = jnp.max(tie_q) > 0

    def copies(kt, slot, with_v):
        cs = [pltpu.make_async_copy(k_hbm.at[kt], kbuf.at[slot], sem.at[0, slot])]
        if with_v:
            cs.append(pltpu.make_async_copy(vt_hbm.at[kt], vbuf.at[slot], sem.at[1, slot]))
        return cs

    def stream(kt, with_v):
        slot = lax.rem(kt, 2)
        for c in copies(kt, slot, with_v):
            c.wait()

        @pl.when(kt + 1 < n_tiles)
        def _():
            for c in copies(kt + 1, 1 - slot, with_v):
                c.start()

        return slot

    def select(kt, eq_before, ties):
        s = sc_scr[kt]
        if not ties:
            return s >= jnp.maximum(t, F32_LOWEST), eq_before
        eq = s == t
        eq_b = jnp.where(eq, 1.0, 0.0).astype(BF16)
        rank = eq_before + _dot(lower_ref[...], eq_b)
        need = (topk - n_gt).astype(F32)
        sel = ((s > t) | (eq & (rank < need))) & (s > NEG_INF)
        return sel, eq_before + jnp.sum(eq_b.astype(F32), axis=0, keepdims=True)

    def logits(slot, h, sel):
        hs = slice(h * DSA_DH, (h + 1) * DSA_DH)
        return jnp.where(sel, _dot(kbuf[slot, :, hs], qt_ref[hs, :]), NEG_BIG)

    def attend(ties):
        zero = jnp.zeros((1, bq), F32)
        for c in copies(0, 0, False):
            c.start()

        def max_body(kt, carry):
            ms, eq_before = carry
            slot = stream(kt, False)
            sel, eq_before = select(kt, eq_before, ties)
            ms = tuple(jnp.maximum(ms[h], jnp.max(logits(slot, h, sel), axis=0, keepdims=True))
                       for h in range(DSA_HEADS))
            return ms, eq_before

        neg = jnp.full((1, bq), NEG_BIG, F32)
        ms, _ = lax.fori_loop(0, n_tiles, max_body, ((neg,) * DSA_HEADS, zero))

        acc_scr[...] = jnp.zeros(acc_scr.shape, F32)
        for c in copies(0, 0, True):
            c.start()

        def sum_body(kt, carry):
            ls, eq_before = carry
            slot = stream(kt, True)
            sel, eq_before = select(kt, eq_before, ties)
            new_ls = []
            for h in range(DSA_HEADS):
                hs = slice(h * DSA_DH, (h + 1) * DSA_DH)
                p = jnp.exp2(logits(slot, h, sel) - ms[h])
                new_ls.append(ls[h] + jnp.sum(p, axis=0, keepdims=True))
                acc_scr[hs, :] += _dot(vbuf[slot, hs, :], p.astype(BF16))
            return tuple(new_ls), eq_before

        ls, _ = lax.fori_loop(0, n_tiles, sum_body, ((zero,) * DSA_HEADS, zero))
        for h in range(DSA_HEADS):
            hs = slice(h * DSA_DH, (h + 1) * DSA_DH)
            acc_scr[hs, :] = acc_scr[hs, :] / ls[h]
        o_ref[...] = jnp.transpose(acc_scr[...]).astype(o_ref.dtype)

    @pl.when(jnp.logical_not(any_tie))
    def _():
        attend(False)

    @pl.when(any_tie)
    def _():
        attend(True)


def _dsa_prompt_t(qt, iqt, g2t, ikb, kb, vt3, bq, bk):
    t_len = qt.shape[1]
    n_kt = t_len // bk
    assert vt3.shape == (n_kt, DSA_WIDTH, bk) and bk % bq == 0
    topk = min(IDX_TOPK_MAX, t_len // 4)
    r = jnp.arange(bk)
    lower = (r[None, :] < r[:, None]).astype(BF16)
    kern = functools.partial(_dsa_prompt_t_kernel, bq=bq, bk=bk, topk=topk)
    col = lambda h: pl.BlockSpec((h, bq), lambda i: (0, i))
    ik3 = ikb.reshape(n_kt, bk, LANE)
    return pl.pallas_call(
        kern,
        grid=(t_len // bq,),
        in_specs=[col(DSA_WIDTH), col(IDX_HEADS * IDX_DIM), col(LANE),
                  pl.BlockSpec(ik3.shape, lambda i: (0, 0, 0)),
                  pl.BlockSpec(lower.shape, lambda i: (0, 0)),
                  pl.BlockSpec(memory_space=pl.ANY), pl.BlockSpec(memory_space=pl.ANY)],
        out_specs=pl.BlockSpec((bq, DSA_WIDTH), lambda i: (i, 0)),
        out_shape=jax.ShapeDtypeStruct((t_len, DSA_WIDTH), BF16),
        scratch_shapes=[
            pltpu.VMEM((n_kt, bk, bq), F32),
            pltpu.VMEM((2, bk, DSA_WIDTH), BF16),
            pltpu.VMEM((2, DSA_WIDTH, bk), BF16),
            pltpu.SemaphoreType.DMA((2, 2)),
            pltpu.VMEM((DSA_WIDTH, bq), F32),
        ],
        compiler_params=_cparams(("arbitrary",)),
        name="dsa_prompt_t",
    )(qt, iqt, g2t, ik3, lower, kb.reshape(n_kt, bk, DSA_WIDTH), vt3)


def _dsa_sample_kernel(q_ref, iq_ref, g2_ref, ck_ref, cv_ref, cilo_ref, cihi_ref,
                       nk_ref, nv_ref, nilo_ref, nihi_ref, upper_ref, o_ref,
                       sc_scr, wb_scr, m_scr, l_scr, acc_scr, pk_scr, pv_scr, plo_scr, phi_scr,
                       *, bq, bk, past, topk):
    n_cache = past // bk
    sub = bk // LANE
    _fill_head_weights(wb_scr, g2_ref, bq)
    pk_scr[...] = jnp.zeros(pk_scr.shape, BF16)
    pv_scr[...] = jnp.zeros(pv_scr.shape, BF16)
    plo_scr[...] = jnp.zeros(plo_scr.shape, BF16)
    phi_scr[...] = jnp.zeros(phi_scr.shape, BF16)
    pk_scr[0:bq, :] = nk_ref[...]
    pv_scr[0:bq, :] = nv_ref[...]
    plo_scr[0:bq, :] = nilo_ref[...]
    phi_scr[0:bq, :] = nihi_ref[...]

    for kt in range(n_cache):
        s = _index_scores(iq_ref, wb_scr, cilo_ref[0, kt * bk:(kt + 1) * bk, :],
                          cihi_ref[0, kt * bk:(kt + 1) * bk, :], bk)
        for c in range(sub):
            sc_scr[kt * sub + c] = s[:, c * LANE:(c + 1) * LANE]
    s = _index_scores(iq_ref, wb_scr, plo_scr[...], phi_scr[...], LANE)
    real = lax.broadcasted_iota(I32, (bq, LANE), 1) < bq
    sc_scr[n_cache * sub] = jnp.where(real, s, NEG_INF)

    n_sub = n_cache * sub + 1
    count = lambda cand, strict: _count_rows(lambda i: sc_scr[i], n_sub, cand, strict, bq, LANE)
    t, n_ge, n_gt = _kth_largest(count, n_sub * LANE, topk, (bq, 1))
    tie_rows = jnp.where((n_ge > topk) & (t > NEG_INF), 1, 0)
    any_tie = jnp.max(tie_rows) > 0

    def attend(ties):
        _init_softmax(m_scr, l_scr, acc_scr)
        eq_before = jnp.zeros((bq, 1), F32)
        for kt in range(n_cache):
            keys = jnp.concatenate([sc_scr[kt * sub + c] for c in range(sub)], axis=1)
            upper = upper_ref[...] if ties else None
            sel, eq_before = _select(keys, t, n_gt, eq_before, topk, upper, ties)
            kv_tile = lambda h, kt=kt: (
                ck_ref[0, kt * bk:(kt + 1) * bk, h, :].astype(BF16),
                cv_ref[0, kt * bk:(kt + 1) * bk, h, :].astype(BF16))
            _attend_tile(sel, q_ref, kv_tile, m_scr, l_scr, acc_scr)
        upper = upper_ref[0:LANE, 0:LANE] if ties else None
        sel, eq_before = _select(sc_scr[n_cache * sub], t, n_gt, eq_before, topk, upper, ties)
        kv_tile = lambda h: (pk_scr[:, h * DSA_DH:(h + 1) * DSA_DH],
                             pv_scr[:, h * DSA_DH:(h + 1) * DSA_DH])
        _attend_tile(sel, q_ref, kv_tile, m_scr, l_scr, acc_scr)
        _finish_softmax(o_ref, l_scr, acc_scr)

    @pl.when(jnp.logical_not(any_tie))
    def _():
        attend(False)

    @pl.when(any_tie)
    def _():
        attend(True)


def _dsa_sample(q, iq, g2, cache_k, cache_v, cache_ilo, cache_ihi, kb, vb, iklo, ikhi, bq, bk):
    n_streams, past = cache_k.shape[0:2]
    assert past % CHUNK == 0 and bq <= CHUNK and past % bk == 0
    topk = min(IDX_TOPK_MAX, (past + bq) // 4)
    upper = _upper_ones(bk)
    kern = functools.partial(_dsa_sample_kernel, bq=bq, bk=bk, past=past, topk=topk)
    row = lambda w: pl.BlockSpec((bq, w), lambda s: (s, 0))
    cache = lambda w: pl.BlockSpec((1, past, w), lambda s: (s, 0, 0))
    cache_kv = pl.BlockSpec((1, past, DSA_HEADS, DSA_DH), lambda s: (s, 0, 0, 0))
    n_sub = past // LANE + 1
    return pl.pallas_call(
        kern,
        grid=(n_streams,),
        in_specs=[row(DSA_WIDTH), row(IDX_HEADS * IDX_DIM), row(LANE),
                  cache_kv, cache_kv, cache(LANE), cache(LANE),
                  row(DSA_WIDTH), row(DSA_WIDTH), row(LANE), row(LANE),
                  pl.BlockSpec(upper.shape, lambda s: (0, 0))],
        out_specs=row(DSA_WIDTH),
        out_shape=jax.ShapeDtypeStruct((n_streams * bq, DSA_WIDTH), BF16),
        scratch_shapes=[
            pltpu.VMEM((n_sub, bq, LANE), F32),
            pltpu.VMEM((IDX_HEADS, bq, LANE), F32),
            pltpu.VMEM((DSA_HEADS, bq, 1), F32),
            pltpu.VMEM((DSA_HEADS, bq, 1), F32),
            pltpu.VMEM((bq, DSA_WIDTH), F32),
            pltpu.VMEM((LANE, DSA_WIDTH), BF16),
            pltpu.VMEM((LANE, DSA_WIDTH), BF16),
            pltpu.VMEM((LANE, LANE), BF16),
            pltpu.VMEM((LANE, LANE), BF16),
        ],
        compiler_params=_cparams(("parallel",)),
        name="dsa_sample",
    )(q, iq, g2, cache_k, cache_v, cache_ilo, cache_ihi, kb, vb, iklo, ikhi, upper)


def _mix_kernel(og_ref, gg_ref, od_ref, x_ref, gnw_ref, wo_ref, nfw_ref, rw_ref, rb_ref,
                h1_ref, xn_ref, re_ref, rg_ref, cnt_ref):
    g = gg_ref[...].astype(F32)
    gate = g / (1.0 + jnp.exp(-g))
    acc = _dot(od_ref[...], wo_ref[GLA_WIDTH:GLA_WIDTH + DSA_WIDTH, :])
    for h in range(GLA_HEADS):
        hs = slice(h * GLA_DV, (h + 1) * GLA_DV)
        oh = og_ref[:, hs]
        ms = jnp.mean(oh * oh, axis=-1, keepdims=True)
        a = oh * lax.rsqrt(ms + EPS) * gnw_ref[...] * gate[:, hs]
        acc = acc + _dot(a.astype(BF16), wo_ref[hs, :])
    h1 = x_ref[...] + acc
    h1_ref[...] = h1
    ms = jnp.mean(h1 * h1, axis=-1, keepdims=True)
    xn = h1 * lax.rsqrt(ms + EPS) * nfw_ref[...]
    _rows_to_tiles(xn_ref, xn)
    xn_hi = xn.astype(BF16)
    xn_lo = (xn - xn_hi.astype(F32)).astype(BF16)
    logits = (_dot(xn_hi, rw_ref[0]) + _dot(xn_hi, rw_ref[1]) + _dot(xn_lo, rw_ref[0])
              + rb_ref[...])
    lane = lax.broadcasted_iota(I32, logits.shape, 1)
    lane_f = lane.astype(F32)
    work = logits
    e_out = jnp.zeros(logits.shape, I32)
    tops, hots = [], []
    for k in range(TOP_K):
        mx = jnp.max(work, axis=1, keepdims=True)
        idx = jnp.min(jnp.where(work == mx, lane_f, float(LANE)), axis=1, keepdims=True)
        idx = idx.astype(I32)
        hot = lane == idx
        e_out = jnp.where(lane == k, idx, e_out)
        tops.append(mx)
        hots.append(hot)
        work = jnp.where(hot, NEG_BIG, work)
    ex = [jnp.exp(v - tops[0]) for v in tops]
    den = ex[0] + ex[1] + ex[2] + ex[3]
    g_out = jnp.zeros(logits.shape, F32)
    for k in range(TOP_K):
        g_out = jnp.where(lane == k, ex[k] / den, g_out)
    rg_ref[...] = g_out
    bm = logits.shape[0]
    earlier = (lax.broadcasted_iota(I32, (bm, bm), 1) < lax.broadcasted_iota(I32, (bm, bm), 0))
    earlier = jnp.where(earlier, 1.0, 0.0).astype(BF16)
    run = jnp.zeros((1, LANE), F32)
    for k in range(TOP_K):
        hot_f = jnp.where(hots[k], 1.0, 0.0)
        before = _dot(earlier, hot_f.astype(BF16)) + run
        rank = jnp.sum(jnp.where(hots[k], before, 0.0), axis=1, keepdims=True).astype(I32)
        e_out = jnp.where(lane == TOP_K + k, rank, e_out)
        run = run + jnp.sum(hot_f, axis=0, keepdims=True)
    re_ref[...] = e_out
    cnt_ref[0] = jnp.broadcast_to(run, (SUBLANE, LANE)).astype(I32)


def _mix_consts(gla_norm_w, w_out, norm_ffn_w, router_w, router_b):
    rw = jnp.zeros((D_MODEL, LANE), F32).at[:, 0:N_EXPERTS].set(router_w)
    rw_hi = rw.astype(BF16)
    rw_lo = (rw - rw_hi.astype(F32)).astype(BF16)
    rb = jnp.full((1, LANE), NEG_BIG, F32).at[0, 0:N_EXPERTS].set(router_b)
    return (gla_norm_w.reshape(1, GLA_DV), w_out.astype(BF16), norm_ffn_w.reshape(1, D_MODEL),
            jnp.stack([rw_hi, rw_lo]), rb)


def _mix(og, gla, od, x2d, consts, bm):
    n = x2d.shape[0]
    gnw, wo, nfw, rws, rb = consts
    row = lambda w: pl.BlockSpec((bm, w), lambda i: (i, 0))
    const = lambda a: pl.BlockSpec(a.shape, lambda i: (0,) * a.ndim)
    gg_col = (2 * GLA_QK + GLA_WIDTH) // GLA_WIDTH
    return pl.pallas_call(
        _mix_kernel,
        grid=(n // bm,),
        in_specs=[row(GLA_WIDTH), pl.BlockSpec((bm, GLA_WIDTH), lambda i: (i, gg_col)),
                  row(DSA_WIDTH), row(D_MODEL),
                  const(gnw), const(wo), const(nfw), const(rws), const(rb)],
        out_specs=[row(D_MODEL), pl.BlockSpec((bm, ROW_TILE, LANE), lambda i: (i, 0, 0)),
                   row(LANE), row(LANE), pl.BlockSpec((1, SUBLANE, LANE), lambda i: (i, 0, 0))],
        out_shape=[jax.ShapeDtypeStruct((n, D_MODEL), F32),
                   jax.ShapeDtypeStruct((n, ROW_TILE, LANE), F32),
                   jax.ShapeDtypeStruct((n, LANE), I32), jax.ShapeDtypeStruct((n, LANE), F32),
                   jax.ShapeDtypeStruct((n // bm, SUBLANE, LANE), I32)],
        compiler_params=_cparams(("parallel",)),
        name="mix",
    )(og, gla, od, x2d, gnw, wo, nfw, rws, rb)


MOE_ROWS = 256
TOKEN_BLOCK = 512


def _route_tables(counts, br, n_pairs):
    c = counts[:, 0, :]
    tot = jnp.sum(c, axis=0)
    padded = (tot + br - 1) // br * br
    pad_end = jnp.cumsum(padded)
    pad_start = pad_end - padded
    base = pad_start[None, :] + jnp.cumsum(c, axis=0) - c
    base = jnp.broadcast_to(base[:, None, :], counts.shape).astype(I32)
    nb = -(-(n_pairs + N_EXPERTS * (br - 1)) // br)
    n_used = (pad_end[N_EXPERTS - 1] // br).astype(I32)
    blk = jnp.arange(nb, dtype=I32)
    last = jnp.minimum(blk, n_used - 1) * br
    block_e = jnp.sum((pad_end[None, 0:N_EXPERTS] <= last[:, None]).astype(I32), axis=1)
    ends = pad_end[0:N_EXPERTS].astype(I32)
    tots = jnp.concatenate([tot[0:N_EXPERTS].astype(I32), n_used.reshape(1)])
    return base, jnp.minimum(block_e, N_EXPERTS - 1), n_used.reshape(1), ends, tots


def _pos_kernel(re_ref, base_ref, pos_ref):
    re = re_ref[...]
    lane = lax.broadcasted_iota(I32, re.shape, 1)
    base_row = base_ref[0, 0:1, :].astype(F32)
    p = jnp.zeros(re.shape, F32)
    for k in range(TOP_K):
        hot = lane == re[:, k:k + 1]
        first = jnp.sum(jnp.where(hot, base_row, 0.0), axis=1, keepdims=True)
        p = jnp.where(lane == k, first + re[:, TOP_K + k:TOP_K + k + 1].astype(F32), p)
    pos_ref[0] = jnp.transpose(p)[0:SUBLANE, :].astype(I32)


def _positions(re, base, bm):
    n = re.shape[0]
    return pl.pallas_call(
        _pos_kernel,
        grid=(n // bm,),
        in_specs=[pl.BlockSpec((bm, LANE), lambda i: (i, 0)),
                  pl.BlockSpec((1, SUBLANE, LANE), lambda i: (i, 0, 0))],
        out_specs=pl.BlockSpec((1, SUBLANE, bm), lambda i: (i, 0, 0)),
        out_shape=jax.ShapeDtypeStruct((n // bm, SUBLANE, bm), I32),
        compiler_params=_cparams(("parallel",)),
        name="positions",
    )(re, base)


def _dispatch_kernel(end_ref, tot_ref, pos_ref, xa_ref, xb_ref, xr_hbm, zero_scr, sem,
                     *, bt, br, nsteps_a):
    i = pl.program_id(0)

    @pl.when(i == 0)
    def _():
        zero_scr[...] = jnp.zeros(zero_scr.shape, F32)
        n_used = tot_ref[N_EXPERTS]
        nb = xr_hbm.shape[0] // br

        def fills():
            for e in range(N_EXPERTS):
                yield tot_ref[e] > 0, end_ref[e] - br
            for j in range(N_EXPERTS):
                yield n_used + j < nb, jnp.minimum(n_used + j, nb - 1) * br

        for act in ("start", "wait"):
            for cond, row0 in fills():
                @pl.when(cond)
                def _(row0=row0, act=act):
                    c = pltpu.make_async_copy(zero_scr, xr_hbm.at[pl.ds(row0, br)], sem.at[1])
                    c.start() if act == "start" else c.wait()

    def scatter(x_ref):
        for k in range(TOP_K):
            def body(r, c, k=k):
                pltpu.make_async_copy(x_ref.at[r], xr_hbm.at[pos_ref[0, k, r]], sem.at[0]).start()
                return c
            lax.fori_loop(0, bt, body, 0)

    @pl.when(i < nsteps_a)
    def _():
        scatter(xa_ref)

    @pl.when(i >= nsteps_a)
    def _():
        scatter(xb_ref)

    done = xr_hbm.at[pl.ds(0, TOP_K * bt)]
    pltpu.make_async_copy(done, done, sem.at[0]).wait()


def _dispatch(xn_a, xn_b, pos_t, pad_end, tot, n_rows, bt, br):
    na, nb_ = xn_a.shape[0] // bt, xn_b.shape[0] // bt
    kern = functools.partial(_dispatch_kernel, bt=bt, br=br, nsteps_a=na)
    grid_spec = pltpu.PrefetchScalarGridSpec(
        num_scalar_prefetch=2,
        grid=(na + nb_,),
        in_specs=[
            pl.BlockSpec((1, SUBLANE, bt), lambda i, e, t: (i, 0, 0), memory_space=pltpu.SMEM),
            pl.BlockSpec((bt, ROW_TILE, LANE), lambda i, e, t: (jnp.minimum(i, na - 1), 0, 0)),
            pl.BlockSpec((bt, ROW_TILE, LANE), lambda i, e, t: (jnp.maximum(i - na, 0), 0, 0)),
        ],
        out_specs=pl.BlockSpec(memory_space=pl.ANY),
        scratch_shapes=[pltpu.VMEM((br, ROW_TILE, LANE), F32), pltpu.SemaphoreType.DMA((2,))],
    )
    return pl.pallas_call(
        kern,
        grid_spec=grid_spec,
        out_shape=jax.ShapeDtypeStruct((n_rows, ROW_TILE, LANE), F32),
        compiler_params=_cparams(("arbitrary",)),
        name="dispatch",
    )(pad_end, tot, pos_t, xn_a, xn_b)


def _rows_to_tiles(ref, x):
    for j in range(ROW_TILE):
        ref[:, j, :] = x[:, j * LANE:(j + 1) * LANE]


def _rows_from_tiles(ref):
    return jnp.concatenate([ref[:, j, :] for j in range(ROW_TILE)], axis=1)


def _moe_kernel(be_ref, nu_ref, x_ref, wgu_ref, bgu_ref, wd_ref, bdn_ref,
                y_ref, wgu_bf, wd_bf):
    i = pl.program_id(0)
    n_used = nu_ref[0]

    @pl.when(i < n_used)
    def _():
        e = be_ref[i]
        prev = be_ref[jnp.maximum(i - 1, 0)]

        @pl.when((i == 0) | (e != prev))
        def _():
            wgu_bf[...] = wgu_ref[0].astype(BF16)
            wd_bf[...] = wd_ref[0].astype(BF16)

        xb = _rows_from_tiles(x_ref).astype(BF16)
        gu = _dot(xb, wgu_bf[...]) + bgu_ref[0]
        g_lin = jnp.minimum(gu[:, 0:D_FF], SWIGLU_LIMIT)
        u_lin = jnp.clip(gu[:, D_FF:2 * D_FF], -SWIGLU_LIMIT, SWIGLU_LIMIT)
        act = g_lin / (1.0 + jnp.exp(-SWIGLU_ALPHA * g_lin)) * (u_lin + 1.0)
        y = _dot(act.astype(BF16), wd_bf[...]) + bdn_ref[0]
        _rows_to_tiles(y_ref, y)

    @pl.when(i >= n_used)
    def _():
        y_ref[...] = jnp.zeros(y_ref.shape, y_ref.dtype)


def _moe(x_rows, block_e, n_used, w_gu, b_gu, w_down, b_down, br):
    nb = x_rows.shape[0] // br
    grid_spec = pltpu.PrefetchScalarGridSpec(
        num_scalar_prefetch=2,
        grid=(nb,),
        in_specs=[
            pl.BlockSpec((br, ROW_TILE, LANE),
                         lambda i, be, nu: (jnp.minimum(i, nu[0] - 1), 0, 0)),
            pl.BlockSpec((1, D_MODEL, 2 * D_FF), lambda i, be, nu: (be[i], 0, 0)),
            pl.BlockSpec((1, 1, 2 * D_FF), lambda i, be, nu: (be[i], 0, 0)),
            pl.BlockSpec((1, D_FF, D_MODEL), lambda i, be, nu: (be[i], 0, 0)),
            pl.BlockSpec((1, 1, D_MODEL), lambda i, be, nu: (be[i], 0, 0)),
        ],
        out_specs=pl.BlockSpec((br, ROW_TILE, LANE), lambda i, be, nu: (i, 0, 0)),
        scratch_shapes=[
            pltpu.VMEM((D_MODEL, 2 * D_FF), BF16),
            pltpu.VMEM((D_FF, D_MODEL), BF16),
        ],
    )
    return pl.pallas_call(
        _moe_kernel,
        grid_spec=grid_spec,
        out_shape=jax.ShapeDtypeStruct((nb * br, ROW_TILE, LANE), F32),
        compiler_params=_cparams(("arbitrary",)),
        name="moe",
    )(block_e, n_used, x_rows, w_gu, b_gu.reshape(N_EXPERTS, 1, 2 * D_FF),
      w_down, b_down.reshape(N_EXPERTS, 1, D_MODEL))


def _combine_kernel(pos_ref, posn_ref, y_hbm, h1_ref, rg_ref, fw_ref, o_ref, yg, sem, *, bt):
    i = pl.program_id(0)
    nsteps = pl.num_programs(0)
    slot = lax.rem(i, 2)

    def issue(pref, s):
        for k in range(TOP_K):
            def body(r, c, k=k):
                pltpu.make_async_copy(y_hbm.at[pref[0, k, r]], yg.at[s, k * bt + r],
                                      sem.at[s]).start()
                return c
            lax.fori_loop(0, bt, body, 0)

    @pl.when(i == 0)
    def _():
        issue(pos_ref, 0)

    @pl.when(i + 1 < nsteps)
    def _():
        issue(posn_ref, 1 - slot)

    pltpu.make_async_copy(yg.at[slot], yg.at[slot], sem.at[slot]).wait()
    gates = [jnp.broadcast_to(rg_ref[:, k:k + 1], (bt, LANE)) for k in range(TOP_K)]
    ss = jnp.zeros((bt, LANE), F32)
    for j in range(ROW_TILE):
        js = slice(j * LANE, (j + 1) * LANE)
        a = h1_ref[:, js]
        for k in range(TOP_K):
            a = a + gates[k] * yg[slot, k * bt:(k + 1) * bt, j, :]
        ss = ss + a * a
        o_ref[:, js] = a
    ms = jnp.sum(ss, axis=-1, keepdims=True) * (1.0 / D_MODEL)
    o_ref[...] = o_ref[...] * lax.rsqrt(ms + EPS) * fw_ref[...]


def _combine(y_rows, pos_t, h1, rg, final_w, bt):
    n = h1.shape[0]
    nsteps = n // bt
    kern = functools.partial(_combine_kernel, bt=bt)
    row = lambda w: pl.BlockSpec((bt, w), lambda i: (i, 0))
    return pl.pallas_call(
        kern,
        grid=(nsteps,),
        in_specs=[
            pl.BlockSpec((1, SUBLANE, bt), lambda i: (i, 0, 0), memory_space=pltpu.SMEM),
            pl.BlockSpec((1, SUBLANE, bt), lambda i: (jnp.minimum(i + 1, nsteps - 1), 0, 0),
                         memory_space=pltpu.SMEM),
            pl.BlockSpec(memory_space=pl.ANY),
            row(D_MODEL), row(LANE),
            pl.BlockSpec((1, D_MODEL), lambda i: (0, 0)),
        ],
        out_specs=row(D_MODEL),
        out_shape=jax.ShapeDtypeStruct((n, D_MODEL), F32),
        scratch_shapes=[pltpu.VMEM((2, TOP_K * bt, ROW_TILE, LANE), F32),
                        pltpu.SemaphoreType.DMA((2,))],
        compiler_params=_cparams(("arbitrary",)),
        name="combine",
    )(pos_t, pos_t, y_rows, h1, rg, final_w.reshape(1, D_MODEL))


def kernel(x_prompt, x_sample, cache_k, cache_v, cache_k_idx, state_gla, norm_mix_w, w_in,
           w_gla_a2, b_gla_a2, gla_norm_w, w_out, norm_ffn_w, router_w, router_b, w_gu, b_gu,
           w_down, b_down, norm_final_w):
    b_p, t_p, _ = x_prompt.shape
    b_s, t_s, _ = x_sample.shape
    past = cache_k.shape[2]
    assert b_p == 1 and norm_mix_w.shape[0] == 1
    l = 0
    pw = _proj_weights(norm_mix_w[l], w_in[l])
    gconsts = _gla_consts(w_gla_a2[l], b_gla_a2[l])

    pp = _project_t(x_prompt.reshape(t_p, D_MODEL), _proj_t_weights(norm_mix_w[l], w_in[l]), DSA_BK)
    gla_p, kf_p, vf_p, kb_p, ikf_p, ikb_p, g2_p, qt_p, iqt_p, vt3_p, g2t_p = pp
    s0 = jnp.zeros((1, GLA_WIDTH, GLA_QK), F32)
    og_p, st_p = _gla(gla_p, g2_p, s0, gconsts, 1, t_p, 512, 16)
    od_p = _dsa_prompt_t(qt_p, iqt_p, g2t_p, ikb_p, kb_p, vt3_p, DSA_BQ, DSA_BK)

    ps = _project(x_sample.reshape(b_s * t_s, D_MODEL), pw, 512)
    gla_s, q_s, kf_s, vf_s, kb_s, vb_s, iq_s, iklo_s, ikhi_s, ikf_s, g2_s = ps
    og_s, st_s = _gla(gla_s, g2_s, _state_to_kernel(state_gla[l]), gconsts, b_s, t_s, t_s, 16)
    zc = jnp.zeros((b_s, past, IDX_DIM), BF16)
    cik = cache_k_idx[l].astype(BF16)
    cilo = jnp.concatenate([cik, zc], axis=2)
    cihi = jnp.concatenate([zc, cik], axis=2)
    od_s = _dsa_sample(q_s, iq_s, g2_s, cache_k[l], cache_v[l], cilo, cihi,
                       kb_s, vb_s, iklo_s, ikhi_s, t_s, 512)

    mconsts = _mix_consts(gla_norm_w[l], w_out[l], norm_ffn_w[l], router_w[l], router_b[l])
    n_s = b_s * t_s
    tb = TOKEN_BLOCK
    h1_p, xn_p, re_p, rg_p, cnt_p = _mix(og_p, gla_p, od_p, x_prompt.reshape(t_p, D_MODEL), mconsts, tb)
    h1_s, xn_s, re_s, rg_s, cnt_s = _mix(og_s, gla_s, od_s, x_sample.reshape(n_s, D_MODEL), mconsts, tb)
    n_pairs = (t_p + n_s) * TOP_K
    base, block_e, n_used, ends, tots = _route_tables(
        jnp.concatenate([cnt_p, cnt_s], axis=0), MOE_ROWS, n_pairs)
    n_rows = block_e.shape[0] * MOE_ROWS
    pos_p = _positions(re_p, base[0:t_p // tb], tb)
    pos_s = _positions(re_s, base[t_p // tb:], tb)
    x_rows = _dispatch(xn_p, xn_s, jnp.concatenate([pos_p, pos_s], axis=0), ends, tots,
                       n_rows, tb, MOE_ROWS)
    y_rows = _moe(x_rows, block_e, n_used, w_gu[l], b_gu[l], w_down[l], b_down[l], MOE_ROWS)
    y_p = _combine(y_rows, pos_p, h1_p, rg_p, norm_final_w, tb)
    y_s = _combine(y_rows, pos_s, h1_s, rg_s, norm_final_w, tb)
    y_p = y_p.reshape(x_prompt.shape)
    y_s = y_s.reshape(x_sample.shape)
    return (y_p, y_s,
            kf_p.reshape(1, 1, t_p, DSA_HEADS, DSA_DH), vf_p.reshape(1, 1, t_p, DSA_HEADS, DSA_DH),
            ikf_p.reshape(1, 1, t_p, IDX_DIM), _state_from_kernel(st_p)[None],
            kf_s.reshape(1, b_s, t_s, DSA_HEADS, DSA_DH), vf_s.reshape(1, b_s, t_s, DSA_HEADS, DSA_DH),
            ikf_s.reshape(1, b_s, t_s, IDX_DIM), _state_from_kernel(st_s)[None])
```

```python
import functools

import jax
import jax.numpy as jnp
from jax import lax
from jax.experimental import pallas as pl
from jax.experimental.pallas import tpu as pltpu

F32 = jnp.float32
BF16 = jnp.bfloat16
I32 = jnp.int32

D_MODEL = 1024
CHUNK = 64
GLA_HEADS = 4
GLA_DK = 64
GLA_DV = 128
GLA_QK = GLA_HEADS * GLA_DK
GLA_WIDTH = GLA_HEADS * GLA_DV
GLA_GATE_RANK = 16
GLA_GATE_TAU = 16.0
DSA_HEADS = 4
DSA_DH = 128
DSA_WIDTH = DSA_HEADS * DSA_DH
IDX_HEADS = 8
IDX_DIM = 64
IDX_TOPK_MAX = 256
N_EXPERTS = 32
TOP_K = 4
D_FF = 1024
SWIGLU_ALPHA = 1.702
SWIGLU_LIMIT = 7.0
EPS = 1e-6

_OFF_GLA = 0
_W_GLA = 2 * GLA_QK + 2 * GLA_WIDTH
_OFF_GA = _OFF_GLA + _W_GLA
_OFF_DSA = _OFF_GA + GLA_GATE_RANK
_W_DSA = 3 * DSA_WIDTH + IDX_HEADS * IDX_DIM
_OFF_IK = _OFF_DSA + _W_DSA
_OFF_IW = _OFF_IK + IDX_DIM
LANE = 128
SUBLANE = 8
ROW_TILE = D_MODEL // LANE
assert ROW_TILE == SUBLANE
LOG2E = 1.4426950408889634
NEG_BIG = -1e30
VMEM_LIMIT = 56 * 1024 * 1024


def _cparams(sem):
    return pltpu.CompilerParams(dimension_semantics=sem, vmem_limit_bytes=VMEM_LIMIT)


def _dot(a, b):
    return jnp.dot(a, b, preferred_element_type=F32)


def _dot_nt(a, b):
    return lax.dot_general(a, b, (((1,), (1,)), ((), ())), preferred_element_type=F32)


def _dot_tn(a, b):
    return lax.dot_general(a, b, (((0,), (0,)), ((), ())), preferred_element_type=F32)


def _split3(x):
    hi = x.astype(BF16)
    r1 = x - hi.astype(F32)
    mid = r1.astype(BF16)
    lo = (r1 - mid.astype(F32)).astype(BF16)
    return hi, mid, lo


def _proj_kernel(x_ref, nw_ref, wg_ref, wd_ref, ws_ref,
                 gla_ref, q_ref, kf_ref, vf_ref, kb_ref, vb_ref, iq_ref,
                 iklo_ref, ikhi_ref, ikf_ref, g2_ref):
    x = x_ref[...]
    ms = jnp.mean(x * x, axis=-1, keepdims=True)
    xn = (x * lax.rsqrt(ms + EPS) * nw_ref[...]).astype(BF16)
    gla_ref[...] = _dot(xn, wg_ref[...]).astype(BF16)
    W = DSA_WIDTH
    dq = _dot(xn, wd_ref[:, 0:W])
    q_ref[...] = (dq * (DSA_DH ** -0.5 * LOG2E)).astype(BF16)
    dk = _dot(xn, wd_ref[:, W:2 * W])
    kf_ref[...] = dk
    kb_ref[...] = dk.astype(BF16)
    dv = _dot(xn, wd_ref[:, 2 * W:3 * W])
    vf_ref[...] = dv
    vb_ref[...] = dv.astype(BF16)
    iq_ref[...] = _dot(xn, wd_ref[:, 3 * W:4 * W]).astype(BF16)
    sm = _dot(xn, ws_ref[...])
    iklo_ref[...] = sm[:, 0:LANE].astype(BF16)
    ikhi_ref[...] = sm[:, LANE:2 * LANE].astype(BF16)
    ikf_ref[...] = sm[:, 0:IDX_DIM]
    g2_ref[...] = sm[:, 2 * LANE:3 * LANE]


def _proj_weights(norm_w, w_in):
    wg = w_in[:, _OFF_GLA:_OFF_GLA + _W_GLA].astype(BF16)
    wd = w_in[:, _OFF_DSA:_OFF_DSA + _W_DSA].astype(BF16)
    ik = w_in[:, _OFF_IK:_OFF_IK + IDX_DIM]
    z64 = jnp.zeros((D_MODEL, IDX_DIM), F32)
    ga = w_in[:, _OFF_GA:_OFF_GA + GLA_GATE_RANK]
    iw = w_in[:, _OFF_IW:_OFF_IW + IDX_HEADS]
    zpad = jnp.zeros((D_MODEL, LANE - GLA_GATE_RANK - IDX_HEADS), F32)
    ws = jnp.concatenate([ik, z64, z64, ik, ga, iw, zpad], axis=1).astype(BF16)
    return norm_w.reshape(1, D_MODEL), wg, wd, ws


def _project(x2d, pw, bm):
    n = x2d.shape[0]
    nw, wg, wd, ws = pw
    row = lambda w: pl.BlockSpec((bm, w), lambda i: (i, 0))
    full = lambda a: pl.BlockSpec(a.shape, lambda i: (0, 0))
    outs = [(_W_GLA, BF16), (DSA_WIDTH, BF16), (DSA_WIDTH, F32), (DSA_WIDTH, F32),
            (DSA_WIDTH, BF16), (DSA_WIDTH, BF16), (IDX_HEADS * IDX_DIM, BF16),
            (LANE, BF16), (LANE, BF16), (IDX_DIM, F32), (LANE, F32)]
    return pl.pallas_call(
        _proj_kernel,
        grid=(n // bm,),
        in_specs=[row(D_MODEL), full(nw), full(wg), full(wd), full(ws)],
        out_specs=[row(w) for w, _ in outs],
        out_shape=[jax.ShapeDtypeStruct((n, w), dt) for w, dt in outs],
        compiler_params=_cparams(("parallel",)),
        name="proj",
    )(x2d, nw, wg, wd, ws)


def _proj_t_kernel(x_ref, nw_ref, wg_ref, wkv_ref, ws_ref, wt_ref, wst_ref,
                   gla_ref, kf_ref, vf_ref, kb_ref, ikf_ref, ikb_ref, g2_ref,
                   qt_ref, iqt_ref, vt_ref, g2t_ref):
    x = x_ref[...]
    ms = jnp.mean(x * x, axis=-1, keepdims=True)
    xn = (x * lax.rsqrt(ms + EPS) * nw_ref[...]).astype(BF16)
    gla_ref[...] = _dot(xn, wg_ref[...]).astype(BF16)
    W = DSA_WIDTH
    dk = _dot(xn, wkv_ref[:, 0:W])
    dv = _dot(xn, wkv_ref[:, W:2 * W])
    kb_ref[...] = dk.astype(BF16)
    for h in range(DSA_HEADS):
        kf_ref[:, h, :] = dk[:, h * DSA_DH:(h + 1) * DSA_DH]
        vf_ref[:, h, :] = dv[:, h * DSA_DH:(h + 1) * DSA_DH]
    sm = _dot(xn, ws_ref[...])
    ikb_ref[...] = sm[:, 0:LANE].astype(BF16)
    ikf_ref[...] = sm[:, 0:IDX_DIM]
    g2_ref[...] = sm[:, LANE:2 * LANE]
    qt_ref[...] = (_dot_nt(wt_ref[0:W, :], xn) * (DSA_DH ** -0.5 * LOG2E)).astype(BF16)
    vt_ref[0] = _dot_nt(wt_ref[W:2 * W, :], xn).astype(BF16)
    iqt_ref[...] = _dot_nt(wt_ref[2 * W:3 * W, :], xn).astype(BF16)
    g2t_ref[...] = _dot_nt(wst_ref[...], xn)


def _proj_t_weights(norm_w, w_in):
    wg = w_in[:, _OFF_GLA:_OFF_GLA + _W_GLA].astype(BF16)
    W = DSA_WIDTH
    wq = w_in[:, _OFF_DSA:_OFF_DSA + W]
    wkv = w_in[:, _OFF_DSA + W:_OFF_DSA + 3 * W]
    wv = w_in[:, _OFF_DSA + 2 * W:_OFF_DSA + 3 * W]
    wiq = w_in[:, _OFF_DSA + 3 * W:_OFF_DSA + 4 * W]
    ik = w_in[:, _OFF_IK:_OFF_IK + IDX_DIM]
    z64 = jnp.zeros((D_MODEL, IDX_DIM), F32)
    ga = w_in[:, _OFF_GA:_OFF_GA + GLA_GATE_RANK]
    iw = w_in[:, _OFF_IW:_OFF_IW + IDX_HEADS]
    zpad = jnp.zeros((D_MODEL, LANE - GLA_GATE_RANK - IDX_HEADS), F32)
    g2w = jnp.concatenate([ga, iw, zpad], axis=1)
    ws = jnp.concatenate([ik, z64, g2w], axis=1).astype(BF16)
    wt = jnp.concatenate([wq, wv, wiq], axis=1).T.astype(BF16)
    return norm_w.reshape(1, D_MODEL), wg, wkv.astype(BF16), ws, wt, g2w.T.astype(BF16)


def _project_t(x2d, pw, bm):
    n = x2d.shape[0]
    row = lambda w: pl.BlockSpec((bm, w), lambda i: (i, 0))
    row3 = pl.BlockSpec((bm, DSA_HEADS, DSA_DH), lambda i: (i, 0, 0))
    col = lambda h: pl.BlockSpec((h, bm), lambda i: (0, i))
    full = lambda a: pl.BlockSpec(a.shape, lambda i: (0, 0))
    out_specs = [row(_W_GLA), row3, row3, row(DSA_WIDTH), row(IDX_DIM), row(LANE), row(LANE),
                 col(DSA_WIDTH), col(IDX_HEADS * IDX_DIM),
                 pl.BlockSpec((1, DSA_WIDTH, bm), lambda i: (i, 0, 0)), col(LANE)]
    out_shape = [jax.ShapeDtypeStruct((n, _W_GLA), BF16),
                 jax.ShapeDtypeStruct((n, DSA_HEADS, DSA_DH), F32),
                 jax.ShapeDtypeStruct((n, DSA_HEADS, DSA_DH), F32),
                 jax.ShapeDtypeStruct((n, DSA_WIDTH), BF16),
                 jax.ShapeDtypeStruct((n, IDX_DIM), F32),
                 jax.ShapeDtypeStruct((n, LANE), BF16),
                 jax.ShapeDtypeStruct((n, LANE), F32),
                 jax.ShapeDtypeStruct((DSA_WIDTH, n), BF16),
                 jax.ShapeDtypeStruct((IDX_HEADS * IDX_DIM, n), BF16),
                 jax.ShapeDtypeStruct((n // bm, DSA_WIDTH, bm), BF16),
                 jax.ShapeDtypeStruct((LANE, n), F32)]
    return pl.pallas_call(
        _proj_t_kernel,
        grid=(n // bm,),
        in_specs=[row(D_MODEL)] + [full(a) for a in pw],
        out_specs=out_specs,
        out_shape=out_shape,
        compiler_params=_cparams(("parallel",)),
        name="proj_t",
    )(x2d, *pw)


def _log_sigmoid(x):
    return jnp.minimum(x, 0.0) - jnp.log1p(jnp.exp(-jnp.abs(x)))


def _gla_kernel(gla_ref, g2_ref, w2_ref, b2_ref, seg_ref, bd_ref, s0_ref,
                o_ref, sT_ref,
                st_scr, kpad, bpad, vpad, qt_scr, kt_scr, dec_scr, oi_scr,
                *, bt, c):
    t = pl.program_id(1)

    @pl.when(t == 0)
    def _():
        st_scr[...] = s0_ref[0]

    q = gla_ref[:, 0:GLA_QK].astype(F32) * (GLA_DK ** -0.5)
    k = gla_ref[:, GLA_QK:2 * GLA_QK].astype(F32)
    vb = gla_ref[:, 2 * GLA_QK:2 * GLA_QK + GLA_WIDTH]
    v = vb.astype(F32)

    ga = g2_ref[...]
    ga_hi = ga.astype(BF16)
    ga_lo = (ga - ga_hi.astype(F32)).astype(BF16)
    logit = (_dot(ga_hi, w2_ref[0]) + _dot(ga_hi, w2_ref[1]) + _dot(ga_lo, w2_ref[0])
             + b2_ref[...])
    lg = _log_sigmoid(logit) * (1.0 / GLA_GATE_TAU)

    shift = c.bit_length() - 1
    row = lax.broadcasted_iota(I32, (bt, bt), 0)
    col = lax.broadcasted_iota(I32, (bt, bt), 1)
    same = (row >> shift) == (col >> shift)
    tri = jnp.where(same & (col <= row), 1.0, 0.0).astype(BF16)
    last = jnp.where(same & ((col & (c - 1)) == c - 1), 1.0, 0.0).astype(BF16)
    l0, l1, l2 = _split3(lg)
    b = _dot(tri, l0) + _dot(tri, l1) + _dot(tri, l2)
    b0, b1, b2s = _split3(b)
    bl = _dot(last, b0) + _dot(last, b1) + _dot(last, b2s)

    zpad = jnp.zeros((c, GLA_QK), F32)
    kpad[0:c, :] = zpad
    bpad[0:c, :] = zpad
    vpad[0:c, :] = jnp.zeros((c, GLA_WIDTH), F32)
    kpad[c:c + bt, :] = k
    bpad[c:c + bt, :] = b
    vpad[c:c + bt, :] = v
    pos = lax.broadcasted_iota(I32, (bt, GLA_QK), 0) & (c - 1)
    seg = seg_ref[...]
    o_intra = jnp.zeros((bt, GLA_WIDTH), F32)
    for d in range(c):
        ks = kpad[c - d:c - d + bt, :]
        bs = bpad[c - d:c - d + bt, :]
        vs = vpad[c - d:c - d + bt, :]
        z = q * ks * jnp.exp(jnp.minimum(b - bs, 0.0))
        z = jnp.where(pos >= d, z, 0.0)
        o_intra = o_intra + _dot(z.astype(BF16), seg) * vs
    oi_scr[...] = o_intra

    qt_scr[...] = q * jnp.exp(b)
    kt_scr[...] = k * jnp.exp(bl - b)
    dec_scr[...] = jnp.exp(bl)
    bd = bd_ref[...]

    def step(ci, carry):
        r0 = pl.multiple_of(ci * c, c)
        qc = qt_scr[pl.ds(r0, c), :].astype(BF16)
        kc = kt_scr[pl.ds(r0, c), :].astype(BF16)
        vc = vpad[pl.ds(r0 + c, c), :].astype(BF16)
        st = st_scr[...]
        o_ref[pl.ds(r0, c), :] = oi_scr[pl.ds(r0, c), :] + _dot_nt(qc, st.astype(BF16))
        dec = dec_scr[pl.ds(r0, 1), :]
        st_scr[...] = st * dec + _dot_tn(vc, kc) * bd
        return carry

    lax.fori_loop(0, bt // c, step, 0)

    @pl.when(t == pl.num_programs(1) - 1)
    def _():
        sT_ref[0] = st_scr[...]


def _gla_consts(w_gla_a2, b_gla_a2):
    w2 = jnp.zeros((LANE, GLA_QK), F32).at[0:GLA_GATE_RANK].set(w_gla_a2)
    w2_hi = w2.astype(BF16)
    w2_lo = (w2 - w2_hi.astype(F32)).astype(BF16)
    w2s = jnp.stack([w2_hi, w2_lo])
    hq = jnp.arange(GLA_QK) // GLA_DK
    hv = jnp.arange(GLA_WIDTH) // GLA_DV
    seg = (hq[:, None] == hv[None, :]).astype(BF16)
    bd = (hv[:, None] == hq[None, :]).astype(F32)
    return w2s, b_gla_a2.reshape(1, GLA_QK), seg, bd


def _state_to_kernel(s):
    n = s.shape[0]
    eye = jnp.eye(GLA_HEADS, dtype=s.dtype)
    t = jnp.swapaxes(s, 2, 3)[:, :, :, None, :] * eye[None, :, None, :, None]
    return t.reshape(n, GLA_WIDTH, GLA_QK)


def _state_from_kernel(t):
    n = t.shape[0]
    t5 = t.reshape(n, GLA_HEADS, GLA_DV, GLA_HEADS, GLA_DK)
    return jnp.einsum("shehd->shde", t5)


def _gla(gla, g2, s0, consts, n_streams, t_len, bt, c):
    w2s, b2, seg, bd = consts
    nb = t_len // bt
    kern = functools.partial(_gla_kernel, bt=bt, c=c)
    const = lambda a: pl.BlockSpec(a.shape, lambda s, t: (0,) * a.ndim)
    o, sT = pl.pallas_call(
        kern,
        grid=(n_streams, nb),
        in_specs=[
            pl.BlockSpec((bt, 2 * GLA_QK + GLA_WIDTH), lambda s, t: (s * nb + t, 0)),
            pl.BlockSpec((bt, LANE), lambda s, t: (s * nb + t, 0)),
            const(w2s), const(b2), const(seg), const(bd),
            pl.BlockSpec((1, GLA_WIDTH, GLA_QK), lambda s, t: (s, 0, 0)),
        ],
        out_specs=[
            pl.BlockSpec((bt, GLA_WIDTH), lambda s, t: (s * nb + t, 0)),
            pl.BlockSpec((1, GLA_WIDTH, GLA_QK), lambda s, t: (s, 0, 0)),
        ],
        out_shape=[
            jax.ShapeDtypeStruct((n_streams * t_len, GLA_WIDTH), F32),
            jax.ShapeDtypeStruct((n_streams, GLA_WIDTH, GLA_QK), F32),
        ],
        scratch_shapes=[
            pltpu.VMEM((GLA_WIDTH, GLA_QK), F32),
            pltpu.VMEM((bt + c, GLA_QK), F32),
            pltpu.VMEM((bt + c, GLA_QK), F32),
            pltpu.VMEM((bt + c, GLA_WIDTH), F32),
            pltpu.VMEM((bt, GLA_QK), F32),
            pltpu.VMEM((bt, GLA_QK), F32),
            pltpu.VMEM((bt, GLA_QK), F32),
            pltpu.VMEM((bt, GLA_WIDTH), F32),
        ],
        compiler_params=_cparams(("arbitrary", "arbitrary")),
        name="gla",
    )(gla, g2, w2s, b2, seg, bd, s0)
    return o, sT


INT_MIN = -2 ** 31
NEG_INF = float("-inf")
KEY_NEG_INF = -2139095041
F32_LOWEST = -3.4028234663852886e38
IDX_SCALE = IDX_DIM ** -0.5 * IDX_HEADS ** -0.5


def _fill_head_weights(wb_scr, g2_ref, bq):
    for h in range(IDX_HEADS):
        c0 = GLA_GATE_RANK + h
        col = g2_ref[:, c0:c0 + 1] * IDX_SCALE
        wb_scr[h] = jnp.broadcast_to(col, (bq, LANE))


def _index_scores(iq_ref, wb_scr, iklo, ikhi, bk):
    parts = [None] * (bk // LANE)
    for p in range(IDX_HEADS // 2):
        iqp = iq_ref[:, p * LANE:(p + 1) * LANE]
        y0 = jnp.maximum(_dot_nt(iqp, iklo), 0.0)
        y1 = jnp.maximum(_dot_nt(iqp, ikhi), 0.0)
        w0 = wb_scr[2 * p]
        w1 = wb_scr[2 * p + 1]
        for c in range(bk // LANE):
            t = y0[:, c * LANE:(c + 1) * LANE] * w0 + y1[:, c * LANE:(c + 1) * LANE] * w1
            parts[c] = t if parts[c] is None else parts[c] + t
    return jnp.concatenate(parts, axis=1)


def _key_to_float(key):
    key = jnp.maximum(key, KEY_NEG_INF)
    return pltpu.bitcast(key ^ ((key >> 31) & 0x7FFFFFFF), F32)


def _kth_largest(count, total, topk, shape):
    def bit_body(i, carry):
        prefix, n_ge = carry
        cand_u = prefix | jnp.left_shift(jnp.int32(1), 31 - i)
        cnt = count(_key_to_float(cand_u ^ INT_MIN), False)
        ok = cnt >= topk
        return jnp.where(ok, cand_u, prefix), jnp.where(ok, cnt, n_ge)

    init = (jnp.zeros(shape, I32), jnp.zeros(shape, I32) + total)
    prefix, n_ge = lax.fori_loop(0, 32, bit_body, init)
    t = _key_to_float(prefix ^ INT_MIN)
    return t, n_ge, count(t, True)


def _count_rows(load_tile, n_tiles, cand, strict, bq, bk):
    cand_b = jnp.broadcast_to(cand, (bq, LANE))

    def body(kt, acc):
        s = load_tile(kt)
        for c in range(bk // LANE):
            sc = s[:, c * LANE:(c + 1) * LANE]
            acc = acc + jnp.where(sc > cand_b if strict else sc >= cand_b, 1, 0)
        return acc

    acc = lax.fori_loop(0, n_tiles, body, jnp.zeros((bq, LANE), I32))
    return jnp.sum(acc, axis=1, keepdims=True)


def _attend_tile(sel, q_ref, kv_tile, m_scr, l_scr, acc_scr):
    for h in range(DSA_HEADS):
        hs = slice(h * DSA_DH, (h + 1) * DSA_DH)
        kh, vh = kv_tile(h)
        lg = jnp.where(sel, _dot_nt(q_ref[:, hs], kh), NEG_BIG)
        m_old = m_scr[h]
        m_new = jnp.maximum(m_old, jnp.max(lg, axis=1, keepdims=True))
        alpha = jnp.exp2(m_old - m_new)
        p = jnp.exp2(lg - m_new)
        l_scr[h] = alpha * l_scr[h] + jnp.sum(p, axis=1, keepdims=True)
        acc_scr[:, hs] = alpha * acc_scr[:, hs] + _dot(p.astype(BF16), vh)
        m_scr[h] = m_new


def _select(s, t, n_gt, eq_before, topk, upper, ties):
    if not ties:
        return s >= jnp.maximum(t, F32_LOWEST), eq_before
    eq = s == t
    eq_f = jnp.where(eq, 1.0, 0.0).astype(BF16)
    rank = eq_before + _dot(eq_f, upper)
    need = (topk - n_gt).astype(F32)
    sel = ((s > t) | (eq & (rank < need))) & (s > NEG_INF)
    return sel, eq_before + jnp.sum(eq_f.astype(F32), axis=1, keepdims=True)


def _init_softmax(m_scr, l_scr, acc_scr):
    m_scr[...] = jnp.full(m_scr.shape, NEG_BIG, F32)
    l_scr[...] = jnp.zeros(l_scr.shape, F32)
    acc_scr[...] = jnp.zeros(acc_scr.shape, F32)


def _finish_softmax(o_ref, l_scr, acc_scr):
    for h in range(DSA_HEADS):
        hs = slice(h * DSA_DH, (h + 1) * DSA_DH)
        o_ref[:, hs] = (acc_scr[:, hs] / l_scr[h]).astype(o_ref.dtype)


def _upper_ones(n):
    r = jnp.arange(n)
    return (r[:, None] < r[None, :]).astype(BF16)


DSA_BQ = 256
DSA_BK = 512

def _count_cols(sc_scr, n_tiles, cand, strict, bq, bk):
    cand_b = jnp.broadcast_to(cand, (SUBLANE, bq))
    n_acc = 4

    def body(kt, accs):
        accs = list(accs)
        for r in range(bk // SUBLANE):
            s = sc_scr[kt, r * SUBLANE:(r + 1) * SUBLANE, :]
            accs[r % n_acc] = accs[r % n_acc] + jnp.where(s > cand_b if strict else s >= cand_b, 1, 0)
        return tuple(accs)

    zero = jnp.zeros((SUBLANE, bq), I32)
    accs = lax.fori_loop(0, n_tiles, body, (zero,) * n_acc)
    return jnp.sum((accs[0] + accs[1]) + (accs[2] + accs[3]), axis=0, keepdims=True)


def _dsa_prompt_t_kernel(qt_ref, iqt_ref, g2t_ref, ik_ref, lower_ref, k_hbm, vt_hbm,
                         o_ref, sc_scr, kbuf, vbuf, sem, acc_scr, p_scr,
                         *, bq, bk, topk):
    qb = pl.program_id(0)
    q0 = qb * bq
    n_tiles = (q0 + bq + bk - 1) // bk
    sub = LANE

    def tile_scores(kt, masked):
        for s in range(bk // sub):
            ik_s = ik_ref[kt, s * sub:(s + 1) * sub, 0:IDX_DIM]
            acc = None
            for h in range(IDX_HEADS):
                y = _dot(ik_s, iqt_ref[h * IDX_DIM:(h + 1) * IDX_DIM, :])
                w = g2t_ref[GLA_GATE_RANK + h:GLA_GATE_RANK + h + 1, :] * IDX_SCALE
                t = jnp.maximum(y, 0.0) * w
                acc = t if acc is None else acc + t
            if masked:
                kpos = kt * bk + s * sub + lax.broadcasted_iota(I32, (sub, bq), 0)
                qpos = q0 + lax.broadcasted_iota(I32, (sub, bq), 1)
                shift = CHUNK.bit_length() - 1
                acc = jnp.where((kpos >> shift) <= (qpos >> shift), acc, NEG_INF)
            sc_scr[kt, s * sub:(s + 1) * sub, :] = acc

    def score_body(kt, carry):
        tile_scores(kt, False)
        return carry

    lax.fori_loop(0, n_tiles - 1, score_body, 0)
    tile_scores(n_tiles - 1, True)

    count = lambda cand, strict: _count_cols(sc_scr, n_tiles, cand, strict, bq, bk)
    t, n_ge, n_gt = _kth_largest(count, n_tiles * bk, topk, (1, bq))
    tie_q = jnp.where((n_ge > topk) & (t > NEG_INF), 1, 0)
    any_tie = jnp.max(tie_q) > 0

    def copies(kt, slot, with_v):
        cs = [pltpu.make_async_copy(k_hbm.at[kt], kbuf.at[slot], sem.at[0, slot])]
        if with_v:
            cs.append(pltpu.make_async_copy(vt_hbm.at[kt], vbuf.at[slot], sem.at[1, slot]))
        return cs

    def stream(kt, with_v):
        slot = lax.rem(kt, 2)
        for c in copies(kt, slot, with_v):
            c.wait()

        @pl.when(kt + 1 < n_tiles)
        def _():
            for c in copies(kt + 1, 1 - slot, with_v):
                c.start()

        return slot

    def select(kt, eq_before, ties):
        s = sc_scr[kt]
        if not ties:
            return s >= jnp.maximum(t, F32_LOWEST), eq_before
        eq = s == t
        eq_b = jnp.where(eq, 1.0, 0.0).astype(BF16)
        rank = eq_before + _dot(lower_ref[...], eq_b)
        need = (topk - n_gt).astype(F32)
        sel = ((s > t) | (eq & (rank < need))) & (s > NEG_INF)
        return sel, eq_before + jnp.sum(eq_b.astype(F32), axis=0, keepdims=True)

    def logits(slot, h, sel):
        hs = slice(h * DSA_DH, (h + 1) * DSA_DH)
        return jnp.where(sel, _dot(kbuf[slot, :, hs], qt_ref[hs, :]), NEG_BIG)

    def attend(ties):
        zero = jnp.zeros((1, bq), F32)
        for c in copies(0, 0, False):
            c.start()

        def max_body(kt, carry):
            ms, eq_before = carry
            slot = stream(kt, False)
            sel, eq_before = select(kt, eq_before, ties)
            ms = tuple(jnp.maximum(ms[h], jnp.max(logits(slot, h, sel), axis=0, keepdims=True))
                       for h in range(DSA_HEADS))
            return ms, eq_before

        neg = jnp.full((1, bq), NEG_BIG, F32)
        ms, _ = lax.fori_loop(0, n_tiles, max_body, ((neg,) * DSA_HEADS, zero))

        acc_scr[...] = jnp.zeros(acc_scr.shape, F32)
        for c in copies(0, 0, True):
            c.start()

        def sum_body(kt, carry):
            ls, eq_before = carry
            slot = stream(kt, True)
            sel, eq_before = select(kt, eq_before, ties)
            new_ls = []
            for h in range(DSA_HEADS):
                p = jnp.exp2(logits(slot, h, sel) - ms[h])
                new_ls.append(ls[h] + jnp.sum(p, axis=0, keepdims=True))
                p_scr[h] = p.astype(BF16)
            for h in range(DSA_HEADS):
                hs = slice(h * DSA_DH, (h + 1) * DSA_DH)
                acc_scr[hs, :] += _dot(vbuf[slot, hs, :], p_scr[h])
            return tuple(new_ls), eq_before

        ls, _ = lax.fori_loop(0, n_tiles, sum_body, ((zero,) * DSA_HEADS, zero))
        for h in range(DSA_HEADS):
            hs = slice(h * DSA_DH, (h + 1) * DSA_DH)
            acc_scr[hs, :] = acc_scr[hs, :] / ls[h]
        o_ref[...] = jnp.transpose(acc_scr[...]).astype(o_ref.dtype)

    @pl.when(jnp.logical_not(any_tie))
    def _():
        attend(False)

    @pl.when(any_tie)
    def _():
        attend(True)


def _dsa_prompt_t(qt, iqt, g2t, ikb, kb, vt3, bq, bk):
    t_len = qt.shape[1]
    n_kt = t_len // bk
    assert vt3.shape == (n_kt, DSA_WIDTH, bk) and bk % bq == 0
    topk = min(IDX_TOPK_MAX, t_len // 4)
    r = jnp.arange(bk)
    lower = (r[None, :] < r[:, None]).astype(BF16)
    kern = functools.partial(_dsa_prompt_t_kernel, bq=bq, bk=bk, topk=topk)
    col = lambda h: pl.BlockSpec((h, bq), lambda i: (0, i))
    ik3 = ikb.reshape(n_kt, bk, LANE)
    return pl.pallas_call(
        kern,
        grid=(t_len // bq,),
        in_specs=[col(DSA_WIDTH), col(IDX_HEADS * IDX_DIM), col(LANE),
                  pl.BlockSpec(ik3.shape, lambda i: (0, 0, 0)),
                  pl.BlockSpec(lower.shape, lambda i: (0, 0)),
                  pl.BlockSpec(memory_space=pl.ANY), pl.BlockSpec(memory_space=pl.ANY)],
        out_specs=pl.BlockSpec((bq, DSA_WIDTH), lambda i: (i, 0)),
        out_shape=jax.ShapeDtypeStruct((t_len, DSA_WIDTH), BF16),
        scratch_shapes=[
            pltpu.VMEM((n_kt, bk, bq), F32),
            pltpu.VMEM((2, bk, DSA_WIDTH), BF16),
            pltpu.VMEM((2, DSA_WIDTH, bk), BF16),
            pltpu.SemaphoreType.DMA((2, 2)),
            pltpu.VMEM((DSA_WIDTH, bq), F32),
            pltpu.VMEM((DSA_HEADS, bk, bq), BF16),
        ],
        compiler_params=_cparams(("arbitrary",)),
        name="dsa_prompt_t",
    )(qt, iqt, g2t, ik3, lower, kb.reshape(n_kt, bk, DSA_WIDTH), vt3)


def _dsa_sample_kernel(q_ref, iq_ref, g2_ref, ck_ref, cv_ref, cilo_ref, cihi_ref,
                       nk_ref, nv_ref, nilo_ref, nihi_ref, upper_ref, o_ref,
                       sc_scr, wb_scr, m_scr, l_scr, acc_scr, pk_scr, pv_scr, plo_scr, phi_scr,
                       *, bq, bk, past, topk):
    n_cache = past // bk
    sub = bk // LANE
    _fill_head_weights(wb_scr, g2_ref, bq)
    pk_scr[...] = jnp.zeros(pk_scr.shape, BF16)
    pv_scr[...] = jnp.zeros(pv_scr.shape, BF16)
    plo_scr[...] = jnp.zeros(plo_scr.shape, BF16)
    phi_scr[...] = jnp.zeros(phi_scr.shape, BF16)
    pk_scr[0:bq, :] = nk_ref[...]
    pv_scr[0:bq, :] = nv_ref[...]
    plo_scr[0:bq, :] = nilo_ref[...]
    phi_scr[0:bq, :] = nihi_ref[...]

    for kt in range(n_cache):
        s = _index_scores(iq_ref, wb_scr, cilo_ref[0, kt * bk:(kt + 1) * bk, :],
                          cihi_ref[0, kt * bk:(kt + 1) * bk, :], bk)
        for c in range(sub):
            sc_scr[kt * sub + c] = s[:, c * LANE:(c + 1) * LANE]
    s = _index_scores(iq_ref, wb_scr, plo_scr[...], phi_scr[...], LANE)
    real = lax.broadcasted_iota(I32, (bq, LANE), 1) < bq
    sc_scr[n_cache * sub] = jnp.where(real, s, NEG_INF)

    n_sub = n_cache * sub + 1
    count = lambda cand, strict: _count_rows(lambda i: sc_scr[i], n_sub, cand, strict, bq, LANE)
    t, n_ge, n_gt = _kth_largest(count, n_sub * LANE, topk, (bq, 1))
    tie_rows = jnp.where((n_ge > topk) & (t > NEG_INF), 1, 0)
    any_tie = jnp.max(tie_rows) > 0

    def attend(ties):
        _init_softmax(m_scr, l_scr, acc_scr)
        eq_before = jnp.zeros((bq, 1), F32)
        for kt in range(n_cache):
            keys = jnp.concatenate([sc_scr[kt * sub + c] for c in range(sub)], axis=1)
            upper = upper_ref[...] if ties else None
            sel, eq_before = _select(keys, t, n_gt, eq_before, topk, upper, ties)
            kv_tile = lambda h, kt=kt: (
                ck_ref[0, kt * bk:(kt + 1) * bk, h, :].astype(BF16),
                cv_ref[0, kt * bk:(kt + 1) * bk, h, :].astype(BF16))
            _attend_tile(sel, q_ref, kv_tile, m_scr, l_scr, acc_scr)
        upper = upper_ref[0:LANE, 0:LANE] if ties else None
        sel, eq_before = _select(sc_scr[n_cache * sub], t, n_gt, eq_before, topk, upper, ties)
        kv_tile = lambda h: (pk_scr[:, h * DSA_DH:(h + 1) * DSA_DH],
                             pv_scr[:, h * DSA_DH:(h + 1) * DSA_DH])
        _attend_tile(sel, q_ref, kv_tile, m_scr, l_scr, acc_scr)
        _finish_softmax(o_ref, l_scr, acc_scr)

    @pl.when(jnp.logical_not(any_tie))
    def _():
        attend(False)

    @pl.when(any_tie)
    def _():
        attend(True)


def _dsa_sample(q, iq, g2, cache_k, cache_v, cache_ilo, cache_ihi, kb, vb, iklo, ikhi, bq, bk):
    n_streams, past = cache_k.shape[0:2]
    assert past % CHUNK == 0 and bq <= CHUNK and past % bk == 0
    topk = min(IDX_TOPK_MAX, (past + bq) // 4)
    upper = _upper_ones(bk)
    kern = functools.partial(_dsa_sample_kernel, bq=bq, bk=bk, past=past, topk=topk)
    row = lambda w: pl.BlockSpec((bq, w), lambda s: (s, 0))
    cache = lambda w: pl.BlockSpec((1, past, w), lambda s: (s, 0, 0))
    cache_kv = pl.BlockSpec((1, past, DSA_HEADS, DSA_DH), lambda s: (s, 0, 0, 0))
    n_sub = past // LANE + 1
    return pl.pallas_call(
        kern,
        grid=(n_streams,),
        in_specs=[row(DSA_WIDTH), row(IDX_HEADS * IDX_DIM), row(LANE),
                  cache_kv, cache_kv, cache(LANE), cache(LANE),
                  row(DSA_WIDTH), row(DSA_WIDTH), row(LANE), row(LANE),
                  pl.BlockSpec(upper.shape, lambda s: (0, 0))],
        out_specs=row(DSA_WIDTH),
        out_shape=jax.ShapeDtypeStruct((n_streams * bq, DSA_WIDTH), BF16),
        scratch_shapes=[
            pltpu.VMEM((n_sub, bq, LANE), F32),
            pltpu.VMEM((IDX_HEADS, bq, LANE), F32),
            pltpu.VMEM((DSA_HEADS, bq, 1), F32),
            pltpu.VMEM((DSA_HEADS, bq, 1), F32),
            pltpu.VMEM((bq, DSA_WIDTH), F32),
            pltpu.VMEM((LANE, DSA_WIDTH), BF16),
            pltpu.VMEM((LANE, DSA_WIDTH), BF16),
            pltpu.VMEM((LANE, LANE), BF16),
            pltpu.VMEM((LANE, LANE), BF16),
        ],
        compiler_params=_cparams(("parallel",)),
        name="dsa_sample",
    )(q, iq, g2, cache_k, cache_v, cache_ilo, cache_ihi, kb, vb, iklo, ikhi, upper)


def _mix_kernel(og_ref, gg_ref, od_ref, x_ref, gnw_ref, wo_ref, nfw_ref, rw_ref, rb_ref,
                h1_ref, xn_ref, re_ref, rg_ref, cnt_ref):
    g = gg_ref[...].astype(F32)
    gate = g / (1.0 + jnp.exp(-g))
    acc = _dot(od_ref[...], wo_ref[GLA_WIDTH:GLA_WIDTH + DSA_WIDTH, :])
    for h in range(GLA_HEADS):
        hs = slice(h * GLA_DV, (h + 1) * GLA_DV)
        oh = og_ref[:, hs]
        ms = jnp.mean(oh * oh, axis=-1, keepdims=True)
        a = oh * lax.rsqrt(ms + EPS) * gnw_ref[...] * gate[:, hs]
        acc = acc + _dot(a.astype(BF16), wo_ref[hs, :])
    h1 = x_ref[...] + acc
    h1_ref[...] = h1
    ms = jnp.mean(h1 * h1, axis=-1, keepdims=True)
    xn = h1 * lax.rsqrt(ms + EPS) * nfw_ref[...]
    _rows_to_tiles(xn_ref, xn)
    xn_hi = xn.astype(BF16)
    xn_lo = (xn - xn_hi.astype(F32)).astype(BF16)
    logits = (_dot(xn_hi, rw_ref[0]) + _dot(xn_hi, rw_ref[1]) + _dot(xn_lo, rw_ref[0])
              + rb_ref[...])
    lane = lax.broadcasted_iota(I32, logits.shape, 1)
    lane_f = lane.astype(F32)
    work = logits
    e_out = jnp.zeros(logits.shape, I32)
    tops, hots = [], []
    for k in range(TOP_K):
        mx = jnp.max(work, axis=1, keepdims=True)
        idx = jnp.min(jnp.where(work == mx, lane_f, float(LANE)), axis=1, keepdims=True)
        idx = idx.astype(I32)
        hot = lane == idx
        e_out = jnp.where(lane == k, idx, e_out)
        tops.append(mx)
        hots.append(hot)
        work = jnp.where(hot, NEG_BIG, work)
    ex = [jnp.exp(v - tops[0]) for v in tops]
    den = ex[0] + ex[1] + ex[2] + ex[3]
    g_out = jnp.zeros(logits.shape, F32)
    for k in range(TOP_K):
        g_out = jnp.where(lane == k, ex[k] / den, g_out)
    rg_ref[...] = g_out
    bm = logits.shape[0]
    earlier = (lax.broadcasted_iota(I32, (bm, bm), 1) < lax.broadcasted_iota(I32, (bm, bm), 0))
    earlier = jnp.where(earlier, 1.0, 0.0).astype(BF16)
    run = jnp.zeros((1, LANE), F32)
    for k in range(TOP_K):
        hot_f = jnp.where(hots[k], 1.0, 0.0)
        before = _dot(earlier, hot_f.astype(BF16)) + run
        rank = jnp.sum(jnp.where(hots[k], before, 0.0), axis=1, keepdims=True).astype(I32)
        e_out = jnp.where(lane == TOP_K + k, rank, e_out)
        run = run + jnp.sum(hot_f, axis=0, keepdims=True)
    re_ref[...] = e_out
    cnt_ref[0] = jnp.broadcast_to(run, (SUBLANE, LANE)).astype(I32)


def _mix_consts(gla_norm_w, w_out, norm_ffn_w, router_w, router_b):
    rw = jnp.zeros((D_MODEL, LANE), F32).at[:, 0:N_EXPERTS].set(router_w)
    rw_hi = rw.astype(BF16)
    rw_lo = (rw - rw_hi.astype(F32)).astype(BF16)
    rb = jnp.full((1, LANE), NEG_BIG, F32).at[0, 0:N_EXPERTS].set(router_b)
    return (gla_norm_w.reshape(1, GLA_DV), w_out.astype(BF16), norm_ffn_w.reshape(1, D_MODEL),
            jnp.stack([rw_hi, rw_lo]), rb)


def _mix(og, gla, od, x2d, consts, bm):
    n = x2d.shape[0]
    gnw, wo, nfw, rws, rb = consts
    row = lambda w: pl.BlockSpec((bm, w), lambda i: (i, 0))
    const = lambda a: pl.BlockSpec(a.shape, lambda i: (0,) * a.ndim)
    gg_col = (2 * GLA_QK + GLA_WIDTH) // GLA_WIDTH
    return pl.pallas_call(
        _mix_kernel,
        grid=(n // bm,),
        in_specs=[row(GLA_WIDTH), pl.BlockSpec((bm, GLA_WIDTH), lambda i: (i, gg_col)),
                  row(DSA_WIDTH), row(D_MODEL),
                  const(gnw), const(wo), const(nfw), const(rws), const(rb)],
        out_specs=[row(D_MODEL), pl.BlockSpec((bm * ROW_TILE, LANE), lambda i: (i, 0)),
                   row(LANE), row(LANE), pl.BlockSpec((1, SUBLANE, LANE), lambda i: (i, 0, 0))],
        out_shape=[jax.ShapeDtypeStruct((n, D_MODEL), F32),
                   jax.ShapeDtypeStruct((n * ROW_TILE, LANE), F32),
                   jax.ShapeDtypeStruct((n, LANE), I32), jax.ShapeDtypeStruct((n, LANE), F32),
                   jax.ShapeDtypeStruct((n // bm, SUBLANE, LANE), I32)],
        compiler_params=_cparams(("parallel",)),
        name="mix",
    )(og, gla, od, x2d, gnw, wo, nfw, rws, rb)


MOE_ROWS = 256
DMA_UNROLL = 8
TOKEN_BLOCK = 512


def _route_tables(counts, br, n_pairs):
    c = counts[:, 0, :]
    tot = jnp.sum(c, axis=0)
    padded = (tot + br - 1) // br * br
    pad_end = jnp.cumsum(padded)
    pad_start = pad_end - padded
    base = pad_start[None, :] + jnp.cumsum(c, axis=0) - c
    base = jnp.broadcast_to(base[:, None, :], counts.shape).astype(I32)
    nb = -(-(n_pairs + N_EXPERTS * (br - 1)) // br)
    n_used = (pad_end[N_EXPERTS - 1] // br).astype(I32)
    blk = jnp.arange(nb, dtype=I32)
    last = jnp.minimum(blk, n_used - 1) * br
    block_e = jnp.sum((pad_end[None, 0:N_EXPERTS] <= last[:, None]).astype(I32), axis=1)
    ends = pad_end[0:N_EXPERTS].astype(I32)
    tots = jnp.concatenate([tot[0:N_EXPERTS].astype(I32), n_used.reshape(1)])
    return base, jnp.minimum(block_e, N_EXPERTS - 1), n_used.reshape(1), ends, tots


def _pos_kernel(re_ref, base_ref, pos_ref):
    re = re_ref[...]
    lane = lax.broadcasted_iota(I32, re.shape, 1)
    base_row = base_ref[0, 0:1, :].astype(F32)
    p = jnp.zeros(re.shape, F32)
    for k in range(TOP_K):
        hot = lane == re[:, k:k + 1]
        first = jnp.sum(jnp.where(hot, base_row, 0.0), axis=1, keepdims=True)
        p = jnp.where(lane == k, first + re[:, TOP_K + k:TOP_K + k + 1].astype(F32), p)
    pos_ref[0] = jnp.transpose(p)[0:SUBLANE, :].astype(I32) * ROW_TILE


def _positions(re, base, bm):
    n = re.shape[0]
    return pl.pallas_call(
        _pos_kernel,
        grid=(n // bm,),
        in_specs=[pl.BlockSpec((bm, LANE), lambda i: (i, 0)),
                  pl.BlockSpec((1, SUBLANE, LANE), lambda i: (i, 0, 0))],
        out_specs=pl.BlockSpec((1, SUBLANE, bm), lambda i: (i, 0, 0)),
        out_shape=jax.ShapeDtypeStruct((n // bm, SUBLANE, bm), I32),
        compiler_params=_cparams(("parallel",)),
        name="positions",
    )(re, base)


def _dispatch_kernel(end_ref, tot_ref, pos_ref, xa_ref, xb_ref, xr_hbm, zero_scr, sem,
                     *, bt, br, nsteps_a):
    i = pl.program_id(0)

    @pl.when(i == 0)
    def _():
        zero_scr[...] = jnp.zeros(zero_scr.shape, F32)
        n_used = tot_ref[N_EXPERTS]
        nb = xr_hbm.shape[0] // (br * ROW_TILE)

        def fills():
            for e in range(N_EXPERTS):
                yield tot_ref[e] > 0, end_ref[e] - br
            for j in range(N_EXPERTS):
                yield n_used + j < nb, jnp.minimum(n_used + j, nb - 1) * br

        for act in ("start", "wait"):
            for cond, row0 in fills():
                @pl.when(cond)
                def _(row0=row0, act=act):
                    dst = xr_hbm.at[pl.ds(pl.multiple_of(row0 * ROW_TILE, ROW_TILE), br * ROW_TILE)]
                    c = pltpu.make_async_copy(zero_scr, dst, sem.at[1])
                    c.start() if act == "start" else c.wait()

    def scatter(x_ref):
        for k in range(TOP_K):
            def body(r, c, k=k):
                src = x_ref.at[pl.ds(pl.multiple_of(r * ROW_TILE, ROW_TILE), ROW_TILE)]
                dst = xr_hbm.at[pl.ds(pl.multiple_of(pos_ref[0, k, r], ROW_TILE), ROW_TILE)]
                pltpu.make_async_copy(src, dst, sem.at[0]).start()
                return c
            lax.fori_loop(0, bt, body, 0, unroll=DMA_UNROLL)

    @pl.when(i < nsteps_a)
    def _():
        scatter(xa_ref)

    @pl.when(i >= nsteps_a)
    def _():
        scatter(xb_ref)

    done = xr_hbm.at[pl.ds(0, TOP_K * bt * ROW_TILE)]
    pltpu.make_async_copy(done, done, sem.at[0]).wait()


def _dispatch(xn_a, xn_b, pos_t, pad_end, tot, n_rows, bt, br):
    na, nb_ = xn_a.shape[0] // (bt * ROW_TILE), xn_b.shape[0] // (bt * ROW_TILE)
    kern = functools.partial(_dispatch_kernel, bt=bt, br=br, nsteps_a=na)
    grid_spec = pltpu.PrefetchScalarGridSpec(
        num_scalar_prefetch=2,
        grid=(na + nb_,),
        in_specs=[
            pl.BlockSpec((1, SUBLANE, bt), lambda i, e, t: (i, 0, 0), memory_space=pltpu.SMEM),
            pl.BlockSpec((bt * ROW_TILE, LANE), lambda i, e, t: (jnp.minimum(i, na - 1), 0)),
            pl.BlockSpec((bt * ROW_TILE, LANE), lambda i, e, t: (jnp.maximum(i - na, 0), 0)),
        ],
        out_specs=pl.BlockSpec(memory_space=pl.ANY),
        scratch_shapes=[pltpu.VMEM((br * ROW_TILE, LANE), F32), pltpu.SemaphoreType.DMA((2,))],
    )
    return pl.pallas_call(
        kern,
        grid_spec=grid_spec,
        out_shape=jax.ShapeDtypeStruct((n_rows * ROW_TILE, LANE), F32),
        compiler_params=_cparams(("arbitrary",)),
        name="dispatch",
    )(pad_end, tot, pos_t, xn_a, xn_b)


def _rows_to_tiles(ref, x):
    n = x.shape[0]
    for j in range(ROW_TILE):
        ref[pl.ds(j, n, stride=ROW_TILE), :] = x[:, j * LANE:(j + 1) * LANE]


def _tile_chunk(ref, j, n, row0=0):
    return ref[pl.ds(row0 * ROW_TILE + j, n, stride=ROW_TILE), :]


def _rows_from_tiles(ref, n):
    return jnp.concatenate([_tile_chunk(ref, j, n) for j in range(ROW_TILE)], axis=1)


def _moe_kernel(be_ref, nu_ref, x_ref, wgu_ref, bgu_ref, wd_ref, bdn_ref,
                y_ref, wgu_bf, wd_bf):
    i = pl.program_id(0)
    n_used = nu_ref[0]

    @pl.when(i < n_used)
    def _():
        e = be_ref[i]
        prev = be_ref[jnp.maximum(i - 1, 0)]

        @pl.when((i == 0) | (e != prev))
        def _():
            wgu_bf[...] = wgu_ref[0].astype(BF16)
            wd_bf[...] = wd_ref[0].astype(BF16)

        xb = _rows_from_tiles(x_ref, x_ref.shape[0] // ROW_TILE).astype(BF16)
        gu = _dot(xb, wgu_bf[...]) + bgu_ref[0]
        g_lin = jnp.minimum(gu[:, 0:D_FF], SWIGLU_LIMIT)
        u_lin = jnp.clip(gu[:, D_FF:2 * D_FF], -SWIGLU_LIMIT, SWIGLU_LIMIT)
        act = g_lin / (1.0 + jnp.exp(-SWIGLU_ALPHA * g_lin)) * (u_lin + 1.0)
        y = _dot(act.astype(BF16), wd_bf[...]) + bdn_ref[0]
        _rows_to_tiles(y_ref, y)

    @pl.when(i >= n_used)
    def _():
        y_ref[...] = jnp.zeros(y_ref.shape, y_ref.dtype)


def _moe(x_rows, block_e, n_used, w_gu, b_gu, w_down, b_down, br):
    nb = x_rows.shape[0] // (br * ROW_TILE)
    grid_spec = pltpu.PrefetchScalarGridSpec(
        num_scalar_prefetch=2,
        grid=(nb,),
        in_specs=[
            pl.BlockSpec((br * ROW_TILE, LANE),
                         lambda i, be, nu: (jnp.minimum(i, nu[0] - 1), 0)),
            pl.BlockSpec((1, D_MODEL, 2 * D_FF), lambda i, be, nu: (be[i], 0, 0)),
            pl.BlockSpec((1, 1, 2 * D_FF), lambda i, be, nu: (be[i], 0, 0)),
            pl.BlockSpec((1, D_FF, D_MODEL), lambda i, be, nu: (be[i], 0, 0)),
            pl.BlockSpec((1, 1, D_MODEL), lambda i, be, nu: (be[i], 0, 0)),
        ],
        out_specs=pl.BlockSpec((br * ROW_TILE, LANE), lambda i, be, nu: (i, 0)),
        scratch_shapes=[
            pltpu.VMEM((D_MODEL, 2 * D_FF), BF16),
            pltpu.VMEM((D_FF, D_MODEL), BF16),
        ],
    )
    return pl.pallas_call(
        _moe_kernel,
        grid_spec=grid_spec,
        out_shape=jax.ShapeDtypeStruct((nb * br * ROW_TILE, LANE), F32),
        compiler_params=_cparams(("arbitrary",)),
        name="moe",
    )(block_e, n_used, x_rows, w_gu, b_gu.reshape(N_EXPERTS, 1, 2 * D_FF),
      w_down, b_down.reshape(N_EXPERTS, 1, D_MODEL))


def _combine_kernel(pos_ref, posn_ref, y_hbm, h1_ref, rg_ref, fw_ref, o_ref, yg, sem, *, bt):
    i = pl.program_id(0)
    nsteps = pl.num_programs(0)
    slot = lax.rem(i, 2)

    def issue(pref, s):
        for k in range(TOP_K):
            def body(r, c, k=k):
                src = y_hbm.at[pl.ds(pl.multiple_of(pref[0, k, r], ROW_TILE), ROW_TILE)]
                row0 = pl.multiple_of((k * bt + r) * ROW_TILE, ROW_TILE)
                pltpu.make_async_copy(src, yg.at[s, pl.ds(row0, ROW_TILE)], sem.at[s]).start()
                return c
            lax.fori_loop(0, bt, body, 0, unroll=DMA_UNROLL)

    @pl.when(i == 0)
    def _():
        issue(pos_ref, 0)

    @pl.when(i + 1 < nsteps)
    def _():
        issue(posn_ref, 1 - slot)

    pltpu.make_async_copy(yg.at[slot], yg.at[slot], sem.at[slot]).wait()
    gates = [jnp.broadcast_to(rg_ref[:, k:k + 1], (bt, LANE)) for k in range(TOP_K)]
    ss = jnp.zeros((bt, LANE), F32)
    rows = yg.at[slot]
    for j in range(ROW_TILE):
        js = slice(j * LANE, (j + 1) * LANE)
        a = h1_ref[:, js]
        for k in range(TOP_K):
            a = a + gates[k] * _tile_chunk(rows, j, bt, k * bt)
        ss = ss + a * a
        o_ref[:, js] = a
    ms = jnp.sum(ss, axis=-1, keepdims=True) * (1.0 / D_MODEL)
    o_ref[...] = o_ref[...] * lax.rsqrt(ms + EPS) * fw_ref[...]


def _combine(y_rows, pos_t, h1, rg, final_w, bt):
    n = h1.shape[0]
    nsteps = n // bt
    kern = functools.partial(_combine_kernel, bt=bt)
    row = lambda w: pl.BlockSpec((bt, w), lambda i: (i, 0))
    return pl.pallas_call(
        kern,
        grid=(nsteps,),
        in_specs=[
            pl.BlockSpec((1, SUBLANE, bt), lambda i: (i, 0, 0), memory_space=pltpu.SMEM),
            pl.BlockSpec((1, SUBLANE, bt), lambda i: (jnp.minimum(i + 1, nsteps - 1), 0, 0),
                         memory_space=pltpu.SMEM),
            pl.BlockSpec(memory_space=pl.ANY),
            row(D_MODEL), row(LANE),
            pl.BlockSpec((1, D_MODEL), lambda i: (0, 0)),
        ],
        out_specs=row(D_MODEL),
        out_shape=jax.ShapeDtypeStruct((n, D_MODEL), F32),
        scratch_shapes=[pltpu.VMEM((2, TOP_K * bt * ROW_TILE, LANE), F32),
                        pltpu.SemaphoreType.DMA((2,))],
        compiler_params=_cparams(("arbitrary",)),
        name="combine",
    )(pos_t, pos_t, y_rows, h1, rg, final_w.reshape(1, D_MODEL))


def kernel(x_prompt, x_sample, cache_k, cache_v, cache_k_idx, state_gla, norm_mix_w, w_in,
           w_gla_a2, b_gla_a2, gla_norm_w, w_out, norm_ffn_w, router_w, router_b, w_gu, b_gu,
           w_down, b_down, norm_final_w):
    b_p, t_p, _ = x_prompt.shape
    b_s, t_s, _ = x_sample.shape
    past = cache_k.shape[2]
    assert b_p == 1 and norm_mix_w.shape[0] == 1
    l = 0
    pw = _proj_weights(norm_mix_w[l], w_in[l])
    gconsts = _gla_consts(w_gla_a2[l], b_gla_a2[l])

    pp = _project_t(x_prompt.reshape(t_p, D_MODEL), _proj_t_weights(norm_mix_w[l], w_in[l]), DSA_BK)
    gla_p, kf_p, vf_p, kb_p, ikf_p, ikb_p, g2_p, qt_p, iqt_p, vt3_p, g2t_p = pp
    s0 = jnp.zeros((1, GLA_WIDTH, GLA_QK), F32)
    og_p, st_p = _gla(gla_p, g2_p, s0, gconsts, 1, t_p, 512, 16)
    od_p = _dsa_prompt_t(qt_p, iqt_p, g2t_p, ikb_p, kb_p, vt3_p, DSA_BQ, DSA_BK)

    ps = _project(x_sample.reshape(b_s * t_s, D_MODEL), pw, 512)
    gla_s, q_s, kf_s, vf_s, kb_s, vb_s, iq_s, iklo_s, ikhi_s, ikf_s, g2_s = ps
    og_s, st_s = _gla(gla_s, g2_s, _state_to_kernel(state_gla[l]), gconsts, b_s, t_s, t_s, 16)
    zc = jnp.zeros((b_s, past, IDX_DIM), BF16)
    cik = cache_k_idx[l].astype(BF16)
    cilo = jnp.concatenate([cik, zc], axis=2)
    cihi = jnp.concatenate([zc, cik], axis=2)
    od_s = _dsa_sample(q_s, iq_s, g2_s, cache_k[l], cache_v[l], cilo, cihi,
                       kb_s, vb_s, iklo_s, ikhi_s, t_s, 512)

    mconsts = _mix_consts(gla_norm_w[l], w_out[l], norm_ffn_w[l], router_w[l], router_b[l])
    n_s = b_s * t_s
    tb = TOKEN_BLOCK
    h1_p, xn_p, re_p, rg_p, cnt_p = _mix(og_p, gla_p, od_p, x_prompt.reshape(t_p, D_MODEL), mconsts, tb)
    h1_s, xn_s, re_s, rg_s, cnt_s = _mix(og_s, gla_s, od_s, x_sample.reshape(n_s, D_MODEL), mconsts, tb)
    n_pairs = (t_p + n_s) * TOP_K
    base, block_e, n_used, ends, tots = _route_tables(
        jnp.concatenate([cnt_p, cnt_s], axis=0), MOE_ROWS, n_pairs)
    n_rows = block_e.shape[0] * MOE_ROWS
    pos_p = _positions(re_p, base[0:t_p // tb], tb)
    pos_s = _positions(re_s, base[t_p // tb:], tb)
    x_rows = _dispatch(xn_p, xn_s, jnp.concatenate([pos_p, pos_s], axis=0), ends, tots,
                       n_rows, tb, MOE_ROWS)
    y_rows = _moe(x_rows, block_e, n_used, w_gu[l], b_gu[l], w_down[l], b_down[l], MOE_ROWS)
    y_p = _combine(y_rows, pos_p, h1_p, rg_p, norm_final_w, tb)
    y_s = _combine(y_rows, pos_s, h1_s, rg_s, norm_final_w, tb)
    y_p = y_p.reshape(x_prompt.shape)
    y_s = y_s.reshape(x_sample.shape)
    return (y_p, y_s,
            kf_p.reshape(1, 1, t_p, DSA_HEADS, DSA_DH), vf_p.reshape(1, 1, t_p, DSA_HEADS, DSA_DH),
            ikf_p.reshape(1, 1, t_p, IDX_DIM), _state_from_kernel(st_p)[None],
            kf_s.reshape(1, b_s, t_s, DSA_HEADS, DSA_DH), vf_s.reshape(1, b_s, t_s, DSA_HEADS, DSA_DH),
            ikf_s.reshape(1, b_s, t_s, IDX_DIM), _state_from_kernel(st_s)[None])
```

```python
import functools

import jax
import jax.numpy as jnp
from jax import lax
from jax.experimental import pallas as pl
from jax.experimental.pallas import tpu as pltpu

F32 = jnp.float32
BF16 = jnp.bfloat16
I32 = jnp.int32

D_MODEL = 1024
CHUNK = 64
GLA_HEADS = 4
GLA_DK = 64
GLA_DV = 128
GLA_QK = GLA_HEADS * GLA_DK
GLA_WIDTH = GLA_HEADS * GLA_DV
GLA_GATE_RANK = 16
GLA_GATE_TAU = 16.0
DSA_HEADS = 4
DSA_DH = 128
DSA_WIDTH = DSA_HEADS * DSA_DH
IDX_HEADS = 8
IDX_DIM = 64
IDX_TOPK_MAX = 256
N_EXPERTS = 32
TOP_K = 4
D_FF = 1024
SWIGLU_ALPHA = 1.702
SWIGLU_LIMIT = 7.0
EPS = 1e-6

_OFF_GLA = 0
_W_GLA = 2 * GLA_QK + 2 * GLA_WIDTH
_OFF_GA = _OFF_GLA + _W_GLA
_OFF_DSA = _OFF_GA + GLA_GATE_RANK
_W_DSA = 3 * DSA_WIDTH + IDX_HEADS * IDX_DIM
_OFF_IK = _OFF_DSA + _W_DSA
_OFF_IW = _OFF_IK + IDX_DIM
LANE = 128
SUBLANE = 8
ROW_TILE = D_MODEL // LANE
assert ROW_TILE == SUBLANE
LOG2E = 1.4426950408889634
NEG_BIG = -1e30
VMEM_LIMIT = 56 * 1024 * 1024


def _cparams(sem):
    return pltpu.CompilerParams(dimension_semantics=sem, vmem_limit_bytes=VMEM_LIMIT)


def _dot(a, b):
    return jnp.dot(a, b, preferred_element_type=F32)


def _dot_nt(a, b):
    return lax.dot_general(a, b, (((1,), (1,)), ((), ())), preferred_element_type=F32)


def _dot_tn(a, b):
    return lax.dot_general(a, b, (((0,), (0,)), ((), ())), preferred_element_type=F32)


def _split3(x):
    hi = x.astype(BF16)
    r1 = x - hi.astype(F32)
    mid = r1.astype(BF16)
    lo = (r1 - mid.astype(F32)).astype(BF16)
    return hi, mid, lo


def _proj_kernel(x_ref, nw_ref, wg_ref, wd_ref, ws_ref,
                 gla_ref, q_ref, kf_ref, vf_ref, kb_ref, vb_ref, iq_ref,
                 iklo_ref, ikhi_ref, ikf_ref, g2_ref):
    x = x_ref[...]
    ms = jnp.mean(x * x, axis=-1, keepdims=True)
    xn = (x * lax.rsqrt(ms + EPS) * nw_ref[...]).astype(BF16)
    gla_ref[...] = _dot(xn, wg_ref[...]).astype(BF16)
    W = DSA_WIDTH
    dq = _dot(xn, wd_ref[:, 0:W])
    q_ref[...] = (dq * (DSA_DH ** -0.5 * LOG2E)).astype(BF16)
    dk = _dot(xn, wd_ref[:, W:2 * W])
    kf_ref[...] = dk
    kb_ref[...] = dk.astype(BF16)
    dv = _dot(xn, wd_ref[:, 2 * W:3 * W])
    vf_ref[...] = dv
    vb_ref[...] = dv.astype(BF16)
    iq_ref[...] = _dot(xn, wd_ref[:, 3 * W:4 * W]).astype(BF16)
    sm = _dot(xn, ws_ref[...])
    iklo_ref[...] = sm[:, 0:LANE].astype(BF16)
    ikhi_ref[...] = sm[:, LANE:2 * LANE].astype(BF16)
    ikf_ref[...] = sm[:, 0:IDX_DIM]
    g2_ref[...] = sm[:, 2 * LANE:3 * LANE]


def _proj_weights(norm_w, w_in):
    wg = w_in[:, _OFF_GLA:_OFF_GLA + _W_GLA].astype(BF16)
    wd = w_in[:, _OFF_DSA:_OFF_DSA + _W_DSA].astype(BF16)
    ik = w_in[:, _OFF_IK:_OFF_IK + IDX_DIM]
    z64 = jnp.zeros((D_MODEL, IDX_DIM), F32)
    ga = w_in[:, _OFF_GA:_OFF_GA + GLA_GATE_RANK]
    iw = w_in[:, _OFF_IW:_OFF_IW + IDX_HEADS]
    zpad = jnp.zeros((D_MODEL, LANE - GLA_GATE_RANK - IDX_HEADS), F32)
    ws = jnp.concatenate([ik, z64, z64, ik, ga, iw, zpad], axis=1).astype(BF16)
    return norm_w.reshape(1, D_MODEL), wg, wd, ws


def _project(x2d, pw, bm):
    n = x2d.shape[0]
    nw, wg, wd, ws = pw
    row = lambda w: pl.BlockSpec((bm, w), lambda i: (i, 0))
    full = lambda a: pl.BlockSpec(a.shape, lambda i: (0, 0))
    outs = [(_W_GLA, BF16), (DSA_WIDTH, BF16), (DSA_WIDTH, F32), (DSA_WIDTH, F32),
            (DSA_WIDTH, BF16), (DSA_WIDTH, BF16), (IDX_HEADS * IDX_DIM, BF16),
            (LANE, BF16), (LANE, BF16), (IDX_DIM, F32), (LANE, F32)]
    return pl.pallas_call(
        _proj_kernel,
        grid=(n // bm,),
        in_specs=[row(D_MODEL), full(nw), full(wg), full(wd), full(ws)],
        out_specs=[row(w) for w, _ in outs],
        out_shape=[jax.ShapeDtypeStruct((n, w), dt) for w, dt in outs],
        compiler_params=_cparams(("parallel",)),
        name="proj",
    )(x2d, nw, wg, wd, ws)


def _proj_t_kernel(x_ref, nw_ref, wg_ref, wkv_ref, ws_ref, wt_ref, wst_ref,
                   gla_ref, kf_ref, vf_ref, kb_ref, ikf_ref, ikb_ref, g2_ref,
                   qt_ref, iqt_ref, vt_ref, g2t_ref):
    x = x_ref[...]
    ms = jnp.mean(x * x, axis=-1, keepdims=True)
    xn = (x * lax.rsqrt(ms + EPS) * nw_ref[...]).astype(BF16)
    gla_ref[...] = _dot(xn, wg_ref[...]).astype(BF16)
    W = DSA_WIDTH
    dk = _dot(xn, wkv_ref[:, 0:W])
    dv = _dot(xn, wkv_ref[:, W:2 * W])
    kb_ref[...] = dk.astype(BF16)
    for h in range(DSA_HEADS):
        kf_ref[:, h, :] = dk[:, h * DSA_DH:(h + 1) * DSA_DH]
        vf_ref[:, h, :] = dv[:, h * DSA_DH:(h + 1) * DSA_DH]
    sm = _dot(xn, ws_ref[...])
    ikb_ref[...] = sm[:, 0:LANE].astype(BF16)
    ikf_ref[...] = sm[:, 0:IDX_DIM]
    g2_ref[...] = sm[:, LANE:2 * LANE]
    qt_ref[...] = (_dot_nt(wt_ref[0:W, :], xn) * (DSA_DH ** -0.5 * LOG2E)).astype(BF16)
    vt_ref[0] = _dot_nt(wt_ref[W:2 * W, :], xn).astype(BF16)
    iqt_ref[...] = _dot_nt(wt_ref[2 * W:3 * W, :], xn).astype(BF16)
    g2t_ref[...] = _dot_nt(wst_ref[...], xn)


def _proj_t_weights(norm_w, w_in):
    wg = w_in[:, _OFF_GLA:_OFF_GLA + _W_GLA].astype(BF16)
    W = DSA_WIDTH
    wq = w_in[:, _OFF_DSA:_OFF_DSA + W]
    wkv = w_in[:, _OFF_DSA + W:_OFF_DSA + 3 * W]
    wv = w_in[:, _OFF_DSA + 2 * W:_OFF_DSA + 3 * W]
    wiq = w_in[:, _OFF_DSA + 3 * W:_OFF_DSA + 4 * W]
    ik = w_in[:, _OFF_IK:_OFF_IK + IDX_DIM]
    z64 = jnp.zeros((D_MODEL, IDX_DIM), F32)
    ga = w_in[:, _OFF_GA:_OFF_GA + GLA_GATE_RANK]
    iw = w_in[:, _OFF_IW:_OFF_IW + IDX_HEADS]
    zpad = jnp.zeros((D_MODEL, LANE - GLA_GATE_RANK - IDX_HEADS), F32)
    g2w = jnp.concatenate([ga, iw, zpad], axis=1)
    ws = jnp.concatenate([ik, z64, g2w], axis=1).astype(BF16)
    wt = jnp.concatenate([wq, wv, wiq], axis=1).T.astype(BF16)
    return norm_w.reshape(1, D_MODEL), wg, wkv.astype(BF16), ws, wt, g2w.T.astype(BF16)


def _project_t(x2d, pw, bm):
    n = x2d.shape[0]
    row = lambda w: pl.BlockSpec((bm, w), lambda i: (i, 0))
    row3 = pl.BlockSpec((bm, DSA_HEADS, DSA_DH), lambda i: (i, 0, 0))
    col = lambda h: pl.BlockSpec((h, bm), lambda i: (0, i))
    full = lambda a: pl.BlockSpec(a.shape, lambda i: (0, 0))
    out_specs = [row(_W_GLA), row3, row3, row(DSA_WIDTH), row(IDX_DIM), row(LANE), row(LANE),
                 col(DSA_WIDTH), col(IDX_HEADS * IDX_DIM),
                 pl.BlockSpec((1, DSA_WIDTH, bm), lambda i: (i, 0, 0)), col(LANE)]
    out_shape = [jax.ShapeDtypeStruct((n, _W_GLA), BF16),
                 jax.ShapeDtypeStruct((n, DSA_HEADS, DSA_DH), F32),
                 jax.ShapeDtypeStruct((n, DSA_HEADS, DSA_DH), F32),
                 jax.ShapeDtypeStruct((n, DSA_WIDTH), BF16),
                 jax.ShapeDtypeStruct((n, IDX_DIM), F32),
                 jax.ShapeDtypeStruct((n, LANE), BF16),
                 jax.ShapeDtypeStruct((n, LANE), F32),
                 jax.ShapeDtypeStruct((DSA_WIDTH, n), BF16),
                 jax.ShapeDtypeStruct((IDX_HEADS * IDX_DIM, n), BF16),
                 jax.ShapeDtypeStruct((n // bm, DSA_WIDTH, bm), BF16),
                 jax.ShapeDtypeStruct((LANE, n), F32)]
    return pl.pallas_call(
        _proj_t_kernel,
        grid=(n // bm,),
        in_specs=[row(D_MODEL)] + [full(a) for a in pw],
        out_specs=out_specs,
        out_shape=out_shape,
        compiler_params=_cparams(("parallel",)),
        name="proj_t",
    )(x2d, *pw)


def _log_sigmoid(x):
    return jnp.minimum(x, 0.0) - jnp.log1p(jnp.exp(-jnp.abs(x)))


def _gla_kernel(gla_ref, g2_ref, w2_ref, b2_ref, seg_ref, bd_ref, s0_ref,
                o_ref, sT_ref,
                st_scr, kpad, bpad, vpad, qt_scr, kt_scr, dec_scr, oi_scr,
                *, bt, c):
    t = pl.program_id(1)

    @pl.when(t == 0)
    def _():
        st_scr[...] = s0_ref[0]

    q = gla_ref[:, 0:GLA_QK].astype(F32) * (GLA_DK ** -0.5)
    k = gla_ref[:, GLA_QK:2 * GLA_QK].astype(F32)
    vb = gla_ref[:, 2 * GLA_QK:2 * GLA_QK + GLA_WIDTH]
    v = vb.astype(F32)

    ga = g2_ref[...]
    ga_hi = ga.astype(BF16)
    ga_lo = (ga - ga_hi.astype(F32)).astype(BF16)
    logit = (_dot(ga_hi, w2_ref[0]) + _dot(ga_hi, w2_ref[1]) + _dot(ga_lo, w2_ref[0])
             + b2_ref[...])
    lg = _log_sigmoid(logit) * (1.0 / GLA_GATE_TAU)

    shift = c.bit_length() - 1
    row = lax.broadcasted_iota(I32, (bt, bt), 0)
    col = lax.broadcasted_iota(I32, (bt, bt), 1)
    same = (row >> shift) == (col >> shift)
    tri = jnp.where(same & (col <= row), 1.0, 0.0).astype(BF16)
    last = jnp.where(same & ((col & (c - 1)) == c - 1), 1.0, 0.0).astype(BF16)
    l0, l1, l2 = _split3(lg)
    b = _dot(tri, l0) + _dot(tri, l1) + _dot(tri, l2)
    b0, b1, b2s = _split3(b)
    bl = _dot(last, b0) + _dot(last, b1) + _dot(last, b2s)

    zpad = jnp.zeros((c, GLA_QK), F32)
    kpad[0:c, :] = zpad
    bpad[0:c, :] = zpad
    vpad[0:c, :] = jnp.zeros((c, GLA_WIDTH), F32)
    kpad[c:c + bt, :] = k
    bpad[c:c + bt, :] = b
    vpad[c:c + bt, :] = v
    pos = lax.broadcasted_iota(I32, (bt, GLA_QK), 0) & (c - 1)
    seg = seg_ref[...]
    o_intra = jnp.zeros((bt, GLA_WIDTH), F32)
    for d in range(c):
        ks = kpad[c - d:c - d + bt, :]
        bs = bpad[c - d:c - d + bt, :]
        vs = vpad[c - d:c - d + bt, :]
        z = q * ks * jnp.exp(jnp.minimum(b - bs, 0.0))
        z = jnp.where(pos >= d, z, 0.0)
        o_intra = o_intra + _dot(z.astype(BF16), seg) * vs
    oi_scr[...] = o_intra

    qt_scr[...] = q * jnp.exp(b)
    kt_scr[...] = k * jnp.exp(bl - b)
    dec_scr[...] = jnp.exp(bl)
    bd = bd_ref[...]

    def step(ci, carry):
        r0 = pl.multiple_of(ci * c, c)
        qc = qt_scr[pl.ds(r0, c), :].astype(BF16)
        kc = kt_scr[pl.ds(r0, c), :].astype(BF16)
        vc = vpad[pl.ds(r0 + c, c), :].astype(BF16)
        st = st_scr[...]
        o_ref[pl.ds(r0, c), :] = oi_scr[pl.ds(r0, c), :] + _dot_nt(qc, st.astype(BF16))
        dec = dec_scr[pl.ds(r0, 1), :]
        st_scr[...] = st * dec + _dot_tn(vc, kc) * bd
        return carry

    lax.fori_loop(0, bt // c, step, 0)

    @pl.when(t == pl.num_programs(1) - 1)
    def _():
        sT_ref[0] = st_scr[...]


def _gla_consts(w_gla_a2, b_gla_a2):
    w2 = jnp.zeros((LANE, GLA_QK), F32).at[0:GLA_GATE_RANK].set(w_gla_a2)
    w2_hi = w2.astype(BF16)
    w2_lo = (w2 - w2_hi.astype(F32)).astype(BF16)
    w2s = jnp.stack([w2_hi, w2_lo])
    hq = jnp.arange(GLA_QK) // GLA_DK
    hv = jnp.arange(GLA_WIDTH) // GLA_DV
    seg = (hq[:, None] == hv[None, :]).astype(BF16)
    bd = (hv[:, None] == hq[None, :]).astype(F32)
    return w2s, b_gla_a2.reshape(1, GLA_QK), seg, bd


def _state_to_kernel(s):
    n = s.shape[0]
    eye = jnp.eye(GLA_HEADS, dtype=s.dtype)
    t = jnp.swapaxes(s, 2, 3)[:, :, :, None, :] * eye[None, :, None, :, None]
    return t.reshape(n, GLA_WIDTH, GLA_QK)


def _state_from_kernel(t):
    n = t.shape[0]
    t5 = t.reshape(n, GLA_HEADS, GLA_DV, GLA_HEADS, GLA_DK)
    return jnp.einsum("shehd->shde", t5)


def _gla(gla, g2, s0, consts, n_streams, t_len, bt, c):
    w2s, b2, seg, bd = consts
    nb = t_len // bt
    kern = functools.partial(_gla_kernel, bt=bt, c=c)
    const = lambda a: pl.BlockSpec(a.shape, lambda s, t: (0,) * a.ndim)
    o, sT = pl.pallas_call(
        kern,
        grid=(n_streams, nb),
        in_specs=[
            pl.BlockSpec((bt, 2 * GLA_QK + GLA_WIDTH), lambda s, t: (s * nb + t, 0)),
            pl.BlockSpec((bt, LANE), lambda s, t: (s * nb + t, 0)),
            const(w2s), const(b2), const(seg), const(bd),
            pl.BlockSpec((1, GLA_WIDTH, GLA_QK), lambda s, t: (s, 0, 0)),
        ],
        out_specs=[
            pl.BlockSpec((bt, GLA_WIDTH), lambda s, t: (s * nb + t, 0)),
            pl.BlockSpec((1, GLA_WIDTH, GLA_QK), lambda s, t: (s, 0, 0)),
        ],
        out_shape=[
            jax.ShapeDtypeStruct((n_streams * t_len, GLA_WIDTH), F32),
            jax.ShapeDtypeStruct((n_streams, GLA_WIDTH, GLA_QK), F32),
        ],
        scratch_shapes=[
            pltpu.VMEM((GLA_WIDTH, GLA_QK), F32),
            pltpu.VMEM((bt + c, GLA_QK), F32),
            pltpu.VMEM((bt + c, GLA_QK), F32),
            pltpu.VMEM((bt + c, GLA_WIDTH), F32),
            pltpu.VMEM((bt, GLA_QK), F32),
            pltpu.VMEM((bt, GLA_QK), F32),
            pltpu.VMEM((bt, GLA_QK), F32),
            pltpu.VMEM((bt, GLA_WIDTH), F32),
        ],
        compiler_params=_cparams(("arbitrary", "arbitrary")),
        name="gla",
    )(gla, g2, w2s, b2, seg, bd, s0)
    return o, sT


INT_MIN = -2 ** 31
NEG_INF = float("-inf")
KEY_NEG_INF = -2139095041
F32_LOWEST = -3.4028234663852886e38
IDX_SCALE = IDX_DIM ** -0.5 * IDX_HEADS ** -0.5


def _fill_head_weights(wb_scr, g2_ref, bq):
    for h in range(IDX_HEADS):
        c0 = GLA_GATE_RANK + h
        col = g2_ref[:, c0:c0 + 1] * IDX_SCALE
        wb_scr[h] = jnp.broadcast_to(col, (bq, LANE))


def _index_scores(iq_ref, wb_scr, iklo, ikhi, bk):
    parts = [None] * (bk // LANE)
    for p in range(IDX_HEADS // 2):
        iqp = iq_ref[:, p * LANE:(p + 1) * LANE]
        y0 = jnp.maximum(_dot_nt(iqp, iklo), 0.0)
        y1 = jnp.maximum(_dot_nt(iqp, ikhi), 0.0)
        w0 = wb_scr[2 * p]
        w1 = wb_scr[2 * p + 1]
        for c in range(bk // LANE):
            t = y0[:, c * LANE:(c + 1) * LANE] * w0 + y1[:, c * LANE:(c + 1) * LANE] * w1
            parts[c] = t if parts[c] is None else parts[c] + t
    return jnp.concatenate(parts, axis=1)


def _key_to_float(key):
    key = jnp.maximum(key, KEY_NEG_INF)
    return pltpu.bitcast(key ^ ((key >> 31) & 0x7FFFFFFF), F32)


def _kth_largest(count, total, topk, shape):
    def bit_body(i, carry):
        prefix, n_ge = carry
        cand_u = prefix | jnp.left_shift(jnp.int32(1), 31 - i)
        cnt = count(_key_to_float(cand_u ^ INT_MIN), False)
        ok = cnt >= topk
        return jnp.where(ok, cand_u, prefix), jnp.where(ok, cnt, n_ge)

    init = (jnp.zeros(shape, I32), jnp.zeros(shape, I32) + total)
    prefix, n_ge = lax.fori_loop(0, 32, bit_body, init)
    t = _key_to_float(prefix ^ INT_MIN)
    return t, n_ge, count(t, True)


def _count_rows(load_tile, n_tiles, cand, strict, bq, bk):
    cand_b = jnp.broadcast_to(cand, (bq, LANE))

    def body(kt, acc):
        s = load_tile(kt)
        for c in range(bk // LANE):
            sc = s[:, c * LANE:(c + 1) * LANE]
            acc = acc + jnp.where(sc > cand_b if strict else sc >= cand_b, 1, 0)
        return acc

    acc = lax.fori_loop(0, n_tiles, body, jnp.zeros((bq, LANE), I32))
    return jnp.sum(acc, axis=1, keepdims=True)


def _attend_tile(sel, q_ref, kv_tile, m_scr, l_scr, acc_scr):
    for h in range(DSA_HEADS):
        hs = slice(h * DSA_DH, (h + 1) * DSA_DH)
        kh, vh = kv_tile(h)
        lg = jnp.where(sel, _dot_nt(q_ref[:, hs], kh), NEG_BIG)
        m_old = m_scr[h]
        m_new = jnp.maximum(m_old, jnp.max(lg, axis=1, keepdims=True))
        alpha = jnp.exp2(m_old - m_new)
        p = jnp.exp2(lg - m_new)
        l_scr[h] = alpha * l_scr[h] + jnp.sum(p, axis=1, keepdims=True)
        acc_scr[:, hs] = alpha * acc_scr[:, hs] + _dot(p.astype(BF16), vh)
        m_scr[h] = m_new


def _select(s, t, n_gt, eq_before, topk, upper, ties):
    if not ties:
        return s >= jnp.maximum(t, F32_LOWEST), eq_before
    eq = s == t
    eq_f = jnp.where(eq, 1.0, 0.0).astype(BF16)
    rank = eq_before + _dot(eq_f, upper)
    need = (topk - n_gt).astype(F32)
    sel = ((s > t) | (eq & (rank < need))) & (s > NEG_INF)
    return sel, eq_before + jnp.sum(eq_f.astype(F32), axis=1, keepdims=True)


def _init_softmax(m_scr, l_scr, acc_scr):
    m_scr[...] = jnp.full(m_scr.shape, NEG_BIG, F32)
    l_scr[...] = jnp.zeros(l_scr.shape, F32)
    acc_scr[...] = jnp.zeros(acc_scr.shape, F32)


def _finish_softmax(o_ref, l_scr, acc_scr):
    for h in range(DSA_HEADS):
        hs = slice(h * DSA_DH, (h + 1) * DSA_DH)
        o_ref[:, hs] = (acc_scr[:, hs] / l_scr[h]).astype(o_ref.dtype)


def _upper_ones(n):
    r = jnp.arange(n)
    return (r[:, None] < r[None, :]).astype(BF16)


DSA_BQ = 256
DSA_BK = 512
V_AHEAD = 3
V_RING = V_AHEAD + 1

def _count_cols(sc_scr, n_tiles, cand, strict, bq, bk):
    cand_b = jnp.broadcast_to(cand, (SUBLANE, bq))
    n_acc = 4

    def body(kt, accs):
        accs = list(accs)
        for r in range(bk // SUBLANE):
            s = sc_scr[kt, r * SUBLANE:(r + 1) * SUBLANE, :]
            accs[r % n_acc] = accs[r % n_acc] + jnp.where(s > cand_b if strict else s >= cand_b, 1, 0)
        return tuple(accs)

    zero = jnp.zeros((SUBLANE, bq), I32)
    accs = lax.fori_loop(0, n_tiles, body, (zero,) * n_acc)
    return jnp.sum((accs[0] + accs[1]) + (accs[2] + accs[3]), axis=0, keepdims=True)


def _dsa_prompt_t_kernel(qt_ref, iqt_ref, g2t_ref, ik_ref, lower_ref, k_ref, vt_hbm,
                         o_ref, sc_scr, vbuf, sem, acc_scr, p_scr,
                         *, bq, bk, topk):
    qb = pl.program_id(0)
    q0 = qb * bq
    n_tiles = (q0 + bq + bk - 1) // bk
    sub = LANE

    def tile_scores(kt, masked):
        for s in range(bk // sub):
            ik_s = ik_ref[kt, s * sub:(s + 1) * sub, 0:IDX_DIM]
            acc = None
            for h in range(IDX_HEADS):
                y = _dot(ik_s, iqt_ref[h * IDX_DIM:(h + 1) * IDX_DIM, :])
                w = g2t_ref[GLA_GATE_RANK + h:GLA_GATE_RANK + h + 1, :] * IDX_SCALE
                t = jnp.maximum(y, 0.0) * w
                acc = t if acc is None else acc + t
            if masked:
                kpos = kt * bk + s * sub + lax.broadcasted_iota(I32, (sub, bq), 0)
                qpos = q0 + lax.broadcasted_iota(I32, (sub, bq), 1)
                shift = CHUNK.bit_length() - 1
                acc = jnp.where((kpos >> shift) <= (qpos >> shift), acc, NEG_INF)
            sc_scr[kt, s * sub:(s + 1) * sub, :] = acc

    def score_body(kt, carry):
        tile_scores(kt, False)
        return carry

    lax.fori_loop(0, n_tiles - 1, score_body, 0)
    tile_scores(n_tiles - 1, True)

    count = lambda cand, strict: _count_cols(sc_scr, n_tiles, cand, strict, bq, bk)
    t, n_ge, n_gt = _kth_largest(count, n_tiles * bk, topk, (1, bq))
    tie_q = jnp.where((n_ge > topk) & (t > NEG_INF), 1, 0)
    any_tie = jnp.max(tie_q) > 0

    def v_copy(kt):
        slot = lax.rem(kt, V_RING)
        return pltpu.make_async_copy(vt_hbm.at[kt], vbuf.at[slot], sem.at[slot])

    def v_prefetch(kt):
        @pl.when(kt < n_tiles)
        def _():
            v_copy(kt).start()

    def select(kt, eq_before, ties):
        s = sc_scr[kt]
        if not ties:
            return s >= jnp.maximum(t, F32_LOWEST), eq_before
        eq = s == t
        eq_b = jnp.where(eq, 1.0, 0.0).astype(BF16)
        rank = eq_before + _dot(lower_ref[...], eq_b)
        need = (topk - n_gt).astype(F32)
        sel = ((s > t) | (eq & (rank < need))) & (s > NEG_INF)
        return sel, eq_before + jnp.sum(eq_b.astype(F32), axis=0, keepdims=True)

    def logits(kt, h, sel):
        hs = slice(h * DSA_DH, (h + 1) * DSA_DH)
        return jnp.where(sel, _dot(k_ref[kt, :, hs], qt_ref[hs, :]), NEG_BIG)

    def attend(ties):
        zero = jnp.zeros((1, bq), F32)
        for j in range(V_AHEAD):
            v_prefetch(j)

        def max_body(kt, carry):
            ms, eq_before = carry
            sel, eq_before = select(kt, eq_before, ties)
            ms = tuple(jnp.maximum(ms[h], jnp.max(logits(kt, h, sel), axis=0, keepdims=True))
                       for h in range(DSA_HEADS))
            return ms, eq_before

        neg = jnp.full((1, bq), NEG_BIG, F32)
        ms, _ = lax.fori_loop(0, n_tiles, max_body, ((neg,) * DSA_HEADS, zero))

        acc_scr[...] = jnp.zeros(acc_scr.shape, F32)

        def sum_body(kt, carry):
            ls, eq_before = carry
            v_prefetch(kt + V_AHEAD)
            sel, eq_before = select(kt, eq_before, ties)
            new_ls = []
            for h in range(DSA_HEADS):
                p = jnp.exp2(logits(kt, h, sel) - ms[h])
                new_ls.append(ls[h] + jnp.sum(p, axis=0, keepdims=True))
                p_scr[h] = p.astype(BF16)
            v_copy(kt).wait()
            slot = lax.rem(kt, V_RING)
            for h in range(DSA_HEADS):
                hs = slice(h * DSA_DH, (h + 1) * DSA_DH)
                acc_scr[hs, :] += _dot(vbuf[slot, hs, :], p_scr[h])
            return tuple(new_ls), eq_before

        ls, _ = lax.fori_loop(0, n_tiles, sum_body, ((zero,) * DSA_HEADS, zero))
        for h in range(DSA_HEADS):
            hs = slice(h * DSA_DH, (h + 1) * DSA_DH)
            acc_scr[hs, :] = acc_scr[hs, :] / ls[h]
        o_ref[...] = jnp.transpose(acc_scr[...]).astype(o_ref.dtype)

    @pl.when(jnp.logical_not(any_tie))
    def _():
        attend(False)

    @pl.when(any_tie)
    def _():
        attend(True)


def _dsa_prompt_t(qt, iqt, g2t, ikb, kb, vt3, bq, bk):
    t_len = qt.shape[1]
    n_kt = t_len // bk
    assert vt3.shape == (n_kt, DSA_WIDTH, bk) and bk % bq == 0
    topk = min(IDX_TOPK_MAX, t_len // 4)
    r = jnp.arange(bk)
    lower = (r[None, :] < r[:, None]).astype(BF16)
    kern = functools.partial(_dsa_prompt_t_kernel, bq=bq, bk=bk, topk=topk)
    col = lambda h: pl.BlockSpec((h, bq), lambda i: (0, i))
    ik3 = ikb.reshape(n_kt, bk, LANE)
    k3 = kb.reshape(n_kt, bk, DSA_WIDTH)
    return pl.pallas_call(
        kern,
        grid=(t_len // bq,),
        in_specs=[col(DSA_WIDTH), col(IDX_HEADS * IDX_DIM), col(LANE),
                  pl.BlockSpec(ik3.shape, lambda i: (0, 0, 0)),
                  pl.BlockSpec(lower.shape, lambda i: (0, 0)),
                  pl.BlockSpec(k3.shape, lambda i: (0, 0, 0)),
                  pl.BlockSpec(memory_space=pl.ANY)],
        out_specs=pl.BlockSpec((bq, DSA_WIDTH), lambda i: (i, 0)),
        out_shape=jax.ShapeDtypeStruct((t_len, DSA_WIDTH), BF16),
        scratch_shapes=[
            pltpu.VMEM((n_kt, bk, bq), F32),
            pltpu.VMEM((V_RING, DSA_WIDTH, bk), BF16),
            pltpu.SemaphoreType.DMA((V_RING,)),
            pltpu.VMEM((DSA_WIDTH, bq), F32),
            pltpu.VMEM((DSA_HEADS, bk, bq), BF16),
        ],
        compiler_params=_cparams(("arbitrary",)),
        name="dsa_prompt_t",
    )(qt, iqt, g2t, ik3, lower, k3, vt3)


def _dsa_sample_kernel(q_ref, iq_ref, g2_ref, ck_ref, cv_ref, cilo_ref, cihi_ref,
                       nk_ref, nv_ref, nilo_ref, nihi_ref, upper_ref, o_ref,
                       sc_scr, wb_scr, m_scr, l_scr, acc_scr, pk_scr, pv_scr, plo_scr, phi_scr,
                       *, bq, bk, past, topk):
    n_cache = past // bk
    sub = bk // LANE
    _fill_head_weights(wb_scr, g2_ref, bq)
    pk_scr[...] = jnp.zeros(pk_scr.shape, BF16)
    pv_scr[...] = jnp.zeros(pv_scr.shape, BF16)
    plo_scr[...] = jnp.zeros(plo_scr.shape, BF16)
    phi_scr[...] = jnp.zeros(phi_scr.shape, BF16)
    pk_scr[0:bq, :] = nk_ref[...]
    pv_scr[0:bq, :] = nv_ref[...]
    plo_scr[0:bq, :] = nilo_ref[...]
    phi_scr[0:bq, :] = nihi_ref[...]

    for kt in range(n_cache):
        s = _index_scores(iq_ref, wb_scr, cilo_ref[0, kt * bk:(kt + 1) * bk, :],
                          cihi_ref[0, kt * bk:(kt + 1) * bk, :], bk)
        for c in range(sub):
            sc_scr[kt * sub + c] = s[:, c * LANE:(c + 1) * LANE]
    s = _index_scores(iq_ref, wb_scr, plo_scr[...], phi_scr[...], LANE)
    real = lax.broadcasted_iota(I32, (bq, LANE), 1) < bq
    sc_scr[n_cache * sub] = jnp.where(real, s, NEG_INF)

    n_sub = n_cache * sub + 1
    count = lambda cand, strict: _count_rows(lambda i: sc_scr[i], n_sub, cand, strict, bq, LANE)
    t, n_ge, n_gt = _kth_largest(count, n_sub * LANE, topk, (bq, 1))
    tie_rows = jnp.where((n_ge > topk) & (t > NEG_INF), 1, 0)
    any_tie = jnp.max(tie_rows) > 0

    def attend(ties):
        _init_softmax(m_scr, l_scr, acc_scr)
        eq_before = jnp.zeros((bq, 1), F32)
        for kt in range(n_cache):
            keys = jnp.concatenate([sc_scr[kt * sub + c] for c in range(sub)], axis=1)
            upper = upper_ref[...] if ties else None
            sel, eq_before = _select(keys, t, n_gt, eq_before, topk, upper, ties)
            kv_tile = lambda h, kt=kt: (
                ck_ref[0, pl.ds(kt * bk * DSA_HEADS + h, bk, stride=DSA_HEADS), :].astype(BF16),
                cv_ref[0, pl.ds(kt * bk * DSA_HEADS + h, bk, stride=DSA_HEADS), :].astype(BF16))
            _attend_tile(sel, q_ref, kv_tile, m_scr, l_scr, acc_scr)
        upper = upper_ref[0:LANE, 0:LANE] if ties else None
        sel, eq_before = _select(sc_scr[n_cache * sub], t, n_gt, eq_before, topk, upper, ties)
        kv_tile = lambda h: (pk_scr[:, h * DSA_DH:(h + 1) * DSA_DH],
                             pv_scr[:, h * DSA_DH:(h + 1) * DSA_DH])
        _attend_tile(sel, q_ref, kv_tile, m_scr, l_scr, acc_scr)
        _finish_softmax(o_ref, l_scr, acc_scr)

    @pl.when(jnp.logical_not(any_tie))
    def _():
        attend(False)

    @pl.when(any_tie)
    def _():
        attend(True)


def _dsa_sample(q, iq, g2, cache_k, cache_v, cache_ilo, cache_ihi, kb, vb, iklo, ikhi, bq, bk):
    n_streams, past = cache_k.shape[0:2]
    cache_k = cache_k.reshape(n_streams, past * DSA_HEADS, DSA_DH)
    cache_v = cache_v.reshape(n_streams, past * DSA_HEADS, DSA_DH)
    assert past % CHUNK == 0 and bq <= CHUNK and past % bk == 0
    topk = min(IDX_TOPK_MAX, (past + bq) // 4)
    upper = _upper_ones(bk)
    kern = functools.partial(_dsa_sample_kernel, bq=bq, bk=bk, past=past, topk=topk)
    row = lambda w: pl.BlockSpec((bq, w), lambda s: (s, 0))
    cache = lambda w: pl.BlockSpec((1, past, w), lambda s: (s, 0, 0))
    cache_kv = pl.BlockSpec((1, past * DSA_HEADS, DSA_DH), lambda s: (s, 0, 0))
    n_sub = past // LANE + 1
    return pl.pallas_call(
        kern,
        grid=(n_streams,),
        in_specs=[row(DSA_WIDTH), row(IDX_HEADS * IDX_DIM), row(LANE),
                  cache_kv, cache_kv, cache(LANE), cache(LANE),
                  row(DSA_WIDTH), row(DSA_WIDTH), row(LANE), row(LANE),
                  pl.BlockSpec(upper.shape, lambda s: (0, 0))],
        out_specs=row(DSA_WIDTH),
        out_shape=jax.ShapeDtypeStruct((n_streams * bq, DSA_WIDTH), BF16),
        scratch_shapes=[
            pltpu.VMEM((n_sub, bq, LANE), F32),
            pltpu.VMEM((IDX_HEADS, bq, LANE), F32),
            pltpu.VMEM((DSA_HEADS, bq, 1), F32),
            pltpu.VMEM((DSA_HEADS, bq, 1), F32),
            pltpu.VMEM((bq, DSA_WIDTH), F32),
            pltpu.VMEM((LANE, DSA_WIDTH), BF16),
            pltpu.VMEM((LANE, DSA_WIDTH), BF16),
            pltpu.VMEM((LANE, LANE), BF16),
            pltpu.VMEM((LANE, LANE), BF16),
        ],
        compiler_params=_cparams(("parallel",)),
        name="dsa_sample",
    )(q, iq, g2, cache_k, cache_v, cache_ilo, cache_ihi, kb, vb, iklo, ikhi, upper)


def _mix_kernel(og_ref, gg_ref, od_ref, x_ref, gnw_ref, wo_ref, nfw_ref, rw_ref, rb_ref,
                h1_ref, xn_ref, re_ref, rg_ref, cnt_ref):
    g = gg_ref[...].astype(F32)
    gate = g / (1.0 + jnp.exp(-g))
    acc = _dot(od_ref[...], wo_ref[GLA_WIDTH:GLA_WIDTH + DSA_WIDTH, :])
    for h in range(GLA_HEADS):
        hs = slice(h * GLA_DV, (h + 1) * GLA_DV)
        oh = og_ref[:, hs]
        ms = jnp.mean(oh * oh, axis=-1, keepdims=True)
        a = oh * lax.rsqrt(ms + EPS) * gnw_ref[...] * gate[:, hs]
        acc = acc + _dot(a.astype(BF16), wo_ref[hs, :])
    h1 = x_ref[...] + acc
    h1_ref[...] = h1
    ms = jnp.mean(h1 * h1, axis=-1, keepdims=True)
    xn = h1 * lax.rsqrt(ms + EPS) * nfw_ref[...]
    _rows_to_tiles(xn_ref, xn)
    xn_hi = xn.astype(BF16)
    xn_lo = (xn - xn_hi.astype(F32)).astype(BF16)
    logits = (_dot(xn_hi, rw_ref[0]) + _dot(xn_hi, rw_ref[1]) + _dot(xn_lo, rw_ref[0])
              + rb_ref[...])
    lane = lax.broadcasted_iota(I32, logits.shape, 1)
    lane_f = lane.astype(F32)
    work = logits
    e_out = jnp.zeros(logits.shape, I32)
    tops, hots = [], []
    for k in range(TOP_K):
        mx = jnp.max(work, axis=1, keepdims=True)
        idx = jnp.min(jnp.where(work == mx, lane_f, float(LANE)), axis=1, keepdims=True)
        idx = idx.astype(I32)
        hot = lane == idx
        e_out = jnp.where(lane == k, idx, e_out)
        tops.append(mx)
        hots.append(hot)
        work = jnp.where(hot, NEG_BIG, work)
    ex = [jnp.exp(v - tops[0]) for v in tops]
    den = ex[0] + ex[1] + ex[2] + ex[3]
    g_out = jnp.zeros(logits.shape, F32)
    for k in range(TOP_K):
        g_out = jnp.where(lane == k, ex[k] / den, g_out)
    rg_ref[...] = g_out
    bm = logits.shape[0]
    earlier = (lax.broadcasted_iota(I32, (bm, bm), 1) < lax.broadcasted_iota(I32, (bm, bm), 0))
    earlier = jnp.where(earlier, 1.0, 0.0).astype(BF16)
    run = jnp.zeros((1, LANE), F32)
    for k in range(TOP_K):
        hot_f = jnp.where(hots[k], 1.0, 0.0)
        before = _dot(earlier, hot_f.astype(BF16)) + run
        rank = jnp.sum(jnp.where(hots[k], before, 0.0), axis=1, keepdims=True).astype(I32)
        e_out = jnp.where(lane == TOP_K + k, rank, e_out)
        run = run + jnp.sum(hot_f, axis=0, keepdims=True)
    re_ref[...] = e_out
    cnt_ref[0] = jnp.broadcast_to(run, (SUBLANE, LANE)).astype(I32)


def _mix_consts(gla_norm_w, w_out, norm_ffn_w, router_w, router_b):
    rw = jnp.zeros((D_MODEL, LANE), F32).at[:, 0:N_EXPERTS].set(router_w)
    rw_hi = rw.astype(BF16)
    rw_lo = (rw - rw_hi.astype(F32)).astype(BF16)
    rb = jnp.full((1, LANE), NEG_BIG, F32).at[0, 0:N_EXPERTS].set(router_b)
    return (gla_norm_w.reshape(1, GLA_DV), w_out.astype(BF16), norm_ffn_w.reshape(1, D_MODEL),
            jnp.stack([rw_hi, rw_lo]), rb)


def _mix(og, gla, od, x2d, consts, bm):
    n = x2d.shape[0]
    gnw, wo, nfw, rws, rb = consts
    row = lambda w: pl.BlockSpec((bm, w), lambda i: (i, 0))
    const = lambda a: pl.BlockSpec(a.shape, lambda i: (0,) * a.ndim)
    gg_col = (2 * GLA_QK + GLA_WIDTH) // GLA_WIDTH
    return pl.pallas_call(
        _mix_kernel,
        grid=(n // bm,),
        in_specs=[row(GLA_WIDTH), pl.BlockSpec((bm, GLA_WIDTH), lambda i: (i, gg_col)),
                  row(DSA_WIDTH), row(D_MODEL),
                  const(gnw), const(wo), const(nfw), const(rws), const(rb)],
        out_specs=[row(D_MODEL), pl.BlockSpec((bm * ROW_TILE, LANE), lambda i: (i, 0)),
                   row(LANE), row(LANE), pl.BlockSpec((1, SUBLANE, LANE), lambda i: (i, 0, 0))],
        out_shape=[jax.ShapeDtypeStruct((n, D_MODEL), F32),
                   jax.ShapeDtypeStruct((n * ROW_TILE, LANE), F32),
                   jax.ShapeDtypeStruct((n, LANE), I32), jax.ShapeDtypeStruct((n, LANE), F32),
                   jax.ShapeDtypeStruct((n // bm, SUBLANE, LANE), I32)],
        compiler_params=_cparams(("parallel",)),
        name="mix",
    )(og, gla, od, x2d, gnw, wo, nfw, rws, rb)


MOE_ROWS = 256
DMA_UNROLL = 8
TOKEN_BLOCK = 512


def _route_tables(counts, br, n_pairs):
    c = counts[:, 0, :]
    tot = jnp.sum(c, axis=0)
    padded = (tot + br - 1) // br * br
    pad_end = jnp.cumsum(padded)
    pad_start = pad_end - padded
    base = pad_start[None, :] + jnp.cumsum(c, axis=0) - c
    base = jnp.broadcast_to(base[:, None, :], counts.shape).astype(I32)
    nb = -(-(n_pairs + N_EXPERTS * (br - 1)) // br)
    n_used = (pad_end[N_EXPERTS - 1] // br).astype(I32)
    blk = jnp.arange(nb, dtype=I32)
    last = jnp.minimum(blk, n_used - 1) * br
    block_e = jnp.sum((pad_end[None, 0:N_EXPERTS] <= last[:, None]).astype(I32), axis=1)
    ends = pad_end[0:N_EXPERTS].astype(I32)
    tots = jnp.concatenate([tot[0:N_EXPERTS].astype(I32), n_used.reshape(1)])
    return base, jnp.minimum(block_e, N_EXPERTS - 1), n_used.reshape(1), ends, tots


def _pos_kernel(re_ref, base_ref, pos_ref):
    re = re_ref[...]
    lane = lax.broadcasted_iota(I32, re.shape, 1)
    base_row = base_ref[0, 0:1, :].astype(F32)
    p = jnp.zeros(re.shape, F32)
    for k in range(TOP_K):
        hot = lane == re[:, k:k + 1]
        first = jnp.sum(jnp.where(hot, base_row, 0.0), axis=1, keepdims=True)
        p = jnp.where(lane == k, first + re[:, TOP_K + k:TOP_K + k + 1].astype(F32), p)
    pos_ref[0] = jnp.transpose(p)[0:SUBLANE, :].astype(I32) * ROW_TILE


def _positions(re, base, bm):
    n = re.shape[0]
    return pl.pallas_call(
        _pos_kernel,
        grid=(n // bm,),
        in_specs=[pl.BlockSpec((bm, LANE), lambda i: (i, 0)),
                  pl.BlockSpec((1, SUBLANE, LANE), lambda i: (i, 0, 0))],
        out_specs=pl.BlockSpec((1, SUBLANE, bm), lambda i: (i, 0, 0)),
        out_shape=jax.ShapeDtypeStruct((n // bm, SUBLANE, bm), I32),
        compiler_params=_cparams(("parallel",)),
        name="positions",
    )(re, base)


def _dispatch_kernel(end_ref, tot_ref, pos_ref, xa_ref, xb_ref, xr_hbm, zero_scr, sem,
                     *, bt, br, nsteps_a):
    i = pl.program_id(0)

    @pl.when(i == 0)
    def _():
        zero_scr[...] = jnp.zeros(zero_scr.shape, F32)
        n_used = tot_ref[N_EXPERTS]
        nb = xr_hbm.shape[0] // (br * ROW_TILE)

        def fills():
            for e in range(N_EXPERTS):
                yield tot_ref[e] > 0, end_ref[e] - br
            for j in range(N_EXPERTS):
                yield n_used + j < nb, jnp.minimum(n_used + j, nb - 1) * br

        for act in ("start", "wait"):
            for cond, row0 in fills():
                @pl.when(cond)
                def _(row0=row0, act=act):
                    dst = xr_hbm.at[pl.ds(pl.multiple_of(row0 * ROW_TILE, ROW_TILE), br * ROW_TILE)]
                    c = pltpu.make_async_copy(zero_scr, dst, sem.at[1])
                    c.start() if act == "start" else c.wait()

    def scatter(x_ref):
        for k in range(TOP_K):
            def body(r, c, k=k):
                src = x_ref.at[pl.ds(pl.multiple_of(r * ROW_TILE, ROW_TILE), ROW_TILE)]
                dst = xr_hbm.at[pl.ds(pl.multiple_of(pos_ref[0, k, r], ROW_TILE), ROW_TILE)]
                pltpu.make_async_copy(src, dst, sem.at[0]).start()
                return c
            lax.fori_loop(0, bt, body, 0, unroll=DMA_UNROLL)

    @pl.when(i < nsteps_a)
    def _():
        scatter(xa_ref)

    @pl.when(i >= nsteps_a)
    def _():
        scatter(xb_ref)

    done = xr_hbm.at[pl.ds(0, TOP_K * bt * ROW_TILE)]
    pltpu.make_async_copy(done, done, sem.at[0]).wait()


def _dispatch(xn_a, xn_b, pos_t, pad_end, tot, n_rows, bt, br):
    na, nb_ = xn_a.shape[0] // (bt * ROW_TILE), xn_b.shape[0] // (bt * ROW_TILE)
    kern = functools.partial(_dispatch_kernel, bt=bt, br=br, nsteps_a=na)
    grid_spec = pltpu.PrefetchScalarGridSpec(
        num_scalar_prefetch=2,
        grid=(na + nb_,),
        in_specs=[
            pl.BlockSpec((1, SUBLANE, bt), lambda i, e, t: (i, 0, 0), memory_space=pltpu.SMEM),
            pl.BlockSpec((bt * ROW_TILE, LANE), lambda i, e, t: (jnp.minimum(i, na - 1), 0)),
            pl.BlockSpec((bt * ROW_TILE, LANE), lambda i, e, t: (jnp.maximum(i - na, 0), 0)),
        ],
        out_specs=pl.BlockSpec(memory_space=pl.ANY),
        scratch_shapes=[pltpu.VMEM((br * ROW_TILE, LANE), F32), pltpu.SemaphoreType.DMA((2,))],
    )
    return pl.pallas_call(
        kern,
        grid_spec=grid_spec,
        out_shape=jax.ShapeDtypeStruct((n_rows * ROW_TILE, LANE), F32),
        compiler_params=_cparams(("arbitrary",)),
        name="dispatch",
    )(pad_end, tot, pos_t, xn_a, xn_b)


def _rows_to_tiles(ref, x):
    n = x.shape[0]
    for j in range(ROW_TILE):
        ref[pl.ds(j, n, stride=ROW_TILE), :] = x[:, j * LANE:(j + 1) * LANE]


def _tile_chunk(ref, j, n, row0=0):
    return ref[pl.ds(row0 * ROW_TILE + j, n, stride=ROW_TILE), :]


def _rows_from_tiles(ref, n):
    return jnp.concatenate([_tile_chunk(ref, j, n) for j in range(ROW_TILE)], axis=1)


def _moe_kernel(be_ref, nu_ref, x_ref, wgu_ref, bgu_ref, wd_ref, bdn_ref,
                y_ref, wgu_bf, wd_bf):
    i = pl.program_id(0)
    n_used = nu_ref[0]

    @pl.when(i < n_used)
    def _():
        e = be_ref[i]
        prev = be_ref[jnp.maximum(i - 1, 0)]

        @pl.when((i == 0) | (e != prev))
        def _():
            wgu_bf[...] = wgu_ref[0].astype(BF16)
            wd_bf[...] = wd_ref[0].astype(BF16)

        xb = _rows_from_tiles(x_ref, x_ref.shape[0] // ROW_TILE).astype(BF16)
        gu = _dot(xb, wgu_bf[...]) + bgu_ref[0]
        g_lin = jnp.minimum(gu[:, 0:D_FF], SWIGLU_LIMIT)
        u_lin = jnp.clip(gu[:, D_FF:2 * D_FF], -SWIGLU_LIMIT, SWIGLU_LIMIT)
        act = g_lin / (1.0 + jnp.exp(-SWIGLU_ALPHA * g_lin)) * (u_lin + 1.0)
        y = _dot(act.astype(BF16), wd_bf[...]) + bdn_ref[0]
        _rows_to_tiles(y_ref, y)

    @pl.when(i >= n_used)
    def _():
        y_ref[...] = jnp.zeros(y_ref.shape, y_ref.dtype)


def _moe(x_rows, block_e, n_used, w_gu, b_gu, w_down, b_down, br):
    nb = x_rows.shape[0] // (br * ROW_TILE)
    grid_spec = pltpu.PrefetchScalarGridSpec(
        num_scalar_prefetch=2,
        grid=(nb,),
        in_specs=[
            pl.BlockSpec((br * ROW_TILE, LANE),
                         lambda i, be, nu: (jnp.minimum(i, nu[0] - 1), 0)),
            pl.BlockSpec((1, D_MODEL, 2 * D_FF), lambda i, be, nu: (be[i], 0, 0)),
            pl.BlockSpec((1, 1, 2 * D_FF), lambda i, be, nu: (be[i], 0, 0)),
            pl.BlockSpec((1, D_FF, D_MODEL), lambda i, be, nu: (be[i], 0, 0)),
            pl.BlockSpec((1, 1, D_MODEL), lambda i, be, nu: (be[i], 0, 0)),
        ],
        out_specs=pl.BlockSpec((br * ROW_TILE, LANE), lambda i, be, nu: (i, 0)),
        scratch_shapes=[
            pltpu.VMEM((D_MODEL, 2 * D_FF), BF16),
            pltpu.VMEM((D_FF, D_MODEL), BF16),
        ],
    )
    return pl.pallas_call(
        _moe_kernel,
        grid_spec=grid_spec,
        out_shape=jax.ShapeDtypeStruct((nb * br * ROW_TILE, LANE), F32),
        compiler_params=_cparams(("arbitrary",)),
        name="moe",
    )(block_e, n_used, x_rows, w_gu, b_gu.reshape(N_EXPERTS, 1, 2 * D_FF),
      w_down, b_down.reshape(N_EXPERTS, 1, D_MODEL))


def _combine_kernel(pos_ref, posn_ref, y_hbm, h1_ref, rg_ref, fw_ref, o_ref, yg, sem, *, bt):
    i = pl.program_id(0)
    nsteps = pl.num_programs(0)
    slot = lax.rem(i, 2)

    def issue(pref, s):
        for k in range(TOP_K):
            def body(r, c, k=k):
                src = y_hbm.at[pl.ds(pl.multiple_of(pref[0, k, r], ROW_TILE), ROW_TILE)]
                row0 = pl.multiple_of((k * bt + r) * ROW_TILE, ROW_TILE)
                pltpu.make_async_copy(src, yg.at[s, pl.ds(row0, ROW_TILE)], sem.at[s]).start()
                return c
            lax.fori_loop(0, bt, body, 0, unroll=DMA_UNROLL)

    @pl.when(i == 0)
    def _():
        issue(pos_ref, 0)

    @pl.when(i + 1 < nsteps)
    def _():
        issue(posn_ref, 1 - slot)

    pltpu.make_async_copy(yg.at[slot], yg.at[slot], sem.at[slot]).wait()
    gates = [jnp.broadcast_to(rg_ref[:, k:k + 1], (bt, LANE)) for k in range(TOP_K)]
    ss = jnp.zeros((bt, LANE), F32)
    rows = yg.at[slot]
    for j in range(ROW_TILE):
        js = slice(j * LANE, (j + 1) * LANE)
        a = h1_ref[:, js]
        for k in range(TOP_K):
            a = a + gates[k] * _tile_chunk(rows, j, bt, k * bt)
        ss = ss + a * a
        o_ref[:, js] = a
    ms = jnp.sum(ss, axis=-1, keepdims=True) * (1.0 / D_MODEL)
    o_ref[...] = o_ref[...] * lax.rsqrt(ms + EPS) * fw_ref[...]


def _combine(y_rows, pos_t, h1, rg, final_w, bt):
    n = h1.shape[0]
    nsteps = n // bt
    kern = functools.partial(_combine_kernel, bt=bt)
    row = lambda w: pl.BlockSpec((bt, w), lambda i: (i, 0))
    return pl.pallas_call(
        kern,
        grid=(nsteps,),
        in_specs=[
            pl.BlockSpec((1, SUBLANE, bt), lambda i: (i, 0, 0), memory_space=pltpu.SMEM),
            pl.BlockSpec((1, SUBLANE, bt), lambda i: (jnp.minimum(i + 1, nsteps - 1), 0, 0),
                         memory_space=pltpu.SMEM),
            pl.BlockSpec(memory_space=pl.ANY),
            row(D_MODEL), row(LANE),
            pl.BlockSpec((1, D_MODEL), lambda i: (0, 0)),
        ],
        out_specs=row(D_MODEL),
        out_shape=jax.ShapeDtypeStruct((n, D_MODEL), F32),
        scratch_shapes=[pltpu.VMEM((2, TOP_K * bt * ROW_TILE, LANE), F32),
                        pltpu.SemaphoreType.DMA((2,))],
        compiler_params=_cparams(("arbitrary",)),
        name="combine",
    )(pos_t, pos_t, y_rows, h1, rg, final_w.reshape(1, D_MODEL))


def kernel(x_prompt, x_sample, cache_k, cache_v, cache_k_idx, state_gla, norm_mix_w, w_in,
           w_gla_a2, b_gla_a2, gla_norm_w, w_out, norm_ffn_w, router_w, router_b, w_gu, b_gu,
           w_down, b_down, norm_final_w):
    b_p, t_p, _ = x_prompt.shape
    b_s, t_s, _ = x_sample.shape
    past = cache_k.shape[2]
    assert b_p == 1 and norm_mix_w.shape[0] == 1
    l = 0
    pw = _proj_weights(norm_mix_w[l], w_in[l])
    gconsts = _gla_consts(w_gla_a2[l], b_gla_a2[l])

    pp = _project_t(x_prompt.reshape(t_p, D_MODEL), _proj_t_weights(norm_mix_w[l], w_in[l]), DSA_BK)
    gla_p, kf_p, vf_p, kb_p, ikf_p, ikb_p, g2_p, qt_p, iqt_p, vt3_p, g2t_p = pp
    s0 = jnp.zeros((1, GLA_WIDTH, GLA_QK), F32)
    og_p, st_p = _gla(gla_p, g2_p, s0, gconsts, 1, t_p, 512, 16)
    od_p = _dsa_prompt_t(qt_p, iqt_p, g2t_p, ikb_p, kb_p, vt3_p, DSA_BQ, DSA_BK)

    ps = _project(x_sample.reshape(b_s * t_s, D_MODEL), pw, 512)
    gla_s, q_s, kf_s, vf_s, kb_s, vb_s, iq_s, iklo_s, ikhi_s, ikf_s, g2_s = ps
    og_s, st_s = _gla(gla_s, g2_s, _state_to_kernel(state_gla[l]), gconsts, b_s, t_s, t_s, 16)
    zc = jnp.zeros((b_s, past, IDX_DIM), BF16)
    cik = cache_k_idx[l].astype(BF16)
    cilo = jnp.concatenate([cik, zc], axis=2)
    cihi = jnp.concatenate([zc, cik], axis=2)
    od_s = _dsa_sample(q_s, iq_s, g2_s, cache_k[l], cache_v[l], cilo, cihi,
                       kb_s, vb_s, iklo_s, ikhi_s, t_s, 512)

    mconsts = _mix_consts(gla_norm_w[l], w_out[l], norm_ffn_w[l], router_w[l], router_b[l])
    n_s = b_s * t_s
    tb = TOKEN_BLOCK
    h1_p, xn_p, re_p, rg_p, cnt_p = _mix(og_p, gla_p, od_p, x_prompt.reshape(t_p, D_MODEL), mconsts, tb)
    h1_s, xn_s, re_s, rg_s, cnt_s = _mix(og_s, gla_s, od_s, x_sample.reshape(n_s, D_MODEL), mconsts, tb)
    n_pairs = (t_p + n_s) * TOP_K
    base, block_e, n_used, ends, tots = _route_tables(
        jnp.concatenate([cnt_p, cnt_s], axis=0), MOE_ROWS, n_pairs)
    n_rows = block_e.shape[0] * MOE_ROWS
    pos_p = _positions(re_p, base[0:t_p // tb], tb)
    pos_s = _positions(re_s, base[t_p // tb:], tb)
    x_rows = _dispatch(xn_p, xn_s, jnp.concatenate([pos_p, pos_s], axis=0), ends, tots,
                       n_rows, tb, MOE_ROWS)
    y_rows = _moe(x_rows, block_e, n_used, w_gu[l], b_gu[l], w_down[l], b_down[l], MOE_ROWS)
    y_p = _combine(y_rows, pos_p, h1_p, rg_p, norm_final_w, tb)
    y_s = _combine(y_rows, pos_s, h1_s, rg_s, norm_final_w, tb)
    y_p = y_p.reshape(x_prompt.shape)
    y_s = y_s.reshape(x_sample.shape)
    return (y_p, y_s,
            kf_p.reshape(1, 1, t_p, DSA_HEADS, DSA_DH), vf_p.reshape(1, 1, t_p, DSA_HEADS, DSA_DH),
            ikf_p.reshape(1, 1, t_p, IDX_DIM), _state_from_kernel(st_p)[None],
            kf_s.reshape(1, b_s, t_s, DSA_HEADS, DSA_DH), vf_s.reshape(1, b_s, t_s, DSA_HEADS, DSA_DH),
            ikf_s.reshape(1, b_s, t_s, IDX_DIM), _state_from_kernel(st_s)[None])
```

```python
import functools

import jax
import jax.numpy as jnp
from jax import lax
from jax.experimental import pallas as pl
from jax.experimental.pallas import tpu as pltpu

F32 = jnp.float32
BF16 = jnp.bfloat16
I32 = jnp.int32

D_MODEL = 1024
CHUNK = 64
GLA_HEADS = 4
GLA_DK = 64
GLA_DV = 128
GLA_QK = GLA_HEADS * GLA_DK
GLA_WIDTH = GLA_HEADS * GLA_DV
GLA_GATE_RANK = 16
GLA_GATE_TAU = 16.0
DSA_HEADS = 4
DSA_DH = 128
DSA_WIDTH = DSA_HEADS * DSA_DH
IDX_HEADS = 8
IDX_DIM = 64
IDX_TOPK_MAX = 256
N_EXPERTS = 32
TOP_K = 4
D_FF = 1024
SWIGLU_ALPHA = 1.702
SWIGLU_LIMIT = 7.0
EPS = 1e-6

_OFF_GLA = 0
_W_GLA = 2 * GLA_QK + 2 * GLA_WIDTH
_OFF_GA = _OFF_GLA + _W_GLA
_OFF_DSA = _OFF_GA + GLA_GATE_RANK
_W_DSA = 3 * DSA_WIDTH + IDX_HEADS * IDX_DIM
_OFF_IK = _OFF_DSA + _W_DSA
_OFF_IW = _OFF_IK + IDX_DIM
LANE = 128
SUBLANE = 8
ROW_TILE = D_MODEL // LANE
assert ROW_TILE == SUBLANE
LOG2E = 1.4426950408889634
NEG_BIG = -1e30
VMEM_LIMIT = 56 * 1024 * 1024


def _cparams(sem):
    return pltpu.CompilerParams(dimension_semantics=sem, vmem_limit_bytes=VMEM_LIMIT)


def _dot(a, b):
    return jnp.dot(a, b, preferred_element_type=F32)


def _dot_nt(a, b):
    return lax.dot_general(a, b, (((1,), (1,)), ((), ())), preferred_element_type=F32)


def _dot_tn(a, b):
    return lax.dot_general(a, b, (((0,), (0,)), ((), ())), preferred_element_type=F32)


def _split3(x):
    hi = x.astype(BF16)
    r1 = x - hi.astype(F32)
    mid = r1.astype(BF16)
    lo = (r1 - mid.astype(F32)).astype(BF16)
    return hi, mid, lo


def _proj_kernel(x_ref, nw_ref, wg_ref, wd_ref, ws_ref,
                 gla_ref, q_ref, kf_ref, vf_ref, kb_ref, vb_ref, iq_ref,
                 iklo_ref, ikhi_ref, ikf_ref, g2_ref):
    x = x_ref[...]
    ms = jnp.mean(x * x, axis=-1, keepdims=True)
    xn = (x * lax.rsqrt(ms + EPS) * nw_ref[...]).astype(BF16)
    gla_ref[...] = _dot(xn, wg_ref[...]).astype(BF16)
    W = DSA_WIDTH
    dq = _dot(xn, wd_ref[:, 0:W])
    q_ref[...] = (dq * (DSA_DH ** -0.5 * LOG2E)).astype(BF16)
    dk = _dot(xn, wd_ref[:, W:2 * W])
    kf_ref[...] = dk
    kb_ref[...] = dk.astype(BF16)
    dv = _dot(xn, wd_ref[:, 2 * W:3 * W])
    vf_ref[...] = dv
    vb_ref[...] = dv.astype(BF16)
    iq_ref[...] = _dot(xn, wd_ref[:, 3 * W:4 * W]).astype(BF16)
    sm = _dot(xn, ws_ref[...])
    iklo_ref[...] = sm[:, 0:LANE].astype(BF16)
    ikhi_ref[...] = sm[:, LANE:2 * LANE].astype(BF16)
    ikf_ref[...] = sm[:, 0:IDX_DIM]
    g2_ref[...] = sm[:, 2 * LANE:3 * LANE]


def _proj_weights(norm_w, w_in):
    wg = w_in[:, _OFF_GLA:_OFF_GLA + _W_GLA].astype(BF16)
    wd = w_in[:, _OFF_DSA:_OFF_DSA + _W_DSA].astype(BF16)
    ik = w_in[:, _OFF_IK:_OFF_IK + IDX_DIM]
    z64 = jnp.zeros((D_MODEL, IDX_DIM), F32)
    ga = w_in[:, _OFF_GA:_OFF_GA + GLA_GATE_RANK]
    iw = w_in[:, _OFF_IW:_OFF_IW + IDX_HEADS]
    zpad = jnp.zeros((D_MODEL, LANE - GLA_GATE_RANK - IDX_HEADS), F32)
    ws = jnp.concatenate([ik, z64, z64, ik, ga, iw, zpad], axis=1).astype(BF16)
    return norm_w.reshape(1, D_MODEL), wg, wd, ws


def _project(x2d, pw, bm):
    n = x2d.shape[0]
    nw, wg, wd, ws = pw
    row = lambda w: pl.BlockSpec((bm, w), lambda i: (i, 0))
    full = lambda a: pl.BlockSpec(a.shape, lambda i: (0, 0))
    outs = [(_W_GLA, BF16), (DSA_WIDTH, BF16), (DSA_WIDTH, F32), (DSA_WIDTH, F32),
            (DSA_WIDTH, BF16), (DSA_WIDTH, BF16), (IDX_HEADS * IDX_DIM, BF16),
            (LANE, BF16), (LANE, BF16), (IDX_DIM, F32), (LANE, F32)]
    return pl.pallas_call(
        _proj_kernel,
        grid=(n // bm,),
        in_specs=[row(D_MODEL), full(nw), full(wg), full(wd), full(ws)],
        out_specs=[row(w) for w, _ in outs],
        out_shape=[jax.ShapeDtypeStruct((n, w), dt) for w, dt in outs],
        compiler_params=_cparams(("parallel",)),
        name="proj",
    )(x2d, nw, wg, wd, ws)


def _proj_t_kernel(x_ref, nw_ref, wg_ref, wkv_ref, ws_ref, wt_ref, wst_ref,
                   gla_ref, kf_ref, vf_ref, kb_ref, ikf_ref, ikb_ref, g2_ref,
                   qt_ref, iqt_ref, vt_ref, g2t_ref):
    x = x_ref[...]
    ms = jnp.mean(x * x, axis=-1, keepdims=True)
    xn = (x * lax.rsqrt(ms + EPS) * nw_ref[...]).astype(BF16)
    gla_ref[...] = _dot(xn, wg_ref[...]).astype(BF16)
    W = DSA_WIDTH
    dk = _dot(xn, wkv_ref[:, 0:W])
    dv = _dot(xn, wkv_ref[:, W:2 * W])
    kb_ref[...] = dk.astype(BF16)
    for h in range(DSA_HEADS):
        kf_ref[:, h, :] = dk[:, h * DSA_DH:(h + 1) * DSA_DH]
        vf_ref[:, h, :] = dv[:, h * DSA_DH:(h + 1) * DSA_DH]
    sm = _dot(xn, ws_ref[...])
    ikb_ref[...] = sm[:, 0:LANE].astype(BF16)
    ikf_ref[...] = sm[:, 0:IDX_DIM]
    g2_ref[...] = sm[:, LANE:2 * LANE]
    qt_ref[...] = (_dot_nt(wt_ref[0:W, :], xn) * (DSA_DH ** -0.5 * LOG2E)).astype(BF16)
    vt_ref[0] = _dot_nt(wt_ref[W:2 * W, :], xn).astype(BF16)
    iqt_ref[...] = _dot_nt(wt_ref[2 * W:3 * W, :], xn).astype(BF16)
    g2t_ref[...] = _dot_nt(wst_ref[...], xn)


def _proj_t_weights(norm_w, w_in):
    wg = w_in[:, _OFF_GLA:_OFF_GLA + _W_GLA].astype(BF16)
    W = DSA_WIDTH
    wq = w_in[:, _OFF_DSA:_OFF_DSA + W]
    wkv = w_in[:, _OFF_DSA + W:_OFF_DSA + 3 * W]
    wv = w_in[:, _OFF_DSA + 2 * W:_OFF_DSA + 3 * W]
    wiq = w_in[:, _OFF_DSA + 3 * W:_OFF_DSA + 4 * W]
    ik = w_in[:, _OFF_IK:_OFF_IK + IDX_DIM]
    z64 = jnp.zeros((D_MODEL, IDX_DIM), F32)
    ga = w_in[:, _OFF_GA:_OFF_GA + GLA_GATE_RANK]
    iw = w_in[:, _OFF_IW:_OFF_IW + IDX_HEADS]
    zpad = jnp.zeros((D_MODEL, LANE - GLA_GATE_RANK - IDX_HEADS), F32)
    g2w = jnp.concatenate([ga, iw, zpad], axis=1)
    ws = jnp.concatenate([ik, z64, g2w], axis=1).astype(BF16)
    wt = jnp.concatenate([wq, wv, wiq], axis=1).T.astype(BF16)
    return norm_w.reshape(1, D_MODEL), wg, wkv.astype(BF16), ws, wt, g2w.T.astype(BF16)


def _project_t(x2d, pw, bm):
    n = x2d.shape[0]
    row = lambda w: pl.BlockSpec((bm, w), lambda i: (i, 0))
    row3 = pl.BlockSpec((bm, DSA_HEADS, DSA_DH), lambda i: (i, 0, 0))
    col = lambda h: pl.BlockSpec((h, bm), lambda i: (0, i))
    full = lambda a: pl.BlockSpec(a.shape, lambda i: (0, 0))
    out_specs = [row(_W_GLA), row3, row3, row(DSA_WIDTH), row(IDX_DIM), row(LANE), row(LANE),
                 col(DSA_WIDTH), col(IDX_HEADS * IDX_DIM),
                 pl.BlockSpec((1, DSA_WIDTH, bm), lambda i: (i, 0, 0)), col(LANE)]
    out_shape = [jax.ShapeDtypeStruct((n, _W_GLA), BF16),
                 jax.ShapeDtypeStruct((n, DSA_HEADS, DSA_DH), F32),
                 jax.ShapeDtypeStruct((n, DSA_HEADS, DSA_DH), F32),
                 jax.ShapeDtypeStruct((n, DSA_WIDTH), BF16),
                 jax.ShapeDtypeStruct((n, IDX_DIM), F32),
                 jax.ShapeDtypeStruct((n, LANE), BF16),
                 jax.ShapeDtypeStruct((n, LANE), F32),
                 jax.ShapeDtypeStruct((DSA_WIDTH, n), BF16),
                 jax.ShapeDtypeStruct((IDX_HEADS * IDX_DIM, n), BF16),
                 jax.ShapeDtypeStruct((n // bm, DSA_WIDTH, bm), BF16),
                 jax.ShapeDtypeStruct((LANE, n), F32)]
    return pl.pallas_call(
        _proj_t_kernel,
        grid=(n // bm,),
        in_specs=[row(D_MODEL)] + [full(a) for a in pw],
        out_specs=out_specs,
        out_shape=out_shape,
        compiler_params=_cparams(("parallel",)),
        name="proj_t",
    )(x2d, *pw)


def _log_sigmoid(x):
    return jnp.minimum(x, 0.0) - jnp.log1p(jnp.exp(-jnp.abs(x)))


def _gla_kernel(gla_ref, g2_ref, w2_ref, b2_ref, seg_ref, bd_ref, s0_ref,
                o_ref, sT_ref,
                st_scr, kpad, bpad, vpad, qt_scr, kt_scr, dec_scr, oi_scr,
                *, bt, c):
    t = pl.program_id(1)

    @pl.when(t == 0)
    def _():
        st_scr[...] = s0_ref[0]

    q = gla_ref[:, 0:GLA_QK].astype(F32) * (GLA_DK ** -0.5)
    k = gla_ref[:, GLA_QK:2 * GLA_QK].astype(F32)
    vb = gla_ref[:, 2 * GLA_QK:2 * GLA_QK + GLA_WIDTH]
    v = vb.astype(F32)

    ga = g2_ref[...]
    ga_hi = ga.astype(BF16)
    ga_lo = (ga - ga_hi.astype(F32)).astype(BF16)
    logit = (_dot(ga_hi, w2_ref[0]) + _dot(ga_hi, w2_ref[1]) + _dot(ga_lo, w2_ref[0])
             + b2_ref[...])
    lg = _log_sigmoid(logit) * (1.0 / GLA_GATE_TAU)

    shift = c.bit_length() - 1
    row = lax.broadcasted_iota(I32, (bt, bt), 0)
    col = lax.broadcasted_iota(I32, (bt, bt), 1)
    same = (row >> shift) == (col >> shift)
    tri = jnp.where(same & (col <= row), 1.0, 0.0).astype(BF16)
    last = jnp.where(same & ((col & (c - 1)) == c - 1), 1.0, 0.0).astype(BF16)
    l0, l1, l2 = _split3(lg)
    b = _dot(tri, l0) + _dot(tri, l1) + _dot(tri, l2)
    b0, b1, b2s = _split3(b)
    bl = _dot(last, b0) + _dot(last, b1) + _dot(last, b2s)

    zpad = jnp.zeros((c, GLA_QK), F32)
    kpad[0:c, :] = zpad
    bpad[0:c, :] = zpad
    vpad[0:c, :] = jnp.zeros((c, GLA_WIDTH), F32)
    kpad[c:c + bt, :] = k
    bpad[c:c + bt, :] = b
    vpad[c:c + bt, :] = v
    pos = lax.broadcasted_iota(I32, (bt, GLA_QK), 0) & (c - 1)
    seg = seg_ref[...]
    o_intra = jnp.zeros((bt, GLA_WIDTH), F32)
    for d in range(c):
        ks = kpad[c - d:c - d + bt, :]
        bs = bpad[c - d:c - d + bt, :]
        vs = vpad[c - d:c - d + bt, :]
        z = q * ks * jnp.exp(jnp.minimum(b - bs, 0.0))
        z = jnp.where(pos >= d, z, 0.0)
        o_intra = o_intra + _dot(z.astype(BF16), seg) * vs
    oi_scr[...] = o_intra

    qt_scr[...] = q * jnp.exp(b)
    kt_scr[...] = k * jnp.exp(bl - b)
    dec_scr[...] = jnp.exp(bl)
    bd = bd_ref[...]

    def step(ci, carry):
        r0 = pl.multiple_of(ci * c, c)
        qc = qt_scr[pl.ds(r0, c), :].astype(BF16)
        kc = kt_scr[pl.ds(r0, c), :].astype(BF16)
        vc = vpad[pl.ds(r0 + c, c), :].astype(BF16)
        st = st_scr[...]
        o_ref[pl.ds(r0, c), :] = oi_scr[pl.ds(r0, c), :] + _dot_nt(qc, st.astype(BF16))
        dec = dec_scr[pl.ds(r0, 1), :]
        st_scr[...] = st * dec + _dot_tn(vc, kc) * bd
        return carry

    lax.fori_loop(0, bt // c, step, 0)

    @pl.when(t == pl.num_programs(1) - 1)
    def _():
        sT_ref[0] = st_scr[...]


def _gla_consts(w_gla_a2, b_gla_a2):
    w2 = jnp.zeros((LANE, GLA_QK), F32).at[0:GLA_GATE_RANK].set(w_gla_a2)
    w2_hi = w2.astype(BF16)
    w2_lo = (w2 - w2_hi.astype(F32)).astype(BF16)
    w2s = jnp.stack([w2_hi, w2_lo])
    hq = jnp.arange(GLA_QK) // GLA_DK
    hv = jnp.arange(GLA_WIDTH) // GLA_DV
    seg = (hq[:, None] == hv[None, :]).astype(BF16)
    bd = (hv[:, None] == hq[None, :]).astype(F32)
    return w2s, b_gla_a2.reshape(1, GLA_QK), seg, bd


def _state_to_kernel(s):
    n = s.shape[0]
    eye = jnp.eye(GLA_HEADS, dtype=s.dtype)
    t = jnp.swapaxes(s, 2, 3)[:, :, :, None, :] * eye[None, :, None, :, None]
    return t.reshape(n, GLA_WIDTH, GLA_QK)


def _state_from_kernel(t):
    n = t.shape[0]
    t5 = t.reshape(n, GLA_HEADS, GLA_DV, GLA_HEADS, GLA_DK)
    return jnp.einsum("shehd->shde", t5)


def _gla(gla, g2, s0, consts, n_streams, t_len, bt, c):
    w2s, b2, seg, bd = consts
    nb = t_len // bt
    kern = functools.partial(_gla_kernel, bt=bt, c=c)
    const = lambda a: pl.BlockSpec(a.shape, lambda s, t: (0,) * a.ndim)
    o, sT = pl.pallas_call(
        kern,
        grid=(n_streams, nb),
        in_specs=[
            pl.BlockSpec((bt, 2 * GLA_QK + GLA_WIDTH), lambda s, t: (s * nb + t, 0)),
            pl.BlockSpec((bt, LANE), lambda s, t: (s * nb + t, 0)),
            const(w2s), const(b2), const(seg), const(bd),
            pl.BlockSpec((1, GLA_WIDTH, GLA_QK), lambda s, t: (s, 0, 0)),
        ],
        out_specs=[
            pl.BlockSpec((bt, GLA_WIDTH), lambda s, t: (s * nb + t, 0)),
            pl.BlockSpec((1, GLA_WIDTH, GLA_QK), lambda s, t: (s, 0, 0)),
        ],
        out_shape=[
            jax.ShapeDtypeStruct((n_streams * t_len, GLA_WIDTH), F32),
            jax.ShapeDtypeStruct((n_streams, GLA_WIDTH, GLA_QK), F32),
        ],
        scratch_shapes=[
            pltpu.VMEM((GLA_WIDTH, GLA_QK), F32),
            pltpu.VMEM((bt + c, GLA_QK), F32),
            pltpu.VMEM((bt + c, GLA_QK), F32),
            pltpu.VMEM((bt + c, GLA_WIDTH), F32),
            pltpu.VMEM((bt, GLA_QK), F32),
            pltpu.VMEM((bt, GLA_QK), F32),
            pltpu.VMEM((bt, GLA_QK), F32),
            pltpu.VMEM((bt, GLA_WIDTH), F32),
        ],
        compiler_params=_cparams(("arbitrary", "arbitrary")),
        name="gla",
    )(gla, g2, w2s, b2, seg, bd, s0)
    return o, sT


INT_MIN = -2 ** 31
NEG_INF = float("-inf")
KEY_NEG_INF = -2139095041
F32_LOWEST = -3.4028234663852886e38
IDX_SCALE = IDX_DIM ** -0.5 * IDX_HEADS ** -0.5


def _fill_head_weights(wb_scr, g2_ref, bq):
    for h in range(IDX_HEADS):
        c0 = GLA_GATE_RANK + h
        col = g2_ref[:, c0:c0 + 1] * IDX_SCALE
        wb_scr[h] = jnp.broadcast_to(col, (bq, LANE))


def _index_scores(iq_ref, wb_scr, iklo, ikhi, bk):
    parts = [None] * (bk // LANE)
    for p in range(IDX_HEADS // 2):
        iqp = iq_ref[:, p * LANE:(p + 1) * LANE]
        y0 = jnp.maximum(_dot_nt(iqp, iklo), 0.0)
        y1 = jnp.maximum(_dot_nt(iqp, ikhi), 0.0)
        w0 = wb_scr[2 * p]
        w1 = wb_scr[2 * p + 1]
        for c in range(bk // LANE):
            t = y0[:, c * LANE:(c + 1) * LANE] * w0 + y1[:, c * LANE:(c + 1) * LANE] * w1
            parts[c] = t if parts[c] is None else parts[c] + t
    return jnp.concatenate(parts, axis=1)


def _key_to_float(key):
    key = jnp.maximum(key, KEY_NEG_INF)
    return pltpu.bitcast(key ^ ((key >> 31) & 0x7FFFFFFF), F32)


def _kth_largest(count, total, topk, shape):
    def bit_body(i, carry):
        prefix, n_ge = carry
        cand_u = prefix | jnp.left_shift(jnp.int32(1), 31 - i)
        cnt = count(_key_to_float(cand_u ^ INT_MIN), False)
        ok = cnt >= topk
        return jnp.where(ok, cand_u, prefix), jnp.where(ok, cnt, n_ge)

    init = (jnp.zeros(shape, I32), jnp.zeros(shape, I32) + total)
    prefix, n_ge = lax.fori_loop(0, 32, bit_body, init)
    t = _key_to_float(prefix ^ INT_MIN)
    return t, n_ge, count(t, True)


def _count_rows(load_tile, n_tiles, cand, strict, bq, bk):
    cand_b = jnp.broadcast_to(cand, (bq, LANE))

    def body(kt, acc):
        s = load_tile(kt)
        for c in range(bk // LANE):
            sc = s[:, c * LANE:(c + 1) * LANE]
            acc = acc + jnp.where(sc > cand_b if strict else sc >= cand_b, 1, 0)
        return acc

    acc = lax.fori_loop(0, n_tiles, body, jnp.zeros((bq, LANE), I32))
    return jnp.sum(acc, axis=1, keepdims=True)


def _attend_tile(sel, q_ref, kv_tile, m_scr, l_scr, acc_scr):
    for h in range(DSA_HEADS):
        hs = slice(h * DSA_DH, (h + 1) * DSA_DH)
        kh, vh = kv_tile(h)
        lg = jnp.where(sel, _dot_nt(q_ref[:, hs], kh), NEG_BIG)
        m_old = m_scr[h]
        m_new = jnp.maximum(m_old, jnp.max(lg, axis=1, keepdims=True))
        alpha = jnp.exp2(m_old - m_new)
        p = jnp.exp2(lg - m_new)
        l_scr[h] = alpha * l_scr[h] + jnp.sum(p, axis=1, keepdims=True)
        acc_scr[:, hs] = alpha * acc_scr[:, hs] + _dot(p.astype(BF16), vh)
        m_scr[h] = m_new


def _select(s, t, n_gt, eq_before, topk, upper, ties):
    if not ties:
        return s >= jnp.maximum(t, F32_LOWEST), eq_before
    eq = s == t
    eq_f = jnp.where(eq, 1.0, 0.0).astype(BF16)
    rank = eq_before + _dot(eq_f, upper)
    need = (topk - n_gt).astype(F32)
    sel = ((s > t) | (eq & (rank < need))) & (s > NEG_INF)
    return sel, eq_before + jnp.sum(eq_f.astype(F32), axis=1, keepdims=True)


def _init_softmax(m_scr, l_scr, acc_scr):
    m_scr[...] = jnp.full(m_scr.shape, NEG_BIG, F32)
    l_scr[...] = jnp.zeros(l_scr.shape, F32)
    acc_scr[...] = jnp.zeros(acc_scr.shape, F32)


def _finish_softmax(o_ref, l_scr, acc_scr):
    for h in range(DSA_HEADS):
        hs = slice(h * DSA_DH, (h + 1) * DSA_DH)
        o_ref[:, hs] = (acc_scr[:, hs] / l_scr[h]).astype(o_ref.dtype)


def _upper_ones(n):
    r = jnp.arange(n)
    return (r[:, None] < r[None, :]).astype(BF16)


DSA_BQ = 256
DSA_BK = 512
V_AHEAD = 3
V_RING = V_AHEAD + 1

def _count_cols(sc_scr, n_tiles, cand, strict, bq, bk):
    cand_b = jnp.broadcast_to(cand, (SUBLANE, bq))
    n_acc = 4

    def body(kt, accs):
        accs = list(accs)
        for r in range(bk // SUBLANE):
            s = sc_scr[kt, r * SUBLANE:(r + 1) * SUBLANE, :]
            accs[r % n_acc] = accs[r % n_acc] + jnp.where(s > cand_b if strict else s >= cand_b, 1, 0)
        return tuple(accs)

    zero = jnp.zeros((SUBLANE, bq), I32)
    accs = lax.fori_loop(0, n_tiles, body, (zero,) * n_acc)
    return jnp.sum((accs[0] + accs[1]) + (accs[2] + accs[3]), axis=0, keepdims=True)


def _max_min_cols(sc_scr, n_tiles, bq, bk):
    def body(kt, carry):
        mx, mn = carry
        for r in range(bk // SUBLANE):
            s = sc_scr[kt, r * SUBLANE:(r + 1) * SUBLANE, :]
            mx = jnp.maximum(mx, s)
            mn = jnp.minimum(mn, jnp.where(s > NEG_INF, s, -NEG_INF))
        return mx, mn

    init = (jnp.full((SUBLANE, bq), NEG_INF, F32), jnp.full((SUBLANE, bq), -NEG_INF, F32))
    mx, mn = lax.fori_loop(0, n_tiles, body, init)
    return jnp.max(mx, axis=0, keepdims=True), jnp.min(mn, axis=0, keepdims=True)


BISECT_WARMUP = 12
BISECT_CAP = 28


def _threshold(count, sc_scr, n_tiles, topk, bq, bk, res_f, res_i):
    zero = jnp.zeros((1, bq), F32)
    n_pos = count(zero, True)
    n_nn = count(zero, False)
    n_fin = count(jnp.full((1, bq), NEG_INF, F32), True)
    mx, mn = _max_min_cols(sc_scr, n_tiles, bq, bk)
    total = n_tiles * bk
    pos = n_pos >= topk
    at_zero = jnp.logical_not(pos) & (n_nn >= topk)
    none = n_fin < topk
    lo = jnp.where(pos | at_zero, zero, jnp.where(none, NEG_INF, mn))
    n_lo = jnp.where(pos | at_zero, n_nn, jnp.where(none, total, n_fin))
    hi = jnp.where(pos, mx * 2.0, zero)
    n_hi = jnp.where(pos, 0, jnp.where(at_zero, n_pos, n_nn))
    done = jnp.where(at_zero | none | (n_lo == topk), 1, 0)

    def movable(lo, hi, done):
        mid = lo + (hi - lo) * 0.5
        return (done == 0) & (mid > lo) & (mid < hi), mid

    def step(state):
        lo, hi, n_lo, n_hi, done = state
        ok, mid = movable(lo, hi, done)
        cnt = count(mid, False)
        up = ok & (cnt >= topk)
        down = ok & (cnt < topk)
        lo = jnp.where(up, mid, lo)
        n_lo = jnp.where(up, cnt, n_lo)
        hi = jnp.where(down, mid, hi)
        n_hi = jnp.where(down, cnt, n_hi)
        return lo, hi, n_lo, n_hi, jnp.where(n_lo == topk, 1, done)

    def any_movable(state):
        return jnp.max(jnp.where(movable(state[0], state[1], state[4])[0], 1, 0))

    state = lax.fori_loop(0, BISECT_WARMUP, lambda i, s: step(s), (lo, hi, n_lo, n_hi, done))

    def cond(c):
        return (c[0] < BISECT_CAP) & (c[1] > 0)

    def body(c):
        s = step(step(c[2]))
        return c[0] + 2, any_movable(s), s

    _, _, state = lax.while_loop(cond, body, (jnp.int32(BISECT_WARMUP), any_movable(state), state))
    lo, hi, n_lo, n_hi, done = state
    res_f[0:1, :] = lo
    res_i[0:1, :] = n_lo
    res_i[1:2, :] = n_hi

    @pl.when(jnp.min(done) == 0)
    def _():
        t, n_ge, n_gt = _kth_largest(count, total, topk, (1, bq))
        res_f[0:1, :] = t
        res_i[0:1, :] = n_ge
        res_i[1:2, :] = n_gt

    return res_f[0:1, :], res_i[0:1, :], res_i[1:2, :]


def _dsa_prompt_t_kernel(qt_ref, iqt_ref, g2t_ref, ik_ref, lower_ref, k_ref, vt_hbm,
                         o_ref, sc_scr, vbuf, sem, acc_scr, p_scr, res_f, res_i,
                         *, bq, bk, topk):
    qb = pl.program_id(0)
    q0 = qb * bq
    n_tiles = (q0 + bq + bk - 1) // bk
    sub = LANE

    def tile_scores(kt, masked):
        for s in range(bk // sub):
            ik_s = ik_ref[kt, s * sub:(s + 1) * sub, 0:IDX_DIM]
            acc = None
            for h in range(IDX_HEADS):
                y = _dot(ik_s, iqt_ref[h * IDX_DIM:(h + 1) * IDX_DIM, :])
                w = g2t_ref[GLA_GATE_RANK + h:GLA_GATE_RANK + h + 1, :] * IDX_SCALE
                t = jnp.maximum(y, 0.0) * w
                acc = t if acc is None else acc + t
            if masked:
                kpos = kt * bk + s * sub + lax.broadcasted_iota(I32, (sub, bq), 0)
                qpos = q0 + lax.broadcasted_iota(I32, (sub, bq), 1)
                shift = CHUNK.bit_length() - 1
                acc = jnp.where((kpos >> shift) <= (qpos >> shift), acc, NEG_INF)
            sc_scr[kt, s * sub:(s + 1) * sub, :] = acc

    def score_body(kt, carry):
        tile_scores(kt, False)
        return carry

    lax.fori_loop(0, n_tiles - 1, score_body, 0)
    tile_scores(n_tiles - 1, True)

    count = lambda cand, strict: _count_cols(sc_scr, n_tiles, cand, strict, bq, bk)
    t, n_ge, n_gt = _threshold(count, sc_scr, n_tiles, topk, bq, bk, res_f, res_i)
    tie_q = jnp.where((n_ge > topk) & (t > NEG_INF), 1, 0)
    any_tie = jnp.max(tie_q) > 0

    def v_copy(kt):
        slot = lax.rem(kt, V_RING)
        return pltpu.make_async_copy(vt_hbm.at[kt], vbuf.at[slot], sem.at[slot])

    def v_prefetch(kt):
        @pl.when(kt < n_tiles)
        def _():
            v_copy(kt).start()

    def select(kt, eq_before, ties):
        s = sc_scr[kt]
        if not ties:
            return s >= jnp.maximum(t, F32_LOWEST), eq_before
        eq = s == t
        eq_b = jnp.where(eq, 1.0, 0.0).astype(BF16)
        rank = eq_before + _dot(lower_ref[...], eq_b)
        need = (topk - n_gt).astype(F32)
        sel = ((s > t) | (eq & (rank < need))) & (s > NEG_INF)
        return sel, eq_before + jnp.sum(eq_b.astype(F32), axis=0, keepdims=True)

    def logits(kt, h, sel):
        hs = slice(h * DSA_DH, (h + 1) * DSA_DH)
        return jnp.where(sel, _dot(k_ref[kt, :, hs], qt_ref[hs, :]), NEG_BIG)

    def attend(ties):
        zero = jnp.zeros((1, bq), F32)
        for j in range(V_AHEAD):
            v_prefetch(j)

        def max_body(kt, carry):
            ms, eq_before = carry
            sel, eq_before = select(kt, eq_before, ties)
            ms = tuple(jnp.maximum(ms[h], jnp.max(logits(kt, h, sel), axis=0, keepdims=True))
                       for h in range(DSA_HEADS))
            return ms, eq_before

        neg = jnp.full((1, bq), NEG_BIG, F32)
        ms, _ = lax.fori_loop(0, n_tiles, max_body, ((neg,) * DSA_HEADS, zero))

        acc_scr[...] = jnp.zeros(acc_scr.shape, F32)

        def sum_body(kt, carry):
            ls, eq_before = carry
            v_prefetch(kt + V_AHEAD)
            sel, eq_before = select(kt, eq_before, ties)
            new_ls = []
            for h in range(DSA_HEADS):
                p = jnp.exp2(logits(kt, h, sel) - ms[h])
                new_ls.append(ls[h] + jnp.sum(p, axis=0, keepdims=True))
                p_scr[h] = p.astype(BF16)
            v_copy(kt).wait()
            slot = lax.rem(kt, V_RING)
            for h in range(DSA_HEADS):
                hs = slice(h * DSA_DH, (h + 1) * DSA_DH)
                acc_scr[hs, :] += _dot(vbuf[slot, hs, :], p_scr[h])
            return tuple(new_ls), eq_before

        ls, _ = lax.fori_loop(0, n_tiles, sum_body, ((zero,) * DSA_HEADS, zero))
        for h in range(DSA_HEADS):
            hs = slice(h * DSA_DH, (h + 1) * DSA_DH)
            acc_scr[hs, :] = acc_scr[hs, :] / ls[h]
        o_ref[...] = jnp.transpose(acc_scr[...]).astype(o_ref.dtype)

    @pl.when(jnp.logical_not(any_tie))
    def _():
        attend(False)

    @pl.when(any_tie)
    def _():
        attend(True)


def _dsa_prompt_t(qt, iqt, g2t, ikb, kb, vt3, bq, bk):
    t_len = qt.shape[1]
    n_kt = t_len // bk
    assert vt3.shape == (n_kt, DSA_WIDTH, bk) and bk % bq == 0
    topk = min(IDX_TOPK_MAX, t_len // 4)
    r = jnp.arange(bk)
    lower = (r[None, :] < r[:, None]).astype(BF16)
    kern = functools.partial(_dsa_prompt_t_kernel, bq=bq, bk=bk, topk=topk)
    col = lambda h: pl.BlockSpec((h, bq), lambda i: (0, i))
    ik3 = ikb.reshape(n_kt, bk, LANE)
    k3 = kb.reshape(n_kt, bk, DSA_WIDTH)
    return pl.pallas_call(
        kern,
        grid=(t_len // bq,),
        in_specs=[col(DSA_WIDTH), col(IDX_HEADS * IDX_DIM), col(LANE),
                  pl.BlockSpec(ik3.shape, lambda i: (0, 0, 0)),
                  pl.BlockSpec(lower.shape, lambda i: (0, 0)),
                  pl.BlockSpec(k3.shape, lambda i: (0, 0, 0)),
                  pl.BlockSpec(memory_space=pl.ANY)],
        out_specs=pl.BlockSpec((bq, DSA_WIDTH), lambda i: (i, 0)),
        out_shape=jax.ShapeDtypeStruct((t_len, DSA_WIDTH), BF16),
        scratch_shapes=[
            pltpu.VMEM((n_kt, bk, bq), F32),
            pltpu.VMEM((V_RING, DSA_WIDTH, bk), BF16),
            pltpu.SemaphoreType.DMA((V_RING,)),
            pltpu.VMEM((DSA_WIDTH, bq), F32),
            pltpu.VMEM((DSA_HEADS, bk, bq), BF16),
            pltpu.VMEM((SUBLANE, bq), F32),
            pltpu.VMEM((SUBLANE, bq), I32),
        ],
        compiler_params=_cparams(("arbitrary",)),
        name="dsa_prompt_t",
    )(qt, iqt, g2t, ik3, lower, k3, vt3)


def _dsa_sample_kernel(q_ref, iq_ref, g2_ref, ck_ref, cv_ref, cilo_ref, cihi_ref,
                       nk_ref, nv_ref, nilo_ref, nihi_ref, upper_ref, o_ref,
                       sc_scr, wb_scr, m_scr, l_scr, acc_scr, pk_scr, pv_scr, plo_scr, phi_scr,
                       *, bq, bk, past, topk):
    n_cache = past // bk
    sub = bk // LANE
    _fill_head_weights(wb_scr, g2_ref, bq)
    pk_scr[...] = jnp.zeros(pk_scr.shape, BF16)
    pv_scr[...] = jnp.zeros(pv_scr.shape, BF16)
    plo_scr[...] = jnp.zeros(plo_scr.shape, BF16)
    phi_scr[...] = jnp.zeros(phi_scr.shape, BF16)
    pk_scr[0:bq, :] = nk_ref[...]
    pv_scr[0:bq, :] = nv_ref[...]
    plo_scr[0:bq, :] = nilo_ref[...]
    phi_scr[0:bq, :] = nihi_ref[...]

    for kt in range(n_cache):
        s = _index_scores(iq_ref, wb_scr, cilo_ref[0, kt * bk:(kt + 1) * bk, :],
                          cihi_ref[0, kt * bk:(kt + 1) * bk, :], bk)
        for c in range(sub):
            sc_scr[kt * sub + c] = s[:, c * LANE:(c + 1) * LANE]
    s = _index_scores(iq_ref, wb_scr, plo_scr[...], phi_scr[...], LANE)
    real = lax.broadcasted_iota(I32, (bq, LANE), 1) < bq
    sc_scr[n_cache * sub] = jnp.where(real, s, NEG_INF)

    n_sub = n_cache * sub + 1
    count = lambda cand, strict: _count_rows(lambda i: sc_scr[i], n_sub, cand, strict, bq, LANE)
    t, n_ge, n_gt = _kth_largest(count, n_sub * LANE, topk, (bq, 1))
    tie_rows = jnp.where((n_ge > topk) & (t > NEG_INF), 1, 0)
    any_tie = jnp.max(tie_rows) > 0

    def attend(ties):
        _init_softmax(m_scr, l_scr, acc_scr)
        eq_before = jnp.zeros((bq, 1), F32)
        for kt in range(n_cache):
            keys = jnp.concatenate([sc_scr[kt * sub + c] for c in range(sub)], axis=1)
            upper = upper_ref[...] if ties else None
            sel, eq_before = _select(keys, t, n_gt, eq_before, topk, upper, ties)
            kv_tile = lambda h, kt=kt: (
                ck_ref[0, pl.ds(kt * bk * DSA_HEADS + h, bk, stride=DSA_HEADS), :].astype(BF16),
                cv_ref[0, pl.ds(kt * bk * DSA_HEADS + h, bk, stride=DSA_HEADS), :].astype(BF16))
            _attend_tile(sel, q_ref, kv_tile, m_scr, l_scr, acc_scr)
        upper = upper_ref[0:LANE, 0:LANE] if ties else None
        sel, eq_before = _select(sc_scr[n_cache * sub], t, n_gt, eq_before, topk, upper, ties)
        kv_tile = lambda h: (pk_scr[:, h * DSA_DH:(h + 1) * DSA_DH],
                             pv_scr[:, h * DSA_DH:(h + 1) * DSA_DH])
        _attend_tile(sel, q_ref, kv_tile, m_scr, l_scr, acc_scr)
        _finish_softmax(o_ref, l_scr, acc_scr)

    @pl.when(jnp.logical_not(any_tie))
    def _():
        attend(False)

    @pl.when(any_tie)
    def _():
        attend(True)


def _dsa_sample(q, iq, g2, cache_k, cache_v, cache_ilo, cache_ihi, kb, vb, iklo, ikhi, bq, bk):
    n_streams, past = cache_k.shape[0:2]
    cache_k = cache_k.reshape(n_streams, past * DSA_HEADS, DSA_DH)
    cache_v = cache_v.reshape(n_streams, past * DSA_HEADS, DSA_DH)
    assert past % CHUNK == 0 and bq <= CHUNK and past % bk == 0
    topk = min(IDX_TOPK_MAX, (past + bq) // 4)
    upper = _upper_ones(bk)
    kern = functools.partial(_dsa_sample_kernel, bq=bq, bk=bk, past=past, topk=topk)
    row = lambda w: pl.BlockSpec((bq, w), lambda s: (s, 0))
    cache = lambda w: pl.BlockSpec((1, past, w), lambda s: (s, 0, 0))
    cache_kv = pl.BlockSpec((1, past * DSA_HEADS, DSA_DH), lambda s: (s, 0, 0))
    n_sub = past // LANE + 1
    return pl.pallas_call(
        kern,
        grid=(n_streams,),
        in_specs=[row(DSA_WIDTH), row(IDX_HEADS * IDX_DIM), row(LANE),
                  cache_kv, cache_kv, cache(LANE), cache(LANE),
                  row(DSA_WIDTH), row(DSA_WIDTH), row(LANE), row(LANE),
                  pl.BlockSpec(upper.shape, lambda s: (0, 0))],
        out_specs=row(DSA_WIDTH),
        out_shape=jax.ShapeDtypeStruct((n_streams * bq, DSA_WIDTH), BF16),
        scratch_shapes=[
            pltpu.VMEM((n_sub, bq, LANE), F32),
            pltpu.VMEM((IDX_HEADS, bq, LANE), F32),
            pltpu.VMEM((DSA_HEADS, bq, 1), F32),
            pltpu.VMEM((DSA_HEADS, bq, 1), F32),
            pltpu.VMEM((bq, DSA_WIDTH), F32),
            pltpu.VMEM((LANE, DSA_WIDTH), BF16),
            pltpu.VMEM((LANE, DSA_WIDTH), BF16),
            pltpu.VMEM((LANE, LANE), BF16),
            pltpu.VMEM((LANE, LANE), BF16),
        ],
        compiler_params=_cparams(("parallel",)),
        name="dsa_sample",
    )(q, iq, g2, cache_k, cache_v, cache_ilo, cache_ihi, kb, vb, iklo, ikhi, upper)


def _mix_kernel(og_ref, gg_ref, od_ref, x_ref, gnw_ref, wo_ref, nfw_ref, rw_ref, rb_ref,
                h1_ref, xn_ref, re_ref, rg_ref, cnt_ref):
    g = gg_ref[...].astype(F32)
    gate = g / (1.0 + jnp.exp(-g))
    acc = _dot(od_ref[...], wo_ref[GLA_WIDTH:GLA_WIDTH + DSA_WIDTH, :])
    for h in range(GLA_HEADS):
        hs = slice(h * GLA_DV, (h + 1) * GLA_DV)
        oh = og_ref[:, hs]
        ms = jnp.mean(oh * oh, axis=-1, keepdims=True)
        a = oh * lax.rsqrt(ms + EPS) * gnw_ref[...] * gate[:, hs]
        acc = acc + _dot(a.astype(BF16), wo_ref[hs, :])
    h1 = x_ref[...] + acc
    h1_ref[...] = h1
    ms = jnp.mean(h1 * h1, axis=-1, keepdims=True)
    xn = h1 * lax.rsqrt(ms + EPS) * nfw_ref[...]
    _rows_to_tiles(xn_ref, xn)
    xn_hi = xn.astype(BF16)
    xn_lo = (xn - xn_hi.astype(F32)).astype(BF16)
    logits = (_dot(xn_hi, rw_ref[0]) + _dot(xn_hi, rw_ref[1]) + _dot(xn_lo, rw_ref[0])
              + rb_ref[...])
    lane = lax.broadcasted_iota(I32, logits.shape, 1)
    lane_f = lane.astype(F32)
    work = logits
    e_out = jnp.zeros(logits.shape, I32)
    tops, hots = [], []
    for k in range(TOP_K):
        mx = jnp.max(work, axis=1, keepdims=True)
        idx = jnp.min(jnp.where(work == mx, lane_f, float(LANE)), axis=1, keepdims=True)
        idx = idx.astype(I32)
        hot = lane == idx
        e_out = jnp.where(lane == k, idx, e_out)
        tops.append(mx)
        hots.append(hot)
        work = jnp.where(hot, NEG_BIG, work)
    ex = [jnp.exp(v - tops[0]) for v in tops]
    den = ex[0] + ex[1] + ex[2] + ex[3]
    g_out = jnp.zeros(logits.shape, F32)
    for k in range(TOP_K):
        g_out = jnp.where(lane == k, ex[k] / den, g_out)
    rg_ref[...] = g_out
    bm = logits.shape[0]
    earlier = (lax.broadcasted_iota(I32, (bm, bm), 1) < lax.broadcasted_iota(I32, (bm, bm), 0))
    earlier = jnp.where(earlier, 1.0, 0.0).astype(BF16)
    run = jnp.zeros((1, LANE), F32)
    for k in range(TOP_K):
        hot_f = jnp.where(hots[k], 1.0, 0.0)
        before = _dot(earlier, hot_f.astype(BF16)) + run
        rank = jnp.sum(jnp.where(hots[k], before, 0.0), axis=1, keepdims=True).astype(I32)
        e_out = jnp.where(lane == TOP_K + k, rank, e_out)
        run = run + jnp.sum(hot_f, axis=0, keepdims=True)
    re_ref[...] = e_out
    cnt_ref[0] = jnp.broadcast_to(run, (SUBLANE, LANE)).astype(I32)


def _mix_consts(gla_norm_w, w_out, norm_ffn_w, router_w, router_b):
    rw = jnp.zeros((D_MODEL, LANE), F32).at[:, 0:N_EXPERTS].set(router_w)
    rw_hi = rw.astype(BF16)
    rw_lo = (rw - rw_hi.astype(F32)).astype(BF16)
    rb = jnp.full((1, LANE), NEG_BIG, F32).at[0, 0:N_EXPERTS].set(router_b)
    return (gla_norm_w.reshape(1, GLA_DV), w_out.astype(BF16), norm_ffn_w.reshape(1, D_MODEL),
            jnp.stack([rw_hi, rw_lo]), rb)


def _mix(og, gla, od, x2d, consts, bm):
    n = x2d.shape[0]
    gnw, wo, nfw, rws, rb = consts
    row = lambda w: pl.BlockSpec((bm, w), lambda i: (i, 0))
    const = lambda a: pl.BlockSpec(a.shape, lambda i: (0,) * a.ndim)
    gg_col = (2 * GLA_QK + GLA_WIDTH) // GLA_WIDTH
    return pl.pallas_call(
        _mix_kernel,
        grid=(n // bm,),
        in_specs=[row(GLA_WIDTH), pl.BlockSpec((bm, GLA_WIDTH), lambda i: (i, gg_col)),
                  row(DSA_WIDTH), row(D_MODEL),
                  const(gnw), const(wo), const(nfw), const(rws), const(rb)],
        out_specs=[row(D_MODEL), pl.BlockSpec((bm * ROW_TILE, LANE), lambda i: (i, 0)),
                   row(LANE), row(LANE), pl.BlockSpec((1, SUBLANE, LANE), lambda i: (i, 0, 0))],
        out_shape=[jax.ShapeDtypeStruct((n, D_MODEL), F32),
                   jax.ShapeDtypeStruct((n * ROW_TILE, LANE), F32),
                   jax.ShapeDtypeStruct((n, LANE), I32), jax.ShapeDtypeStruct((n, LANE), F32),
                   jax.ShapeDtypeStruct((n // bm, SUBLANE, LANE), I32)],
        compiler_params=_cparams(("parallel",)),
        name="mix",
    )(og, gla, od, x2d, gnw, wo, nfw, rws, rb)


MOE_ROWS = 256
DMA_UNROLL = 8
TOKEN_BLOCK = 512


def _route_tables(counts, br, n_pairs):
    c = counts[:, 0, :]
    tot = jnp.sum(c, axis=0)
    padded = (tot + br - 1) // br * br
    pad_end = jnp.cumsum(padded)
    pad_start = pad_end - padded
    base = pad_start[None, :] + jnp.cumsum(c, axis=0) - c
    base = jnp.broadcast_to(base[:, None, :], counts.shape).astype(I32)
    nb = -(-(n_pairs + N_EXPERTS * (br - 1)) // br)
    n_used = (pad_end[N_EXPERTS - 1] // br).astype(I32)
    blk = jnp.arange(nb, dtype=I32)
    last = jnp.minimum(blk, n_used - 1) * br
    block_e = jnp.sum((pad_end[None, 0:N_EXPERTS] <= last[:, None]).astype(I32), axis=1)
    ends = pad_end[0:N_EXPERTS].astype(I32)
    tots = jnp.concatenate([tot[0:N_EXPERTS].astype(I32), n_used.reshape(1)])
    return base, jnp.minimum(block_e, N_EXPERTS - 1), n_used.reshape(1), ends, tots


def _pos_kernel(re_ref, base_ref, pos_ref):
    re = re_ref[...]
    lane = lax.broadcasted_iota(I32, re.shape, 1)
    base_row = base_ref[0, 0:1, :].astype(F32)
    p = jnp.zeros(re.shape, F32)
    for k in range(TOP_K):
        hot = lane == re[:, k:k + 1]
        first = jnp.sum(jnp.where(hot, base_row, 0.0), axis=1, keepdims=True)
        p = jnp.where(lane == k, first + re[:, TOP_K + k:TOP_K + k + 1].astype(F32), p)
    pos_ref[0] = jnp.transpose(p)[0:SUBLANE, :].astype(I32) * ROW_TILE


def _positions(re, base, bm):
    n = re.shape[0]
    return pl.pallas_call(
        _pos_kernel,
        grid=(n // bm,),
        in_specs=[pl.BlockSpec((bm, LANE), lambda i: (i, 0)),
                  pl.BlockSpec((1, SUBLANE, LANE), lambda i: (i, 0, 0))],
        out_specs=pl.BlockSpec((1, SUBLANE, bm), lambda i: (i, 0, 0)),
        out_shape=jax.ShapeDtypeStruct((n // bm, SUBLANE, bm), I32),
        compiler_params=_cparams(("parallel",)),
        name="positions",
    )(re, base)


def _dispatch_kernel(end_ref, tot_ref, pos_ref, xa_ref, xb_ref, xr_hbm, zero_scr, sem,
                     *, bt, br, nsteps_a):
    i = pl.program_id(0)

    @pl.when(i == 0)
    def _():
        zero_scr[...] = jnp.zeros(zero_scr.shape, F32)
        n_used = tot_ref[N_EXPERTS]
        nb = xr_hbm.shape[0] // (br * ROW_TILE)

        def fills():
            for e in range(N_EXPERTS):
                yield tot_ref[e] > 0, end_ref[e] - br
            for j in range(N_EXPERTS):
                yield n_used + j < nb, jnp.minimum(n_used + j, nb - 1) * br

        for act in ("start", "wait"):
            for cond, row0 in fills():
                @pl.when(cond)
                def _(row0=row0, act=act):
                    dst = xr_hbm.at[pl.ds(pl.multiple_of(row0 * ROW_TILE, ROW_TILE), br * ROW_TILE)]
                    c = pltpu.make_async_copy(zero_scr, dst, sem.at[1])
                    c.start() if act == "start" else c.wait()

    def scatter(x_ref):
        for k in range(TOP_K):
            def body(r, c, k=k):
                src = x_ref.at[pl.ds(pl.multiple_of(r * ROW_TILE, ROW_TILE), ROW_TILE)]
                dst = xr_hbm.at[pl.ds(pl.multiple_of(pos_ref[0, k, r], ROW_TILE), ROW_TILE)]
                pltpu.make_async_copy(src, dst, sem.at[0]).start()
                return c
            lax.fori_loop(0, bt, body, 0, unroll=DMA_UNROLL)

    @pl.when(i < nsteps_a)
    def _():
        scatter(xa_ref)

    @pl.when(i >= nsteps_a)
    def _():
        scatter(xb_ref)

    done = xr_hbm.at[pl.ds(0, TOP_K * bt * ROW_TILE)]
    pltpu.make_async_copy(done, done, sem.at[0]).wait()


def _dispatch(xn_a, xn_b, pos_t, pad_end, tot, n_rows, bt, br):
    na, nb_ = xn_a.shape[0] // (bt * ROW_TILE), xn_b.shape[0] // (bt * ROW_TILE)
    kern = functools.partial(_dispatch_kernel, bt=bt, br=br, nsteps_a=na)
    grid_spec = pltpu.PrefetchScalarGridSpec(
        num_scalar_prefetch=2,
        grid=(na + nb_,),
        in_specs=[
            pl.BlockSpec((1, SUBLANE, bt), lambda i, e, t: (i, 0, 0), memory_space=pltpu.SMEM),
            pl.BlockSpec((bt * ROW_TILE, LANE), lambda i, e, t: (jnp.minimum(i, na - 1), 0)),
            pl.BlockSpec((bt * ROW_TILE, LANE), lambda i, e, t: (jnp.maximum(i - na, 0), 0)),
        ],
        out_specs=pl.BlockSpec(memory_space=pl.ANY),
        scratch_shapes=[pltpu.VMEM((br * ROW_TILE, LANE), F32), pltpu.SemaphoreType.DMA((2,))],
    )
    return pl.pallas_call(
        kern,
        grid_spec=grid_spec,
        out_shape=jax.ShapeDtypeStruct((n_rows * ROW_TILE, LANE), F32),
        compiler_params=_cparams(("arbitrary",)),
        name="dispatch",
    )(pad_end, tot, pos_t, xn_a, xn_b)


def _rows_to_tiles(ref, x):
    n = x.shape[0]
    for j in range(ROW_TILE):
        ref[pl.ds(j, n, stride=ROW_TILE), :] = x[:, j * LANE:(j + 1) * LANE]


def _tile_chunk(ref, j, n, row0=0):
    return ref[pl.ds(row0 * ROW_TILE + j, n, stride=ROW_TILE), :]


def _rows_from_tiles(ref, n):
    return jnp.concatenate([_tile_chunk(ref, j, n) for j in range(ROW_TILE)], axis=1)


def _moe_kernel(be_ref, nu_ref, x_ref, wgu_ref, bgu_ref, wd_ref, bdn_ref,
                y_ref, wgu_bf, wd_bf):
    i = pl.program_id(0)
    n_used = nu_ref[0]

    @pl.when(i < n_used)
    def _():
        e = be_ref[i]
        prev = be_ref[jnp.maximum(i - 1, 0)]

        @pl.when((i == 0) | (e != prev))
        def _():
            wgu_bf[...] = wgu_ref[0].astype(BF16)
            wd_bf[...] = wd_ref[0].astype(BF16)

        xb = _rows_from_tiles(x_ref, x_ref.shape[0] // ROW_TILE).astype(BF16)
        gu = _dot(xb, wgu_bf[...]) + bgu_ref[0]
        g_lin = jnp.minimum(gu[:, 0:D_FF], SWIGLU_LIMIT)
        u_lin = jnp.clip(gu[:, D_FF:2 * D_FF], -SWIGLU_LIMIT, SWIGLU_LIMIT)
        act = g_lin / (1.0 + jnp.exp(-SWIGLU_ALPHA * g_lin)) * (u_lin + 1.0)
        y = _dot(act.astype(BF16), wd_bf[...]) + bdn_ref[0]
        _rows_to_tiles(y_ref, y)

    @pl.when(i >= n_used)
    def _():
        y_ref[...] = jnp.zeros(y_ref.shape, y_ref.dtype)


def _moe(x_rows, block_e, n_used, w_gu, b_gu, w_down, b_down, br):
    nb = x_rows.shape[0] // (br * ROW_TILE)
    grid_spec = pltpu.PrefetchScalarGridSpec(
        num_scalar_prefetch=2,
        grid=(nb,),
        in_specs=[
            pl.BlockSpec((br * ROW_TILE, LANE),
                         lambda i, be, nu: (jnp.minimum(i, nu[0] - 1), 0)),
            pl.BlockSpec((1, D_MODEL, 2 * D_FF), lambda i, be, nu: (be[i], 0, 0)),
            pl.BlockSpec((1, 1, 2 * D_FF), lambda i, be, nu: (be[i], 0, 0)),
            pl.BlockSpec((1, D_FF, D_MODEL), lambda i, be, nu: (be[i], 0, 0)),
            pl.BlockSpec((1, 1, D_MODEL), lambda i, be, nu: (be[i], 0, 0)),
        ],
        out_specs=pl.BlockSpec((br * ROW_TILE, LANE), lambda i, be, nu: (i, 0)),
        scratch_shapes=[
            pltpu.VMEM((D_MODEL, 2 * D_FF), BF16),
            pltpu.VMEM((D_FF, D_MODEL), BF16),
        ],
    )
    return pl.pallas_call(
        _moe_kernel,
        grid_spec=grid_spec,
        out_shape=jax.ShapeDtypeStruct((nb * br * ROW_TILE, LANE), F32),
        compiler_params=_cparams(("arbitrary",)),
        name="moe",
    )(block_e, n_used, x_rows, w_gu, b_gu.reshape(N_EXPERTS, 1, 2 * D_FF),
      w_down, b_down.reshape(N_EXPERTS, 1, D_MODEL))


def _combine_kernel(pos_ref, posn_ref, y_hbm, h1_ref, rg_ref, fw_ref, o_ref, yg, sem, *, bt):
    i = pl.program_id(0)
    nsteps = pl.num_programs(0)
    slot = lax.rem(i, 2)

    def issue(pref, s):
        for k in range(TOP_K):
            def body(r, c, k=k):
                src = y_hbm.at[pl.ds(pl.multiple_of(pref[0, k, r], ROW_TILE), ROW_TILE)]
                row0 = pl.multiple_of((k * bt + r) * ROW_TILE, ROW_TILE)
                pltpu.make_async_copy(src, yg.at[s, pl.ds(row0, ROW_TILE)], sem.at[s]).start()
                return c
            lax.fori_loop(0, bt, body, 0, unroll=DMA_UNROLL)

    @pl.when(i == 0)
    def _():
        issue(pos_ref, 0)

    @pl.when(i + 1 < nsteps)
    def _():
        issue(posn_ref, 1 - slot)

    pltpu.make_async_copy(yg.at[slot], yg.at[slot], sem.at[slot]).wait()
    gates = [jnp.broadcast_to(rg_ref[:, k:k + 1], (bt, LANE)) for k in range(TOP_K)]
    ss = jnp.zeros((bt, LANE), F32)
    rows = yg.at[slot]
    for j in range(ROW_TILE):
        js = slice(j * LANE, (j + 1) * LANE)
        a = h1_ref[:, js]
        for k in range(TOP_K):
            a = a + gates[k] * _tile_chunk(rows, j, bt, k * bt)
        ss = ss + a * a
        o_ref[:, js] = a
    ms = jnp.sum(ss, axis=-1, keepdims=True) * (1.0 / D_MODEL)
    o_ref[...] = o_ref[...] * lax.rsqrt(ms + EPS) * fw_ref[...]


def _combine(y_rows, pos_t, h1, rg, final_w, bt):
    n = h1.shape[0]
    nsteps = n // bt
    kern = functools.partial(_combine_kernel, bt=bt)
    row = lambda w: pl.BlockSpec((bt, w), lambda i: (i, 0))
    return pl.pallas_call(
        kern,
        grid=(nsteps,),
        in_specs=[
            pl.BlockSpec((1, SUBLANE, bt), lambda i: (i, 0, 0), memory_space=pltpu.SMEM),
            pl.BlockSpec((1, SUBLANE, bt), lambda i: (jnp.minimum(i + 1, nsteps - 1), 0, 0),
                         memory_space=pltpu.SMEM),
            pl.BlockSpec(memory_space=pl.ANY),
            row(D_MODEL), row(LANE),
            pl.BlockSpec((1, D_MODEL), lambda i: (0, 0)),
        ],
        out_specs=row(D_MODEL),
        out_shape=jax.ShapeDtypeStruct((n, D_MODEL), F32),
        scratch_shapes=[pltpu.VMEM((2, TOP_K * bt * ROW_TILE, LANE), F32),
                        pltpu.SemaphoreType.DMA((2,))],
        compiler_params=_cparams(("arbitrary",)),
        name="combine",
    )(pos_t, pos_t, y_rows, h1, rg, final_w.reshape(1, D_MODEL))


def kernel(x_prompt, x_sample, cache_k, cache_v, cache_k_idx, state_gla, norm_mix_w, w_in,
           w_gla_a2, b_gla_a2, gla_norm_w, w_out, norm_ffn_w, router_w, router_b, w_gu, b_gu,
           w_down, b_down, norm_final_w):
    b_p, t_p, _ = x_prompt.shape
    b_s, t_s, _ = x_sample.shape
    past = cache_k.shape[2]
    assert b_p == 1 and norm_mix_w.shape[0] == 1
    l = 0
    pw = _proj_weights(norm_mix_w[l], w_in[l])
    gconsts = _gla_consts(w_gla_a2[l], b_gla_a2[l])

    pp = _project_t(x_prompt.reshape(t_p, D_MODEL), _proj_t_weights(norm_mix_w[l], w_in[l]), DSA_BK)
    gla_p, kf_p, vf_p, kb_p, ikf_p, ikb_p, g2_p, qt_p, iqt_p, vt3_p, g2t_p = pp
    s0 = jnp.zeros((1, GLA_WIDTH, GLA_QK), F32)
    og_p, st_p = _gla(gla_p, g2_p, s0, gconsts, 1, t_p, 512, 16)
    od_p = _dsa_prompt_t(qt_p, iqt_p, g2t_p, ikb_p, kb_p, vt3_p, DSA_BQ, DSA_BK)

    ps = _project(x_sample.reshape(b_s * t_s, D_MODEL), pw, 512)
    gla_s, q_s, kf_s, vf_s, kb_s, vb_s, iq_s, iklo_s, ikhi_s, ikf_s, g2_s = ps
    og_s, st_s = _gla(gla_s, g2_s, _state_to_kernel(state_gla[l]), gconsts, b_s, t_s, t_s, 16)
    zc = jnp.zeros((b_s, past, IDX_DIM), BF16)
    cik = cache_k_idx[l].astype(BF16)
    cilo = jnp.concatenate([cik, zc], axis=2)
    cihi = jnp.concatenate([zc, cik], axis=2)
    od_s = _dsa_sample(q_s, iq_s, g2_s, cache_k[l], cache_v[l], cilo, cihi,
                       kb_s, vb_s, iklo_s, ikhi_s, t_s, 512)

    mconsts = _mix_consts(gla_norm_w[l], w_out[l], norm_ffn_w[l], router_w[l], router_b[l])
    n_s = b_s * t_s
    tb = TOKEN_BLOCK
    h1_p, xn_p, re_p, rg_p, cnt_p = _mix(og_p, gla_p, od_p, x_prompt.reshape(t_p, D_MODEL), mconsts, tb)
    h1_s, xn_s, re_s, rg_s, cnt_s = _mix(og_s, gla_s, od_s, x_sample.reshape(n_s, D_MODEL), mconsts, tb)
    n_pairs = (t_p + n_s) * TOP_K
    base, block_e, n_used, ends, tots = _route_tables(
        jnp.concatenate([cnt_p, cnt_s], axis=0), MOE_ROWS, n_pairs)
    n_rows = block_e.shape[0] * MOE_ROWS
    pos_p = _positions(re_p, base[0:t_p // tb], tb)
    pos_s = _positions(re_s, base[t_p // tb:], tb)
    x_rows = _dispatch(xn_p, xn_s, jnp.concatenate([pos_p, pos_s], axis=0), ends, tots,
                       n_rows, tb, MOE_ROWS)
    y_rows = _moe(x_rows, block_e, n_used, w_gu[l], b_gu[l], w_down[l], b_down[l], MOE_ROWS)
    y_p = _combine(y_rows, pos_p, h1_p, rg_p, norm_final_w, tb)
    y_s = _combine(y_rows, pos_s, h1_s, rg_s, norm_final_w, tb)
    y_p = y_p.reshape(x_prompt.shape)
    y_s = y_s.reshape(x_sample.shape)
    return (y_p, y_s,
            kf_p.reshape(1, 1, t_p, DSA_HEADS, DSA_DH), vf_p.reshape(1, 1, t_p, DSA_HEADS, DSA_DH),
            ikf_p.reshape(1, 1, t_p, IDX_DIM), _state_from_kernel(st_p)[None],
            kf_s.reshape(1, b_s, t_s, DSA_HEADS, DSA_DH), vf_s.reshape(1, b_s, t_s, DSA_HEADS, DSA_DH),
            ikf_s.reshape(1, b_s, t_s, IDX_DIM), _state_from_kernel(st_s)[None])
```

```python
import functools

import jax
import jax.numpy as jnp
from jax import lax
from jax.experimental import pallas as pl
from jax.experimental.pallas import tpu as pltpu

F32 = jnp.float32
BF16 = jnp.bfloat16
I32 = jnp.int32

D_MODEL = 1024
CHUNK = 64
GLA_HEADS = 4
GLA_DK = 64
GLA_DV = 128
GLA_QK = GLA_HEADS * GLA_DK
GLA_WIDTH = GLA_HEADS * GLA_DV
GLA_GATE_RANK = 16
GLA_GATE_TAU = 16.0
DSA_HEADS = 4
DSA_DH = 128
DSA_WIDTH = DSA_HEADS * DSA_DH
IDX_HEADS = 8
IDX_DIM = 64
IDX_TOPK_MAX = 256
N_EXPERTS = 32
TOP_K = 4
D_FF = 1024
SWIGLU_ALPHA = 1.702
SWIGLU_LIMIT = 7.0
EPS = 1e-6

_OFF_GLA = 0
_W_GLA = 2 * GLA_QK + 2 * GLA_WIDTH
_OFF_GA = _OFF_GLA + _W_GLA
_OFF_DSA = _OFF_GA + GLA_GATE_RANK
_W_DSA = 3 * DSA_WIDTH + IDX_HEADS * IDX_DIM
_OFF_IK = _OFF_DSA + _W_DSA
_OFF_IW = _OFF_IK + IDX_DIM
LANE = 128
SUBLANE = 8
ROW_TILE = D_MODEL // LANE
assert ROW_TILE == SUBLANE
LOG2E = 1.4426950408889634
NEG_BIG = -1e30
VMEM_LIMIT = 56 * 1024 * 1024


def _cparams(sem):
    return pltpu.CompilerParams(dimension_semantics=sem, vmem_limit_bytes=VMEM_LIMIT)


def _dot(a, b):
    return jnp.dot(a, b, preferred_element_type=F32)


def _dot_nt(a, b):
    return lax.dot_general(a, b, (((1,), (1,)), ((), ())), preferred_element_type=F32)


def _dot_tn(a, b):
    return lax.dot_general(a, b, (((0,), (0,)), ((), ())), preferred_element_type=F32)


def _split3(x):
    hi = x.astype(BF16)
    r1 = x - hi.astype(F32)
    mid = r1.astype(BF16)
    lo = (r1 - mid.astype(F32)).astype(BF16)
    return hi, mid, lo


def _proj_kernel(x_ref, nw_ref, wg_ref, wd_ref, ws_ref,
                 gla_ref, q_ref, kf_ref, vf_ref, kb_ref, vb_ref, iq_ref,
                 iklo_ref, ikhi_ref, ikf_ref, g2_ref):
    x = x_ref[...]
    ms = jnp.mean(x * x, axis=-1, keepdims=True)
    xn = (x * lax.rsqrt(ms + EPS) * nw_ref[...]).astype(BF16)
    gla_ref[...] = _dot(xn, wg_ref[...]).astype(BF16)
    W = DSA_WIDTH
    dq = _dot(xn, wd_ref[:, 0:W])
    q_ref[...] = (dq * (DSA_DH ** -0.5 * LOG2E)).astype(BF16)
    dk = _dot(xn, wd_ref[:, W:2 * W])
    kf_ref[...] = dk
    kb_ref[...] = dk.astype(BF16)
    dv = _dot(xn, wd_ref[:, 2 * W:3 * W])
    vf_ref[...] = dv
    vb_ref[...] = dv.astype(BF16)
    iq_ref[...] = _dot(xn, wd_ref[:, 3 * W:4 * W]).astype(BF16)
    sm = _dot(xn, ws_ref[...])
    iklo_ref[...] = sm[:, 0:LANE].astype(BF16)
    ikhi_ref[...] = sm[:, LANE:2 * LANE].astype(BF16)
    ikf_ref[...] = sm[:, 0:IDX_DIM]
    g2_ref[...] = sm[:, 2 * LANE:3 * LANE]


def _proj_weights(norm_w, w_in):
    wg = w_in[:, _OFF_GLA:_OFF_GLA + _W_GLA].astype(BF16)
    wd = w_in[:, _OFF_DSA:_OFF_DSA + _W_DSA].astype(BF16)
    ik = w_in[:, _OFF_IK:_OFF_IK + IDX_DIM]
    z64 = jnp.zeros((D_MODEL, IDX_DIM), F32)
    ga = w_in[:, _OFF_GA:_OFF_GA + GLA_GATE_RANK]
    iw = w_in[:, _OFF_IW:_OFF_IW + IDX_HEADS]
    zpad = jnp.zeros((D_MODEL, LANE - GLA_GATE_RANK - IDX_HEADS), F32)
    ws = jnp.concatenate([ik, z64, z64, ik, ga, iw, zpad], axis=1).astype(BF16)
    return norm_w.reshape(1, D_MODEL), wg, wd, ws


def _project(x2d, pw, bm):
    n = x2d.shape[0]
    nw, wg, wd, ws = pw
    row = lambda w: pl.BlockSpec((bm, w), lambda i: (i, 0))
    full = lambda a: pl.BlockSpec(a.shape, lambda i: (0, 0))
    outs = [(_W_GLA, BF16), (DSA_WIDTH, BF16), (DSA_WIDTH, F32), (DSA_WIDTH, F32),
            (DSA_WIDTH, BF16), (DSA_WIDTH, BF16), (IDX_HEADS * IDX_DIM, BF16),
            (LANE, BF16), (LANE, BF16), (IDX_DIM, F32), (LANE, F32)]
    return pl.pallas_call(
        _proj_kernel,
        grid=(n // bm,),
        in_specs=[row(D_MODEL), full(nw), full(wg), full(wd), full(ws)],
        out_specs=[row(w) for w, _ in outs],
        out_shape=[jax.ShapeDtypeStruct((n, w), dt) for w, dt in outs],
        compiler_params=_cparams(("parallel",)),
        name="proj",
    )(x2d, nw, wg, wd, ws)


def _proj_t_kernel(x_ref, nw_ref, wg_ref, wkv_ref, ws_ref, wt_ref, wst_ref,
                   gla_ref, kf_ref, vf_ref, kb_ref, ikf_ref, ikb_ref, g2_ref,
                   qt_ref, iqt_ref, vt_ref, g2t_ref):
    x = x_ref[...]
    ms = jnp.mean(x * x, axis=-1, keepdims=True)
    xn = (x * lax.rsqrt(ms + EPS) * nw_ref[...]).astype(BF16)
    gla_ref[...] = _dot(xn, wg_ref[...]).astype(BF16)
    W = DSA_WIDTH
    dk = _dot(xn, wkv_ref[:, 0:W])
    dv = _dot(xn, wkv_ref[:, W:2 * W])
    kb_ref[...] = dk.astype(BF16)
    for h in range(DSA_HEADS):
        kf_ref[:, h, :] = dk[:, h * DSA_DH:(h + 1) * DSA_DH]
        vf_ref[:, h, :] = dv[:, h * DSA_DH:(h + 1) * DSA_DH]
    sm = _dot(xn, ws_ref[...])
    ikb_ref[...] = sm[:, 0:LANE].astype(BF16)
    ikf_ref[...] = sm[:, 0:IDX_DIM]
    g2_ref[...] = sm[:, LANE:2 * LANE]
    qt_ref[...] = (_dot_nt(wt_ref[0:W, :], xn) * (DSA_DH ** -0.5 * LOG2E)).astype(BF16)
    vt_ref[0] = _dot_nt(wt_ref[W:2 * W, :], xn).astype(BF16)
    iqt_ref[...] = _dot_nt(wt_ref[2 * W:3 * W, :], xn).astype(BF16)
    g2t_ref[...] = _dot_nt(wst_ref[...], xn)


def _proj_t_weights(norm_w, w_in):
    wg = w_in[:, _OFF_GLA:_OFF_GLA + _W_GLA].astype(BF16)
    W = DSA_WIDTH
    wq = w_in[:, _OFF_DSA:_OFF_DSA + W]
    wkv = w_in[:, _OFF_DSA + W:_OFF_DSA + 3 * W]
    wv = w_in[:, _OFF_DSA + 2 * W:_OFF_DSA + 3 * W]
    wiq = w_in[:, _OFF_DSA + 3 * W:_OFF_DSA + 4 * W]
    ik = w_in[:, _OFF_IK:_OFF_IK + IDX_DIM]
    z64 = jnp.zeros((D_MODEL, IDX_DIM), F32)
    ga = w_in[:, _OFF_GA:_OFF_GA + GLA_GATE_RANK]
    iw = w_in[:, _OFF_IW:_OFF_IW + IDX_HEADS]
    zpad = jnp.zeros((D_MODEL, LANE - GLA_GATE_RANK - IDX_HEADS), F32)
    g2w = jnp.concatenate([ga, iw, zpad], axis=1)
    ws = jnp.concatenate([ik, z64, g2w], axis=1).astype(BF16)
    wt = jnp.concatenate([wq, wv, wiq], axis=1).T.astype(BF16)
    return norm_w.reshape(1, D_MODEL), wg, wkv.astype(BF16), ws, wt, g2w.T.astype(BF16)


def _project_t(x2d, pw, bm):
    n = x2d.shape[0]
    row = lambda w: pl.BlockSpec((bm, w), lambda i: (i, 0))
    row3 = pl.BlockSpec((bm, DSA_HEADS, DSA_DH), lambda i: (i, 0, 0))
    col = lambda h: pl.BlockSpec((h, bm), lambda i: (0, i))
    full = lambda a: pl.BlockSpec(a.shape, lambda i: (0, 0))
    out_specs = [row(_W_GLA), row3, row3, row(DSA_WIDTH), row(IDX_DIM), row(LANE), row(LANE),
                 col(DSA_WIDTH), col(IDX_HEADS * IDX_DIM),
                 pl.BlockSpec((1, DSA_WIDTH, bm), lambda i: (i, 0, 0)), col(LANE)]
    out_shape = [jax.ShapeDtypeStruct((n, _W_GLA), BF16),
                 jax.ShapeDtypeStruct((n, DSA_HEADS, DSA_DH), F32),
                 jax.ShapeDtypeStruct((n, DSA_HEADS, DSA_DH), F32),
                 jax.ShapeDtypeStruct((n, DSA_WIDTH), BF16),
                 jax.ShapeDtypeStruct((n, IDX_DIM), F32),
                 jax.ShapeDtypeStruct((n, LANE), BF16),
                 jax.ShapeDtypeStruct((n, LANE), F32),
                 jax.ShapeDtypeStruct((DSA_WIDTH, n), BF16),
                 jax.ShapeDtypeStruct((IDX_HEADS * IDX_DIM, n), BF16),
                 jax.ShapeDtypeStruct((n // bm, DSA_WIDTH, bm), BF16),
                 jax.ShapeDtypeStruct((LANE, n), F32)]
    return pl.pallas_call(
        _proj_t_kernel,
        grid=(n // bm,),
        in_specs=[row(D_MODEL)] + [full(a) for a in pw],
        out_specs=out_specs,
        out_shape=out_shape,
        compiler_params=_cparams(("parallel",)),
        name="proj_t",
    )(x2d, *pw)


def _log_sigmoid(x):
    return jnp.minimum(x, 0.0) - jnp.log1p(jnp.exp(-jnp.abs(x)))


def _gla_kernel(gla_ref, g2_ref, w2_ref, b2_ref, seg_ref, bd_ref, s0_ref,
                o_ref, sT_ref,
                st_scr, kpad, bpad, vpad, qt_scr, kt_scr, dec_scr, oi_scr,
                *, bt, c):
    t = pl.program_id(1)

    @pl.when(t == 0)
    def _():
        st_scr[...] = s0_ref[0]

    q = gla_ref[:, 0:GLA_QK].astype(F32) * (GLA_DK ** -0.5)
    k = gla_ref[:, GLA_QK:2 * GLA_QK].astype(F32)
    vb = gla_ref[:, 2 * GLA_QK:2 * GLA_QK + GLA_WIDTH]
    v = vb.astype(F32)

    ga = g2_ref[...]
    ga_hi = ga.astype(BF16)
    ga_lo = (ga - ga_hi.astype(F32)).astype(BF16)
    logit = (_dot(ga_hi, w2_ref[0]) + _dot(ga_hi, w2_ref[1]) + _dot(ga_lo, w2_ref[0])
             + b2_ref[...])
    lg = _log_sigmoid(logit) * (1.0 / GLA_GATE_TAU)

    shift = c.bit_length() - 1
    row = lax.broadcasted_iota(I32, (bt, bt), 0)
    col = lax.broadcasted_iota(I32, (bt, bt), 1)
    same = (row >> shift) == (col >> shift)
    tri = jnp.where(same & (col <= row), 1.0, 0.0).astype(BF16)
    last = jnp.where(same & ((col & (c - 1)) == c - 1), 1.0, 0.0).astype(BF16)
    l0, l1, l2 = _split3(lg)
    b = _dot(tri, l0) + _dot(tri, l1) + _dot(tri, l2)
    b0, b1, b2s = _split3(b)
    bl = _dot(last, b0) + _dot(last, b1) + _dot(last, b2s)

    zpad = jnp.zeros((c, GLA_QK), F32)
    kpad[0:c, :] = zpad
    bpad[0:c, :] = zpad
    vpad[0:c, :] = jnp.zeros((c, GLA_WIDTH), F32)
    kpad[c:c + bt, :] = k
    bpad[c:c + bt, :] = b
    vpad[c:c + bt, :] = v
    pos = lax.broadcasted_iota(I32, (bt, GLA_QK), 0) & (c - 1)
    seg = seg_ref[...]
    o_intra = jnp.zeros((bt, GLA_WIDTH), F32)
    for d in range(c):
        ks = kpad[c - d:c - d + bt, :]
        bs = bpad[c - d:c - d + bt, :]
        vs = vpad[c - d:c - d + bt, :]
        z = q * ks * jnp.exp(jnp.minimum(b - bs, 0.0))
        z = jnp.where(pos >= d, z, 0.0)
        o_intra = o_intra + _dot(z.astype(BF16), seg) * vs
    oi_scr[...] = o_intra

    qt_scr[...] = q * jnp.exp(b)
    kt_scr[...] = k * jnp.exp(bl - b)
    dec_scr[...] = jnp.exp(bl)
    bd = bd_ref[...]

    def step(ci, carry):
        r0 = pl.multiple_of(ci * c, c)
        qc = qt_scr[pl.ds(r0, c), :].astype(BF16)
        kc = kt_scr[pl.ds(r0, c), :].astype(BF16)
        vc = vpad[pl.ds(r0 + c, c), :].astype(BF16)
        st = st_scr[...]
        o_ref[pl.ds(r0, c), :] = oi_scr[pl.ds(r0, c), :] + _dot_nt(qc, st.astype(BF16))
        dec = dec_scr[pl.ds(r0, 1), :]
        st_scr[...] = st * dec + _dot_tn(vc, kc) * bd
        return carry

    lax.fori_loop(0, bt // c, step, 0)

    @pl.when(t == pl.num_programs(1) - 1)
    def _():
        sT_ref[0] = st_scr[...]


def _gla_consts(w_gla_a2, b_gla_a2):
    w2 = jnp.zeros((LANE, GLA_QK), F32).at[0:GLA_GATE_RANK].set(w_gla_a2)
    w2_hi = w2.astype(BF16)
    w2_lo = (w2 - w2_hi.astype(F32)).astype(BF16)
    w2s = jnp.stack([w2_hi, w2_lo])
    hq = jnp.arange(GLA_QK) // GLA_DK
    hv = jnp.arange(GLA_WIDTH) // GLA_DV
    seg = (hq[:, None] == hv[None, :]).astype(BF16)
    bd = (hv[:, None] == hq[None, :]).astype(F32)
    return w2s, b_gla_a2.reshape(1, GLA_QK), seg, bd


def _state_to_kernel(s):
    n = s.shape[0]
    eye = jnp.eye(GLA_HEADS, dtype=s.dtype)
    t = jnp.swapaxes(s, 2, 3)[:, :, :, None, :] * eye[None, :, None, :, None]
    return t.reshape(n, GLA_WIDTH, GLA_QK)


def _state_from_kernel(t):
    n = t.shape[0]
    t5 = t.reshape(n, GLA_HEADS, GLA_DV, GLA_HEADS, GLA_DK)
    return jnp.einsum("shehd->shde", t5)


def _gla(gla, g2, s0, consts, n_streams, t_len, bt, c):
    w2s, b2, seg, bd = consts
    nb = t_len // bt
    kern = functools.partial(_gla_kernel, bt=bt, c=c)
    const = lambda a: pl.BlockSpec(a.shape, lambda s, t: (0,) * a.ndim)
    o, sT = pl.pallas_call(
        kern,
        grid=(n_streams, nb),
        in_specs=[
            pl.BlockSpec((bt, 2 * GLA_QK + GLA_WIDTH), lambda s, t: (s * nb + t, 0)),
            pl.BlockSpec((bt, LANE), lambda s, t: (s * nb + t, 0)),
            const(w2s), const(b2), const(seg), const(bd),
            pl.BlockSpec((1, GLA_WIDTH, GLA_QK), lambda s, t: (s, 0, 0)),
        ],
        out_specs=[
            pl.BlockSpec((bt, GLA_WIDTH), lambda s, t: (s * nb + t, 0)),
            pl.BlockSpec((1, GLA_WIDTH, GLA_QK), lambda s, t: (s, 0, 0)),
        ],
        out_shape=[
            jax.ShapeDtypeStruct((n_streams * t_len, GLA_WIDTH), F32),
            jax.ShapeDtypeStruct((n_streams, GLA_WIDTH, GLA_QK), F32),
        ],
        scratch_shapes=[
            pltpu.VMEM((GLA_WIDTH, GLA_QK), F32),
            pltpu.VMEM((bt + c, GLA_QK), F32),
            pltpu.VMEM((bt + c, GLA_QK), F32),
            pltpu.VMEM((bt + c, GLA_WIDTH), F32),
            pltpu.VMEM((bt, GLA_QK), F32),
            pltpu.VMEM((bt, GLA_QK), F32),
            pltpu.VMEM((bt, GLA_QK), F32),
            pltpu.VMEM((bt, GLA_WIDTH), F32),
        ],
        compiler_params=_cparams(("arbitrary", "arbitrary")),
        name="gla",
    )(gla, g2, w2s, b2, seg, bd, s0)
    return o, sT


INT_MIN = -2 ** 31
NEG_INF = float("-inf")
KEY_NEG_INF = -2139095041
F32_LOWEST = -3.4028234663852886e38
IDX_SCALE = IDX_DIM ** -0.5 * IDX_HEADS ** -0.5


def _fill_head_weights(wb_scr, g2_ref, bq):
    for h in range(IDX_HEADS):
        c0 = GLA_GATE_RANK + h
        col = g2_ref[:, c0:c0 + 1] * IDX_SCALE
        wb_scr[h] = jnp.broadcast_to(col, (bq, LANE))


def _index_scores(iq_ref, wb_scr, iklo, ikhi, bk):
    parts = [None] * (bk // LANE)
    for p in range(IDX_HEADS // 2):
        iqp = iq_ref[:, p * LANE:(p + 1) * LANE]
        y0 = jnp.maximum(_dot_nt(iqp, iklo), 0.0)
        y1 = jnp.maximum(_dot_nt(iqp, ikhi), 0.0)
        w0 = wb_scr[2 * p]
        w1 = wb_scr[2 * p + 1]
        for c in range(bk // LANE):
            t = y0[:, c * LANE:(c + 1) * LANE] * w0 + y1[:, c * LANE:(c + 1) * LANE] * w1
            parts[c] = t if parts[c] is None else parts[c] + t
    return jnp.concatenate(parts, axis=1)


def _key_to_float(key):
    key = jnp.maximum(key, KEY_NEG_INF)
    return pltpu.bitcast(key ^ ((key >> 31) & 0x7FFFFFFF), F32)


def _kth_largest(count, total, topk, shape):
    def bit_body(i, carry):
        prefix, n_ge = carry
        cand_u = prefix | jnp.left_shift(jnp.int32(1), 31 - i)
        cnt = count(_key_to_float(cand_u ^ INT_MIN), False)
        ok = cnt >= topk
        return jnp.where(ok, cand_u, prefix), jnp.where(ok, cnt, n_ge)

    init = (jnp.zeros(shape, I32), jnp.zeros(shape, I32) + total)
    prefix, n_ge = lax.fori_loop(0, 32, bit_body, init)
    t = _key_to_float(prefix ^ INT_MIN)
    return t, n_ge, count(t, True)


def _count_rows(load_tile, n_tiles, cand, strict, bq, bk):
    cand_b = jnp.broadcast_to(cand, (bq, LANE))

    def body(kt, acc):
        s = load_tile(kt)
        for c in range(bk // LANE):
            sc = s[:, c * LANE:(c + 1) * LANE]
            acc = acc + jnp.where(sc > cand_b if strict else sc >= cand_b, 1, 0)
        return acc

    acc = lax.fori_loop(0, n_tiles, body, jnp.zeros((bq, LANE), I32))
    return jnp.sum(acc, axis=1, keepdims=True)


def _attend_tile(sel, q_ref, kv_tile, m_scr, l_scr, acc_scr):
    for h in range(DSA_HEADS):
        hs = slice(h * DSA_DH, (h + 1) * DSA_DH)
        kh, vh = kv_tile(h)
        lg = jnp.where(sel, _dot_nt(q_ref[:, hs], kh), NEG_BIG)
        m_old = m_scr[h]
        m_new = jnp.maximum(m_old, jnp.max(lg, axis=1, keepdims=True))
        alpha = jnp.exp2(m_old - m_new)
        p = jnp.exp2(lg - m_new)
        l_scr[h] = alpha * l_scr[h] + jnp.sum(p, axis=1, keepdims=True)
        acc_scr[:, hs] = alpha * acc_scr[:, hs] + _dot(p.astype(BF16), vh)
        m_scr[h] = m_new


def _select(s, t, n_gt, eq_before, topk, upper, ties):
    if not ties:
        return s >= jnp.maximum(t, F32_LOWEST), eq_before
    eq = s == t
    eq_f = jnp.where(eq, 1.0, 0.0).astype(BF16)
    rank = eq_before + _dot(eq_f, upper)
    need = (topk - n_gt).astype(F32)
    sel = ((s > t) | (eq & (rank < need))) & (s > NEG_INF)
    return sel, eq_before + jnp.sum(eq_f.astype(F32), axis=1, keepdims=True)


def _init_softmax(m_scr, l_scr, acc_scr):
    m_scr[...] = jnp.full(m_scr.shape, NEG_BIG, F32)
    l_scr[...] = jnp.zeros(l_scr.shape, F32)
    acc_scr[...] = jnp.zeros(acc_scr.shape, F32)


def _finish_softmax(o_ref, l_scr, acc_scr):
    for h in range(DSA_HEADS):
        hs = slice(h * DSA_DH, (h + 1) * DSA_DH)
        o_ref[:, hs] = (acc_scr[:, hs] / l_scr[h]).astype(o_ref.dtype)


def _upper_ones(n):
    r = jnp.arange(n)
    return (r[:, None] < r[None, :]).astype(BF16)


DSA_BQ = 256
DSA_BK = 512
V_AHEAD = 3
V_RING = V_AHEAD + 1

def _count_cols(sc_scr, n_tiles, cand, strict, bq, bk):
    cand_b = jnp.broadcast_to(cand, (SUBLANE, bq))
    n_acc = 4

    def body(kt, accs):
        accs = list(accs)
        for r in range(bk // SUBLANE):
            s = sc_scr[kt, r * SUBLANE:(r + 1) * SUBLANE, :]
            accs[r % n_acc] = accs[r % n_acc] + jnp.where(s > cand_b if strict else s >= cand_b, 1, 0)
        return tuple(accs)

    zero = jnp.zeros((SUBLANE, bq), I32)
    accs = lax.fori_loop(0, n_tiles, body, (zero,) * n_acc)
    return jnp.sum((accs[0] + accs[1]) + (accs[2] + accs[3]), axis=0, keepdims=True)


def _max_min_cols(sc_scr, n_tiles, bq, bk):
    def body(kt, carry):
        mx, mn = carry
        for r in range(bk // SUBLANE):
            s = sc_scr[kt, r * SUBLANE:(r + 1) * SUBLANE, :]
            mx = jnp.maximum(mx, s)
            mn = jnp.minimum(mn, jnp.where(s > NEG_INF, s, -NEG_INF))
        return mx, mn

    init = (jnp.full((SUBLANE, bq), NEG_INF, F32), jnp.full((SUBLANE, bq), -NEG_INF, F32))
    mx, mn = lax.fori_loop(0, n_tiles, body, init)
    return jnp.max(mx, axis=0, keepdims=True), jnp.min(mn, axis=0, keepdims=True)


BISECT_WARMUP = 12
BISECT_CAP = 28


def _threshold(count, sc_scr, n_tiles, topk, bq, bk, res_f, res_i):
    zero = jnp.zeros((1, bq), F32)
    n_pos = count(zero, True)
    n_nn = count(zero, False)
    n_fin = count(jnp.full((1, bq), NEG_INF, F32), True)
    mx, mn = _max_min_cols(sc_scr, n_tiles, bq, bk)
    total = n_tiles * bk
    pos = n_pos >= topk
    at_zero = jnp.logical_not(pos) & (n_nn >= topk)
    none = n_fin < topk
    lo = jnp.where(pos | at_zero, zero, jnp.where(none, NEG_INF, mn))
    n_lo = jnp.where(pos | at_zero, n_nn, jnp.where(none, total, n_fin))
    hi = jnp.where(pos, mx * 2.0, zero)
    n_hi = jnp.where(pos, 0, jnp.where(at_zero, n_pos, n_nn))
    done = jnp.where(at_zero | none | (n_lo == topk), 1, 0)

    def movable(lo, hi, done):
        mid = lo + (hi - lo) * 0.5
        return (done == 0) & (mid > lo) & (mid < hi), mid

    def step(state):
        lo, hi, n_lo, n_hi, done = state
        ok, mid = movable(lo, hi, done)
        cnt = count(mid, False)
        up = ok & (cnt >= topk)
        down = ok & (cnt < topk)
        lo = jnp.where(up, mid, lo)
        n_lo = jnp.where(up, cnt, n_lo)
        hi = jnp.where(down, mid, hi)
        n_hi = jnp.where(down, cnt, n_hi)
        return lo, hi, n_lo, n_hi, jnp.where(n_lo == topk, 1, done)

    def any_movable(state):
        return jnp.max(jnp.where(movable(state[0], state[1], state[4])[0], 1, 0))

    state = lax.fori_loop(0, BISECT_WARMUP, lambda i, s: step(s), (lo, hi, n_lo, n_hi, done))

    def cond(c):
        return (c[0] < BISECT_CAP) & (c[1] > 0)

    def body(c):
        s = step(step(c[2]))
        return c[0] + 2, any_movable(s), s

    _, _, state = lax.while_loop(cond, body, (jnp.int32(BISECT_WARMUP), any_movable(state), state))
    lo, hi, n_lo, n_hi, done = state
    res_f[0:1, :] = lo
    res_i[0:1, :] = n_lo
    res_i[1:2, :] = n_hi

    @pl.when(jnp.min(done) == 0)
    def _():
        t, n_ge, n_gt = _kth_largest(count, total, topk, (1, bq))
        res_f[0:1, :] = t
        res_i[0:1, :] = n_ge
        res_i[1:2, :] = n_gt

    return res_f[0:1, :], res_i[0:1, :], res_i[1:2, :]


def _dsa_prompt_t_kernel(qt_ref, iqt_ref, g2t_ref, ik_ref, lower_ref, k_ref, vt_hbm,
                         o_ref, sc_scr, vbuf, sem, acc_scr, p_scr, res_f, res_i,
                         *, bq, bk, topk):
    qb = pl.program_id(0)
    q0 = qb * bq
    n_tiles = (q0 + bq + bk - 1) // bk
    sub = LANE

    def tile_scores(kt, masked):
        for s in range(bk // sub):
            ik_s = ik_ref[kt, s * sub:(s + 1) * sub, 0:IDX_DIM]
            acc = None
            for h in range(IDX_HEADS):
                y = _dot(ik_s, iqt_ref[h * IDX_DIM:(h + 1) * IDX_DIM, :])
                w = g2t_ref[GLA_GATE_RANK + h:GLA_GATE_RANK + h + 1, :] * IDX_SCALE
                t = jnp.maximum(y, 0.0) * w
                acc = t if acc is None else acc + t
            if masked:
                kpos = kt * bk + s * sub + lax.broadcasted_iota(I32, (sub, bq), 0)
                qpos = q0 + lax.broadcasted_iota(I32, (sub, bq), 1)
                shift = CHUNK.bit_length() - 1
                acc = jnp.where((kpos >> shift) <= (qpos >> shift), acc, NEG_INF)
            sc_scr[kt, s * sub:(s + 1) * sub, :] = acc

    def score_body(kt, carry):
        tile_scores(kt, False)
        return carry

    lax.fori_loop(0, n_tiles - 1, score_body, 0)
    tile_scores(n_tiles - 1, True)

    count = lambda cand, strict: _count_cols(sc_scr, n_tiles, cand, strict, bq, bk)
    t, n_ge, n_gt = _threshold(count, sc_scr, n_tiles, topk, bq, bk, res_f, res_i)
    def v_copy(kt):
        slot = lax.rem(kt, V_RING)
        return pltpu.make_async_copy(vt_hbm.at[kt], vbuf.at[slot], sem.at[slot])

    def v_prefetch(kt):
        @pl.when(kt < n_tiles)
        def _():
            v_copy(kt).start()

    for j in range(V_AHEAD):
        v_prefetch(j)

    tied = (n_ge > topk) & (t > NEG_INF)

    @pl.when(jnp.max(jnp.where(tied, 1, 0)) > 0)
    def _():
        need = (topk - n_gt).astype(F32)

        def drop_body(kt, eq_before):
            s = sc_scr[kt]
            eq = (s == t) & tied
            eq_b = jnp.where(eq, 1.0, 0.0).astype(BF16)
            rank = eq_before + _dot(lower_ref[...], eq_b)
            sc_scr[kt] = jnp.where(eq & (rank >= need), NEG_INF, s)
            return eq_before + jnp.sum(eq_b.astype(F32), axis=0, keepdims=True)

        lax.fori_loop(0, n_tiles, drop_body, jnp.zeros((1, bq), F32))

    thr = jnp.maximum(t, F32_LOWEST)

    def logits(kt, h):
        hs = slice(h * DSA_DH, (h + 1) * DSA_DH)
        return jnp.where(sc_scr[kt] >= thr, _dot(k_ref[kt, :, hs], qt_ref[hs, :]), NEG_BIG)

    def max_body(kt, ms):
        return tuple(jnp.maximum(ms[h], jnp.max(logits(kt, h), axis=0, keepdims=True))
                     for h in range(DSA_HEADS))

    ms = lax.fori_loop(0, n_tiles, max_body, (jnp.full((1, bq), NEG_BIG, F32),) * DSA_HEADS)

    acc_scr[...] = jnp.zeros(acc_scr.shape, F32)

    def sum_body(kt, ls):
        v_prefetch(kt + V_AHEAD)
        new_ls = []
        for h in range(DSA_HEADS):
            p = jnp.exp2(logits(kt, h) - ms[h])
            new_ls.append(ls[h] + jnp.sum(p, axis=0, keepdims=True))
            p_scr[h] = p.astype(BF16)
        v_copy(kt).wait()
        slot = lax.rem(kt, V_RING)
        for h in range(DSA_HEADS):
            hs = slice(h * DSA_DH, (h + 1) * DSA_DH)
            acc_scr[hs, :] += _dot(vbuf[slot, hs, :], p_scr[h])
        return tuple(new_ls)

    ls = lax.fori_loop(0, n_tiles, sum_body, (jnp.zeros((1, bq), F32),) * DSA_HEADS)
    for h in range(DSA_HEADS):
        hs = slice(h * DSA_DH, (h + 1) * DSA_DH)
        acc_scr[hs, :] = acc_scr[hs, :] / ls[h]
    o_ref[...] = jnp.transpose(acc_scr[...]).astype(o_ref.dtype)


def _dsa_prompt_t(qt, iqt, g2t, ikb, kb, vt3, bq, bk):
    t_len = qt.shape[1]
    n_kt = t_len // bk
    assert vt3.shape == (n_kt, DSA_WIDTH, bk) and bk % bq == 0
    topk = min(IDX_TOPK_MAX, t_len // 4)
    r = jnp.arange(bk)
    lower = (r[None, :] < r[:, None]).astype(BF16)
    kern = functools.partial(_dsa_prompt_t_kernel, bq=bq, bk=bk, topk=topk)
    col = lambda h: pl.BlockSpec((h, bq), lambda i: (0, i))
    ik3 = ikb.reshape(n_kt, bk, LANE)
    k3 = kb.reshape(n_kt, bk, DSA_WIDTH)
    return pl.pallas_call(
        kern,
        grid=(t_len // bq,),
        in_specs=[col(DSA_WIDTH), col(IDX_HEADS * IDX_DIM), col(LANE),
                  pl.BlockSpec(ik3.shape, lambda i: (0, 0, 0)),
                  pl.BlockSpec(lower.shape, lambda i: (0, 0)),
                  pl.BlockSpec(k3.shape, lambda i: (0, 0, 0)),
                  pl.BlockSpec(memory_space=pl.ANY)],
        out_specs=pl.BlockSpec((bq, DSA_WIDTH), lambda i: (i, 0)),
        out_shape=jax.ShapeDtypeStruct((t_len, DSA_WIDTH), BF16),
        scratch_shapes=[
            pltpu.VMEM((n_kt, bk, bq), F32),
            pltpu.VMEM((V_RING, DSA_WIDTH, bk), BF16),
            pltpu.SemaphoreType.DMA((V_RING,)),
            pltpu.VMEM((DSA_WIDTH, bq), F32),
            pltpu.VMEM((DSA_HEADS, bk, bq), BF16),
            pltpu.VMEM((SUBLANE, bq), F32),
            pltpu.VMEM((SUBLANE, bq), I32),
        ],
        compiler_params=_cparams(("arbitrary",)),
        name="dsa_prompt_t",
    )(qt, iqt, g2t, ik3, lower, k3, vt3)


def _dsa_sample_kernel(q_ref, iq_ref, g2_ref, ck_ref, cv_ref, cilo_ref, cihi_ref,
                       nk_ref, nv_ref, nilo_ref, nihi_ref, upper_ref, o_ref,
                       sc_scr, wb_scr, m_scr, l_scr, acc_scr, pk_scr, pv_scr, plo_scr, phi_scr,
                       *, bq, bk, past, topk):
    n_cache = past // bk
    sub = bk // LANE
    _fill_head_weights(wb_scr, g2_ref, bq)
    pk_scr[...] = jnp.zeros(pk_scr.shape, BF16)
    pv_scr[...] = jnp.zeros(pv_scr.shape, BF16)
    plo_scr[...] = jnp.zeros(plo_scr.shape, BF16)
    phi_scr[...] = jnp.zeros(phi_scr.shape, BF16)
    pk_scr[0:bq, :] = nk_ref[...]
    pv_scr[0:bq, :] = nv_ref[...]
    plo_scr[0:bq, :] = nilo_ref[...]
    phi_scr[0:bq, :] = nihi_ref[...]

    for kt in range(n_cache):
        s = _index_scores(iq_ref, wb_scr, cilo_ref[0, kt * bk:(kt + 1) * bk, :],
                          cihi_ref[0, kt * bk:(kt + 1) * bk, :], bk)
        for c in range(sub):
            sc_scr[kt * sub + c] = s[:, c * LANE:(c + 1) * LANE]
    s = _index_scores(iq_ref, wb_scr, plo_scr[...], phi_scr[...], LANE)
    real = lax.broadcasted_iota(I32, (bq, LANE), 1) < bq
    sc_scr[n_cache * sub] = jnp.where(real, s, NEG_INF)

    n_sub = n_cache * sub + 1
    count = lambda cand, strict: _count_rows(lambda i: sc_scr[i], n_sub, cand, strict, bq, LANE)
    t, n_ge, n_gt = _kth_largest(count, n_sub * LANE, topk, (bq, 1))
    tie_rows = jnp.where((n_ge > topk) & (t > NEG_INF), 1, 0)
    any_tie = jnp.max(tie_rows) > 0

    def attend(ties):
        _init_softmax(m_scr, l_scr, acc_scr)
        eq_before = jnp.zeros((bq, 1), F32)
        for kt in range(n_cache):
            keys = jnp.concatenate([sc_scr[kt * sub + c] for c in range(sub)], axis=1)
            upper = upper_ref[...] if ties else None
            sel, eq_before = _select(keys, t, n_gt, eq_before, topk, upper, ties)
            kv_tile = lambda h, kt=kt: (
                ck_ref[0, pl.ds(kt * bk * DSA_HEADS + h, bk, stride=DSA_HEADS), :].astype(BF16),
                cv_ref[0, pl.ds(kt * bk * DSA_HEADS + h, bk, stride=DSA_HEADS), :].astype(BF16))
            _attend_tile(sel, q_ref, kv_tile, m_scr, l_scr, acc_scr)
        upper = upper_ref[0:LANE, 0:LANE] if ties else None
        sel, eq_before = _select(sc_scr[n_cache * sub], t, n_gt, eq_before, topk, upper, ties)
        kv_tile = lambda h: (pk_scr[:, h * DSA_DH:(h + 1) * DSA_DH],
                             pv_scr[:, h * DSA_DH:(h + 1) * DSA_DH])
        _attend_tile(sel, q_ref, kv_tile, m_scr, l_scr, acc_scr)
        _finish_softmax(o_ref, l_scr, acc_scr)

    @pl.when(jnp.logical_not(any_tie))
    def _():
        attend(False)

    @pl.when(any_tie)
    def _():
        attend(True)


def _dsa_sample(q, iq, g2, cache_k, cache_v, cache_ilo, cache_ihi, kb, vb, iklo, ikhi, bq, bk):
    n_streams, past = cache_k.shape[0:2]
    cache_k = cache_k.reshape(n_streams, past * DSA_HEADS, DSA_DH)
    cache_v = cache_v.reshape(n_streams, past * DSA_HEADS, DSA_DH)
    assert past % CHUNK == 0 and bq <= CHUNK and past % bk == 0
    topk = min(IDX_TOPK_MAX, (past + bq) // 4)
    upper = _upper_ones(bk)
    kern = functools.partial(_dsa_sample_kernel, bq=bq, bk=bk, past=past, topk=topk)
    row = lambda w: pl.BlockSpec((bq, w), lambda s: (s, 0))
    cache = lambda w: pl.BlockSpec((1, past, w), lambda s: (s, 0, 0))
    cache_kv = pl.BlockSpec((1, past * DSA_HEADS, DSA_DH), lambda s: (s, 0, 0))
    n_sub = past // LANE + 1
    return pl.pallas_call(
        kern,
        grid=(n_streams,),
        in_specs=[row(DSA_WIDTH), row(IDX_HEADS * IDX_DIM), row(LANE),
                  cache_kv, cache_kv, cache(LANE), cache(LANE),
                  row(DSA_WIDTH), row(DSA_WIDTH), row(LANE), row(LANE),
                  pl.BlockSpec(upper.shape, lambda s: (0, 0))],
        out_specs=row(DSA_WIDTH),
        out_shape=jax.ShapeDtypeStruct((n_streams * bq, DSA_WIDTH), BF16),
        scratch_shapes=[
            pltpu.VMEM((n_sub, bq, LANE), F32),
            pltpu.VMEM((IDX_HEADS, bq, LANE), F32),
            pltpu.VMEM((DSA_HEADS, bq, 1), F32),
            pltpu.VMEM((DSA_HEADS, bq, 1), F32),
            pltpu.VMEM((bq, DSA_WIDTH), F32),
            pltpu.VMEM((LANE, DSA_WIDTH), BF16),
            pltpu.VMEM((LANE, DSA_WIDTH), BF16),
            pltpu.VMEM((LANE, LANE), BF16),
            pltpu.VMEM((LANE, LANE), BF16),
        ],
        compiler_params=_cparams(("parallel",)),
        name="dsa_sample",
    )(q, iq, g2, cache_k, cache_v, cache_ilo, cache_ihi, kb, vb, iklo, ikhi, upper)


def _mix_kernel(og_ref, gg_ref, od_ref, x_ref, gnw_ref, wo_ref, nfw_ref, rw_ref, rb_ref,
                h1_ref, xn_ref, re_ref, rg_ref, cnt_ref):
    g = gg_ref[...].astype(F32)
    gate = g / (1.0 + jnp.exp(-g))
    acc = _dot(od_ref[...], wo_ref[GLA_WIDTH:GLA_WIDTH + DSA_WIDTH, :])
    for h in range(GLA_HEADS):
        hs = slice(h * GLA_DV, (h + 1) * GLA_DV)
        oh = og_ref[:, hs]
        ms = jnp.mean(oh * oh, axis=-1, keepdims=True)
        a = oh * lax.rsqrt(ms + EPS) * gnw_ref[...] * gate[:, hs]
        acc = acc + _dot(a.astype(BF16), wo_ref[hs, :])
    h1 = x_ref[...] + acc
    h1_ref[...] = h1
    ms = jnp.mean(h1 * h1, axis=-1, keepdims=True)
    xn = h1 * lax.rsqrt(ms + EPS) * nfw_ref[...]
    _rows_to_tiles(xn_ref, xn)
    xn_hi = xn.astype(BF16)
    xn_lo = (xn - xn_hi.astype(F32)).astype(BF16)
    logits = (_dot(xn_hi, rw_ref[0]) + _dot(xn_hi, rw_ref[1]) + _dot(xn_lo, rw_ref[0])
              + rb_ref[...])
    lane = lax.broadcasted_iota(I32, logits.shape, 1)
    lane_f = lane.astype(F32)
    work = logits
    e_out = jnp.zeros(logits.shape, I32)
    tops, hots = [], []
    for k in range(TOP_K):
        mx = jnp.max(work, axis=1, keepdims=True)
        idx = jnp.min(jnp.where(work == mx, lane_f, float(LANE)), axis=1, keepdims=True)
        idx = idx.astype(I32)
        hot = lane == idx
        e_out = jnp.where(lane == k, idx, e_out)
        tops.append(mx)
        hots.append(hot)
        work = jnp.where(hot, NEG_BIG, work)
    ex = [jnp.exp(v - tops[0]) for v in tops]
    den = ex[0] + ex[1] + ex[2] + ex[3]
    g_out = jnp.zeros(logits.shape, F32)
    for k in range(TOP_K):
        g_out = jnp.where(lane == k, ex[k] / den, g_out)
    rg_ref[...] = g_out
    bm = logits.shape[0]
    earlier = (lax.broadcasted_iota(I32, (bm, bm), 1) < lax.broadcasted_iota(I32, (bm, bm), 0))
    earlier = jnp.where(earlier, 1.0, 0.0).astype(BF16)
    run = jnp.zeros((1, LANE), F32)
    for k in range(TOP_K):
        hot_f = jnp.where(hots[k], 1.0, 0.0)
        before = _dot(earlier, hot_f.astype(BF16)) + run
        rank = jnp.sum(jnp.where(hots[k], before, 0.0), axis=1, keepdims=True).astype(I32)
        e_out = jnp.where(lane == TOP_K + k, rank, e_out)
        run = run + jnp.sum(hot_f, axis=0, keepdims=True)
    re_ref[...] = e_out
    cnt_ref[0] = jnp.broadcast_to(run, (SUBLANE, LANE)).astype(I32)


def _mix_consts(gla_norm_w, w_out, norm_ffn_w, router_w, router_b):
    rw = jnp.zeros((D_MODEL, LANE), F32).at[:, 0:N_EXPERTS].set(router_w)
    rw_hi = rw.astype(BF16)
    rw_lo = (rw - rw_hi.astype(F32)).astype(BF16)
    rb = jnp.full((1, LANE), NEG_BIG, F32).at[0, 0:N_EXPERTS].set(router_b)
    return (gla_norm_w.reshape(1, GLA_DV), w_out.astype(BF16), norm_ffn_w.reshape(1, D_MODEL),
            jnp.stack([rw_hi, rw_lo]), rb)


def _mix(og, gla, od, x2d, consts, bm):
    n = x2d.shape[0]
    gnw, wo, nfw, rws, rb = consts
    row = lambda w: pl.BlockSpec((bm, w), lambda i: (i, 0))
    const = lambda a: pl.BlockSpec(a.shape, lambda i: (0,) * a.ndim)
    gg_col = (2 * GLA_QK + GLA_WIDTH) // GLA_WIDTH
    return pl.pallas_call(
        _mix_kernel,
        grid=(n // bm,),
        in_specs=[row(GLA_WIDTH), pl.BlockSpec((bm, GLA_WIDTH), lambda i: (i, gg_col)),
                  row(DSA_WIDTH), row(D_MODEL),
                  const(gnw), const(wo), const(nfw), const(rws), const(rb)],
        out_specs=[row(D_MODEL), pl.BlockSpec((bm * ROW_TILE, LANE), lambda i: (i, 0)),
                   row(LANE), row(LANE), pl.BlockSpec((1, SUBLANE, LANE), lambda i: (i, 0, 0))],
        out_shape=[jax.ShapeDtypeStruct((n, D_MODEL), F32),
                   jax.ShapeDtypeStruct((n * ROW_TILE, LANE), F32),
                   jax.ShapeDtypeStruct((n, LANE), I32), jax.ShapeDtypeStruct((n, LANE), F32),
                   jax.ShapeDtypeStruct((n // bm, SUBLANE, LANE), I32)],
        compiler_params=_cparams(("parallel",)),
        name="mix",
    )(og, gla, od, x2d, gnw, wo, nfw, rws, rb)


MOE_ROWS = 256
DMA_UNROLL = 8
TOKEN_BLOCK = 512


def _route_tables(counts, br, n_pairs):
    c = counts[:, 0, :]
    tot = jnp.sum(c, axis=0)
    padded = (tot + br - 1) // br * br
    pad_end = jnp.cumsum(padded)
    pad_start = pad_end - padded
    base = pad_start[None, :] + jnp.cumsum(c, axis=0) - c
    base = jnp.broadcast_to(base[:, None, :], counts.shape).astype(I32)
    nb = -(-(n_pairs + N_EXPERTS * (br - 1)) // br)
    n_used = (pad_end[N_EXPERTS - 1] // br).astype(I32)
    blk = jnp.arange(nb, dtype=I32)
    last = jnp.minimum(blk, n_used - 1) * br
    block_e = jnp.sum((pad_end[None, 0:N_EXPERTS] <= last[:, None]).astype(I32), axis=1)
    ends = pad_end[0:N_EXPERTS].astype(I32)
    tots = jnp.concatenate([tot[0:N_EXPERTS].astype(I32), n_used.reshape(1)])
    return base, jnp.minimum(block_e, N_EXPERTS - 1), n_used.reshape(1), ends, tots


def _pos_kernel(re_ref, base_ref, pos_ref):
    re = re_ref[...]
    lane = lax.broadcasted_iota(I32, re.shape, 1)
    base_row = base_ref[0, 0:1, :].astype(F32)
    p = jnp.zeros(re.shape, F32)
    for k in range(TOP_K):
        hot = lane == re[:, k:k + 1]
        first = jnp.sum(jnp.where(hot, base_row, 0.0), axis=1, keepdims=True)
        p = jnp.where(lane == k, first + re[:, TOP_K + k:TOP_K + k + 1].astype(F32), p)
    pos_ref[0] = jnp.transpose(p)[0:SUBLANE, :].astype(I32) * ROW_TILE


def _positions(re, base, bm):
    n = re.shape[0]
    return pl.pallas_call(
        _pos_kernel,
        grid=(n // bm,),
        in_specs=[pl.BlockSpec((bm, LANE), lambda i: (i, 0)),
                  pl.BlockSpec((1, SUBLANE, LANE), lambda i: (i, 0, 0))],
        out_specs=pl.BlockSpec((1, SUBLANE, bm), lambda i: (i, 0, 0)),
        out_shape=jax.ShapeDtypeStruct((n // bm, SUBLANE, bm), I32),
        compiler_params=_cparams(("parallel",)),
        name="positions",
    )(re, base)


def _dispatch_kernel(end_ref, tot_ref, pos_ref, xa_ref, xb_ref, xr_hbm, zero_scr, sem,
                     *, bt, br, nsteps_a):
    i = pl.program_id(0)

    @pl.when(i == 0)
    def _():
        zero_scr[...] = jnp.zeros(zero_scr.shape, F32)
        n_used = tot_ref[N_EXPERTS]
        nb = xr_hbm.shape[0] // (br * ROW_TILE)

        def fills():
            for e in range(N_EXPERTS):
                yield tot_ref[e] > 0, end_ref[e] - br
            for j in range(N_EXPERTS):
                yield n_used + j < nb, jnp.minimum(n_used + j, nb - 1) * br

        for act in ("start", "wait"):
            for cond, row0 in fills():
                @pl.when(cond)
                def _(row0=row0, act=act):
                    dst = xr_hbm.at[pl.ds(pl.multiple_of(row0 * ROW_TILE, ROW_TILE), br * ROW_TILE)]
                    c = pltpu.make_async_copy(zero_scr, dst, sem.at[1])
                    c.start() if act == "start" else c.wait()

    def scatter(x_ref):
        for k in range(TOP_K):
            def body(r, c, k=k):
                src = x_ref.at[pl.ds(pl.multiple_of(r * ROW_TILE, ROW_TILE), ROW_TILE)]
                dst = xr_hbm.at[pl.ds(pl.multiple_of(pos_ref[0, k, r], ROW_TILE), ROW_TILE)]
                pltpu.make_async_copy(src, dst, sem.at[0]).start()
                return c
            lax.fori_loop(0, bt, body, 0, unroll=DMA_UNROLL)

    @pl.when(i < nsteps_a)
    def _():
        scatter(xa_ref)

    @pl.when(i >= nsteps_a)
    def _():
        scatter(xb_ref)

    done = xr_hbm.at[pl.ds(0, TOP_K * bt * ROW_TILE)]
    pltpu.make_async_copy(done, done, sem.at[0]).wait()


def _dispatch(xn_a, xn_b, pos_t, pad_end, tot, n_rows, bt, br):
    na, nb_ = xn_a.shape[0] // (bt * ROW_TILE), xn_b.shape[0] // (bt * ROW_TILE)
    kern = functools.partial(_dispatch_kernel, bt=bt, br=br, nsteps_a=na)
    grid_spec = pltpu.PrefetchScalarGridSpec(
        num_scalar_prefetch=2,
        grid=(na + nb_,),
        in_specs=[
            pl.BlockSpec((1, SUBLANE, bt), lambda i, e, t: (i, 0, 0), memory_space=pltpu.SMEM),
            pl.BlockSpec((bt * ROW_TILE, LANE), lambda i, e, t: (jnp.minimum(i, na - 1), 0)),
            pl.BlockSpec((bt * ROW_TILE, LANE), lambda i, e, t: (jnp.maximum(i - na, 0), 0)),
        ],
        out_specs=pl.BlockSpec(memory_space=pl.ANY),
        scratch_shapes=[pltpu.VMEM((br * ROW_TILE, LANE), F32), pltpu.SemaphoreType.DMA((2,))],
    )
    return pl.pallas_call(
        kern,
        grid_spec=grid_spec,
        out_shape=jax.ShapeDtypeStruct((n_rows * ROW_TILE, LANE), F32),
        compiler_params=_cparams(("arbitrary",)),
        name="dispatch",
    )(pad_end, tot, pos_t, xn_a, xn_b)


def _rows_to_tiles(ref, x):
    n = x.shape[0]
    for j in range(ROW_TILE):
        ref[pl.ds(j, n, stride=ROW_TILE), :] = x[:, j * LANE:(j + 1) * LANE]


def _tile_chunk(ref, j, n, row0=0):
    return ref[pl.ds(row0 * ROW_TILE + j, n, stride=ROW_TILE), :]


def _rows_from_tiles(ref, n):
    return jnp.concatenate([_tile_chunk(ref, j, n) for j in range(ROW_TILE)], axis=1)


def _moe_kernel(be_ref, nu_ref, x_ref, wgu_ref, bgu_ref, wd_ref, bdn_ref,
                y_ref, wgu_bf, wd_bf):
    i = pl.program_id(0)
    n_used = nu_ref[0]

    @pl.when(i < n_used)
    def _():
        e = be_ref[i]
        prev = be_ref[jnp.maximum(i - 1, 0)]

        @pl.when((i == 0) | (e != prev))
        def _():
            wgu_bf[...] = wgu_ref[0].astype(BF16)
            wd_bf[...] = wd_ref[0].astype(BF16)

        xb = _rows_from_tiles(x_ref, x_ref.shape[0] // ROW_TILE).astype(BF16)
        gu = _dot(xb, wgu_bf[...]) + bgu_ref[0]
        g_lin = jnp.minimum(gu[:, 0:D_FF], SWIGLU_LIMIT)
        u_lin = jnp.clip(gu[:, D_FF:2 * D_FF], -SWIGLU_LIMIT, SWIGLU_LIMIT)
        act = g_lin / (1.0 + jnp.exp(-SWIGLU_ALPHA * g_lin)) * (u_lin + 1.0)
        y = _dot(act.astype(BF16), wd_bf[...]) + bdn_ref[0]
        _rows_to_tiles(y_ref, y)

    @pl.when(i >= n_used)
    def _():
        y_ref[...] = jnp.zeros(y_ref.shape, y_ref.dtype)


def _moe(x_rows, block_e, n_used, w_gu, b_gu, w_down, b_down, br):
    nb = x_rows.shape[0] // (br * ROW_TILE)
    grid_spec = pltpu.PrefetchScalarGridSpec(
        num_scalar_prefetch=2,
        grid=(nb,),
        in_specs=[
            pl.BlockSpec((br * ROW_TILE, LANE),
                         lambda i, be, nu: (jnp.minimum(i, nu[0] - 1), 0)),
            pl.BlockSpec((1, D_MODEL, 2 * D_FF), lambda i, be, nu: (be[i], 0, 0)),
            pl.BlockSpec((1, 1, 2 * D_FF), lambda i, be, nu: (be[i], 0, 0)),
            pl.BlockSpec((1, D_FF, D_MODEL), lambda i, be, nu: (be[i], 0, 0)),
            pl.BlockSpec((1, 1, D_MODEL), lambda i, be, nu: (be[i], 0, 0)),
        ],
        out_specs=pl.BlockSpec((br * ROW_TILE, LANE), lambda i, be, nu: (i, 0)),
        scratch_shapes=[
            pltpu.VMEM((D_MODEL, 2 * D_FF), BF16),
            pltpu.VMEM((D_FF, D_MODEL), BF16),
        ],
    )
    return pl.pallas_call(
        _moe_kernel,
        grid_spec=grid_spec,
        out_shape=jax.ShapeDtypeStruct((nb * br * ROW_TILE, LANE), F32),
        compiler_params=_cparams(("arbitrary",)),
        name="moe",
    )(block_e, n_used, x_rows, w_gu, b_gu.reshape(N_EXPERTS, 1, 2 * D_FF),
      w_down, b_down.reshape(N_EXPERTS, 1, D_MODEL))


def _combine_kernel(pos_ref, posn_ref, y_hbm, h1_ref, rg_ref, fw_ref, o_ref, yg, sem, *, bt):
    i = pl.program_id(0)
    nsteps = pl.num_programs(0)
    slot = lax.rem(i, 2)

    def issue(pref, s):
        for k in range(TOP_K):
            def body(r, c, k=k):
                src = y_hbm.at[pl.ds(pl.multiple_of(pref[0, k, r], ROW_TILE), ROW_TILE)]
                row0 = pl.multiple_of((k * bt + r) * ROW_TILE, ROW_TILE)
                pltpu.make_async_copy(src, yg.at[s, pl.ds(row0, ROW_TILE)], sem.at[s]).start()
                return c
            lax.fori_loop(0, bt, body, 0, unroll=DMA_UNROLL)

    @pl.when(i == 0)
    def _():
        issue(pos_ref, 0)

    @pl.when(i + 1 < nsteps)
    def _():
        issue(posn_ref, 1 - slot)

    pltpu.make_async_copy(yg.at[slot], yg.at[slot], sem.at[slot]).wait()
    gates = [jnp.broadcast_to(rg_ref[:, k:k + 1], (bt, LANE)) for k in range(TOP_K)]
    ss = jnp.zeros((bt, LANE), F32)
    rows = yg.at[slot]
    for j in range(ROW_TILE):
        js = slice(j * LANE, (j + 1) * LANE)
        a = h1_ref[:, js]
        for k in range(TOP_K):
            a = a + gates[k] * _tile_chunk(rows, j, bt, k * bt)
        ss = ss + a * a
        o_ref[:, js] = a
    ms = jnp.sum(ss, axis=-1, keepdims=True) * (1.0 / D_MODEL)
    o_ref[...] = o_ref[...] * lax.rsqrt(ms + EPS) * fw_ref[...]


def _combine(y_rows, pos_t, h1, rg, final_w, bt):
    n = h1.shape[0]
    nsteps = n // bt
    kern = functools.partial(_combine_kernel, bt=bt)
    row = lambda w: pl.BlockSpec((bt, w), lambda i: (i, 0))
    return pl.pallas_call(
        kern,
        grid=(nsteps,),
        in_specs=[
            pl.BlockSpec((1, SUBLANE, bt), lambda i: (i, 0, 0), memory_space=pltpu.SMEM),
            pl.BlockSpec((1, SUBLANE, bt), lambda i: (jnp.minimum(i + 1, nsteps - 1), 0, 0),
                         memory_space=pltpu.SMEM),
            pl.BlockSpec(memory_space=pl.ANY),
            row(D_MODEL), row(LANE),
            pl.BlockSpec((1, D_MODEL), lambda i: (0, 0)),
        ],
        out_specs=row(D_MODEL),
        out_shape=jax.ShapeDtypeStruct((n, D_MODEL), F32),
        scratch_shapes=[pltpu.VMEM((2, TOP_K * bt * ROW_TILE, LANE), F32),
                        pltpu.SemaphoreType.DMA((2,))],
        compiler_params=_cparams(("arbitrary",)),
        name="combine",
    )(pos_t, pos_t, y_rows, h1, rg, final_w.reshape(1, D_MODEL))


def kernel(x_prompt, x_sample, cache_k, cache_v, cache_k_idx, state_gla, norm_mix_w, w_in,
           w_gla_a2, b_gla_a2, gla_norm_w, w_out, norm_ffn_w, router_w, router_b, w_gu, b_gu,
           w_down, b_down, norm_final_w):
    b_p, t_p, _ = x_prompt.shape
    b_s, t_s, _ = x_sample.shape
    past = cache_k.shape[2]
    assert b_p == 1 and norm_mix_w.shape[0] == 1
    l = 0
    pw = _proj_weights(norm_mix_w[l], w_in[l])
    gconsts = _gla_consts(w_gla_a2[l], b_gla_a2[l])

    pp = _project_t(x_prompt.reshape(t_p, D_MODEL), _proj_t_weights(norm_mix_w[l], w_in[l]), DSA_BK)
    gla_p, kf_p, vf_p, kb_p, ikf_p, ikb_p, g2_p, qt_p, iqt_p, vt3_p, g2t_p = pp
    s0 = jnp.zeros((1, GLA_WIDTH, GLA_QK), F32)
    og_p, st_p = _gla(gla_p, g2_p, s0, gconsts, 1, t_p, 512, 16)
    od_p = _dsa_prompt_t(qt_p, iqt_p, g2t_p, ikb_p, kb_p, vt3_p, DSA_BQ, DSA_BK)

    ps = _project(x_sample.reshape(b_s * t_s, D_MODEL), pw, 512)
    gla_s, q_s, kf_s, vf_s, kb_s, vb_s, iq_s, iklo_s, ikhi_s, ikf_s, g2_s = ps
    og_s, st_s = _gla(gla_s, g2_s, _state_to_kernel(state_gla[l]), gconsts, b_s, t_s, t_s, 16)
    zc = jnp.zeros((b_s, past, IDX_DIM), BF16)
    cik = cache_k_idx[l].astype(BF16)
    cilo = jnp.concatenate([cik, zc], axis=2)
    cihi = jnp.concatenate([zc, cik], axis=2)
    od_s = _dsa_sample(q_s, iq_s, g2_s, cache_k[l], cache_v[l], cilo, cihi,
                       kb_s, vb_s, iklo_s, ikhi_s, t_s, 512)

    mconsts = _mix_consts(gla_norm_w[l], w_out[l], norm_ffn_w[l], router_w[l], router_b[l])
    n_s = b_s * t_s
    tb = TOKEN_BLOCK
    h1_p, xn_p, re_p, rg_p, cnt_p = _mix(og_p, gla_p, od_p, x_prompt.reshape(t_p, D_MODEL), mconsts, tb)
    h1_s, xn_s, re_s, rg_s, cnt_s = _mix(og_s, gla_s, od_s, x_sample.reshape(n_s, D_MODEL), mconsts, tb)
    n_pairs = (t_p + n_s) * TOP_K
    base, block_e, n_used, ends, tots = _route_tables(
        jnp.concatenate([cnt_p, cnt_s], axis=0), MOE_ROWS, n_pairs)
    n_rows = block_e.shape[0] * MOE_ROWS
    pos_p = _positions(re_p, base[0:t_p // tb], tb)
    pos_s = _positions(re_s, base[t_p // tb:], tb)
    x_rows = _dispatch(xn_p, xn_s, jnp.concatenate([pos_p, pos_s], axis=0), ends, tots,
                       n_rows, tb, MOE_ROWS)
    y_rows = _moe(x_rows, block_e, n_used, w_gu[l], b_gu[l], w_down[l], b_down[l], MOE_ROWS)
    y_p = _combine(y_rows, pos_p, h1_p, rg_p, norm_final_w, tb)
    y_s = _combine(y_rows, pos_s, h1_s, rg_s, norm_final_w, tb)
    y_p = y_p.reshape(x_prompt.shape)
    y_s = y_s.reshape(x_sample.shape)
    return (y_p, y_s,
            kf_p.reshape(1, 1, t_p, DSA_HEADS, DSA_DH), vf_p.reshape(1, 1, t_p, DSA_HEADS, DSA_DH),
            ikf_p.reshape(1, 1, t_p, IDX_DIM), _state_from_kernel(st_p)[None],
            kf_s.reshape(1, b_s, t_s, DSA_HEADS, DSA_DH), vf_s.reshape(1, b_s, t_s, DSA_HEADS, DSA_DH),
            ikf_s.reshape(1, b_s, t_s, IDX_DIM), _state_from_kernel(st_s)[None])
```

```python
import functools

import jax
import jax.numpy as jnp
from jax import lax
from jax.experimental import pallas as pl
from jax.experimental.pallas import tpu as pltpu

F32 = jnp.float32
BF16 = jnp.bfloat16
I32 = jnp.int32

D_MODEL = 1024
CHUNK = 64
GLA_HEADS = 4
GLA_DK = 64
GLA_DV = 128
GLA_QK = GLA_HEADS * GLA_DK
GLA_WIDTH = GLA_HEADS * GLA_DV
GLA_GATE_RANK = 16
GLA_GATE_TAU = 16.0
DSA_HEADS = 4
DSA_DH = 128
DSA_WIDTH = DSA_HEADS * DSA_DH
IDX_HEADS = 8
IDX_DIM = 64
IDX_TOPK_MAX = 256
N_EXPERTS = 32
TOP_K = 4
D_FF = 1024
SWIGLU_ALPHA = 1.702
SWIGLU_LIMIT = 7.0
EPS = 1e-6

_OFF_GLA = 0
_W_GLA = 2 * GLA_QK + 2 * GLA_WIDTH
_OFF_GA = _OFF_GLA + _W_GLA
_OFF_DSA = _OFF_GA + GLA_GATE_RANK
_W_DSA = 3 * DSA_WIDTH + IDX_HEADS * IDX_DIM
_OFF_IK = _OFF_DSA + _W_DSA
_OFF_IW = _OFF_IK + IDX_DIM
LANE = 128
SUBLANE = 8
ROW_TILE = D_MODEL // LANE
assert ROW_TILE == SUBLANE
LOG2E = 1.4426950408889634
NEG_BIG = -1e30
VMEM_LIMIT = 56 * 1024 * 1024


def _cparams(sem):
    return pltpu.CompilerParams(dimension_semantics=sem, vmem_limit_bytes=VMEM_LIMIT)


def _dot(a, b):
    return jnp.dot(a, b, preferred_element_type=F32)


def _dot_nt(a, b):
    return lax.dot_general(a, b, (((1,), (1,)), ((), ())), preferred_element_type=F32)


def _dot_tn(a, b):
    return lax.dot_general(a, b, (((0,), (0,)), ((), ())), preferred_element_type=F32)


def _split3(x):
    hi = x.astype(BF16)
    r1 = x - hi.astype(F32)
    mid = r1.astype(BF16)
    lo = (r1 - mid.astype(F32)).astype(BF16)
    return hi, mid, lo


def _proj_kernel(x_ref, nw_ref, wg_ref, wd_ref, ws_ref,
                 gla_ref, q_ref, kf_ref, vf_ref, kb_ref, vb_ref, iq_ref,
                 iklo_ref, ikhi_ref, ikf_ref, g2_ref):
    x = x_ref[...]
    ms = jnp.mean(x * x, axis=-1, keepdims=True)
    xn = (x * lax.rsqrt(ms + EPS) * nw_ref[...]).astype(BF16)
    gla_ref[...] = _dot(xn, wg_ref[...]).astype(BF16)
    W = DSA_WIDTH
    dq = _dot(xn, wd_ref[:, 0:W])
    q_ref[...] = (dq * (DSA_DH ** -0.5 * LOG2E)).astype(BF16)
    dk = _dot(xn, wd_ref[:, W:2 * W])
    kf_ref[...] = dk
    kb_ref[...] = dk.astype(BF16)
    dv = _dot(xn, wd_ref[:, 2 * W:3 * W])
    vf_ref[...] = dv
    vb_ref[...] = dv.astype(BF16)
    iq_ref[...] = _dot(xn, wd_ref[:, 3 * W:4 * W]).astype(BF16)
    sm = _dot(xn, ws_ref[...])
    iklo_ref[...] = sm[:, 0:LANE].astype(BF16)
    ikhi_ref[...] = sm[:, LANE:2 * LANE].astype(BF16)
    ikf_ref[...] = sm[:, 0:IDX_DIM]
    g2_ref[...] = sm[:, 2 * LANE:3 * LANE]


def _proj_weights(norm_w, w_in):
    wg = w_in[:, _OFF_GLA:_OFF_GLA + _W_GLA].astype(BF16)
    wd = w_in[:, _OFF_DSA:_OFF_DSA + _W_DSA].astype(BF16)
    ik = w_in[:, _OFF_IK:_OFF_IK + IDX_DIM]
    z64 = jnp.zeros((D_MODEL, IDX_DIM), F32)
    ga = w_in[:, _OFF_GA:_OFF_GA + GLA_GATE_RANK]
    iw = w_in[:, _OFF_IW:_OFF_IW + IDX_HEADS]
    zpad = jnp.zeros((D_MODEL, LANE - GLA_GATE_RANK - IDX_HEADS), F32)
    ws = jnp.concatenate([ik, z64, z64, ik, ga, iw, zpad], axis=1).astype(BF16)
    return norm_w.reshape(1, D_MODEL), wg, wd, ws


def _project(x2d, pw, bm):
    n = x2d.shape[0]
    nw, wg, wd, ws = pw
    row = lambda w: pl.BlockSpec((bm, w), lambda i: (i, 0))
    full = lambda a: pl.BlockSpec(a.shape, lambda i: (0, 0))
    outs = [(_W_GLA, BF16), (DSA_WIDTH, BF16), (DSA_WIDTH, F32), (DSA_WIDTH, F32),
            (DSA_WIDTH, BF16), (DSA_WIDTH, BF16), (IDX_HEADS * IDX_DIM, BF16),
            (LANE, BF16), (LANE, BF16), (IDX_DIM, F32), (LANE, F32)]
    return pl.pallas_call(
        _proj_kernel,
        grid=(n // bm,),
        in_specs=[row(D_MODEL), full(nw), full(wg), full(wd), full(ws)],
        out_specs=[row(w) for w, _ in outs],
        out_shape=[jax.ShapeDtypeStruct((n, w), dt) for w, dt in outs],
        compiler_params=_cparams(("parallel",)),
        name="proj",
    )(x2d, nw, wg, wd, ws)


def _proj_t_kernel(x_ref, nw_ref, wg_ref, wkv_ref, ws_ref, wt_ref, wst_ref,
                   gla_ref, kf_ref, vf_ref, kb_ref, ikf_ref, ikb_ref, g2_ref,
                   qt_ref, iqt_ref, vt_ref, g2t_ref):
    x = x_ref[...]
    ms = jnp.mean(x * x, axis=-1, keepdims=True)
    xn = (x * lax.rsqrt(ms + EPS) * nw_ref[...]).astype(BF16)
    gla_ref[...] = _dot(xn, wg_ref[...]).astype(BF16)
    W = DSA_WIDTH
    dk = _dot(xn, wkv_ref[:, 0:W])
    dv = _dot(xn, wkv_ref[:, W:2 * W])
    kb_ref[...] = dk.astype(BF16)
    for h in range(DSA_HEADS):
        kf_ref[:, h, :] = dk[:, h * DSA_DH:(h + 1) * DSA_DH]
        vf_ref[:, h, :] = dv[:, h * DSA_DH:(h + 1) * DSA_DH]
    sm = _dot(xn, ws_ref[...])
    ikb_ref[...] = sm[:, 0:LANE].astype(BF16)
    ikf_ref[...] = sm[:, 0:IDX_DIM]
    g2_ref[...] = sm[:, LANE:2 * LANE]
    qt_ref[...] = (_dot_nt(wt_ref[0:W, :], xn) * (DSA_DH ** -0.5 * LOG2E)).astype(BF16)
    vt_ref[0] = _dot_nt(wt_ref[W:2 * W, :], xn).astype(BF16)
    iqt_ref[...] = _dot_nt(wt_ref[2 * W:3 * W, :], xn).astype(BF16)
    g2t_ref[...] = _dot_nt(wst_ref[...], xn)


def _proj_t_weights(norm_w, w_in):
    wg = w_in[:, _OFF_GLA:_OFF_GLA + _W_GLA].astype(BF16)
    W = DSA_WIDTH
    wq = w_in[:, _OFF_DSA:_OFF_DSA + W]
    wkv = w_in[:, _OFF_DSA + W:_OFF_DSA + 3 * W]
    wv = w_in[:, _OFF_DSA + 2 * W:_OFF_DSA + 3 * W]
    wiq = w_in[:, _OFF_DSA + 3 * W:_OFF_DSA + 4 * W]
    ik = w_in[:, _OFF_IK:_OFF_IK + IDX_DIM]
    z64 = jnp.zeros((D_MODEL, IDX_DIM), F32)
    ga = w_in[:, _OFF_GA:_OFF_GA + GLA_GATE_RANK]
    iw = w_in[:, _OFF_IW:_OFF_IW + IDX_HEADS]
    zpad = jnp.zeros((D_MODEL, LANE - GLA_GATE_RANK - IDX_HEADS), F32)
    g2w = jnp.concatenate([ga, iw, zpad], axis=1)
    ws = jnp.concatenate([ik, z64, g2w], axis=1).astype(BF16)
    wt = jnp.concatenate([wq, wv, wiq], axis=1).T.astype(BF16)
    return norm_w.reshape(1, D_MODEL), wg, wkv.astype(BF16), ws, wt, g2w.T.astype(BF16)


def _project_t(x2d, pw, bm):
    n = x2d.shape[0]
    row = lambda w: pl.BlockSpec((bm, w), lambda i: (i, 0))
    row3 = pl.BlockSpec((bm, DSA_HEADS, DSA_DH), lambda i: (i, 0, 0))
    col = lambda h: pl.BlockSpec((h, bm), lambda i: (0, i))
    full = lambda a: pl.BlockSpec(a.shape, lambda i: (0, 0))
    out_specs = [row(_W_GLA), row3, row3, row(DSA_WIDTH), row(IDX_DIM), row(LANE), row(LANE),
                 col(DSA_WIDTH), col(IDX_HEADS * IDX_DIM),
                 pl.BlockSpec((1, DSA_WIDTH, bm), lambda i: (i, 0, 0)), col(LANE)]
    out_shape = [jax.ShapeDtypeStruct((n, _W_GLA), BF16),
                 jax.ShapeDtypeStruct((n, DSA_HEADS, DSA_DH), F32),
                 jax.ShapeDtypeStruct((n, DSA_HEADS, DSA_DH), F32),
                 jax.ShapeDtypeStruct((n, DSA_WIDTH), BF16),
                 jax.ShapeDtypeStruct((n, IDX_DIM), F32),
                 jax.ShapeDtypeStruct((n, LANE), BF16),
                 jax.ShapeDtypeStruct((n, LANE), F32),
                 jax.ShapeDtypeStruct((DSA_WIDTH, n), BF16),
                 jax.ShapeDtypeStruct((IDX_HEADS * IDX_DIM, n), BF16),
                 jax.ShapeDtypeStruct((n // bm, DSA_WIDTH, bm), BF16),
                 jax.ShapeDtypeStruct((LANE, n), F32)]
    return pl.pallas_call(
        _proj_t_kernel,
        grid=(n // bm,),
        in_specs=[row(D_MODEL)] + [full(a) for a in pw],
        out_specs=out_specs,
        out_shape=out_shape,
        compiler_params=_cparams(("parallel",)),
        name="proj_t",
    )(x2d, *pw)


def _log_sigmoid(x):
    return jnp.minimum(x, 0.0) - jnp.log1p(jnp.exp(-jnp.abs(x)))


def _gla_kernel(gla_ref, g2_ref, w2_ref, b2_ref, seg_ref, bd_ref, s0_ref,
                o_ref, sT_ref,
                st_scr, kpad, bpad, vpad, qt_scr, kt_scr, dec_scr, oi_scr,
                *, bt, c):
    t = pl.program_id(1)

    @pl.when(t == 0)
    def _():
        st_scr[...] = s0_ref[0]

    q = gla_ref[:, 0:GLA_QK].astype(F32) * (GLA_DK ** -0.5)
    k = gla_ref[:, GLA_QK:2 * GLA_QK].astype(F32)
    vb = gla_ref[:, 2 * GLA_QK:2 * GLA_QK + GLA_WIDTH]
    v = vb.astype(F32)

    ga = g2_ref[...]
    ga_hi = ga.astype(BF16)
    ga_lo = (ga - ga_hi.astype(F32)).astype(BF16)
    logit = (_dot(ga_hi, w2_ref[0]) + _dot(ga_hi, w2_ref[1]) + _dot(ga_lo, w2_ref[0])
             + b2_ref[...])
    lg = _log_sigmoid(logit) * (1.0 / GLA_GATE_TAU)

    shift = c.bit_length() - 1
    row = lax.broadcasted_iota(I32, (bt, bt), 0)
    col = lax.broadcasted_iota(I32, (bt, bt), 1)
    same = (row >> shift) == (col >> shift)
    tri = jnp.where(same & (col <= row), 1.0, 0.0).astype(BF16)
    last = jnp.where(same & ((col & (c - 1)) == c - 1), 1.0, 0.0).astype(BF16)
    l0, l1, l2 = _split3(lg)
    b = _dot(tri, l0) + _dot(tri, l1) + _dot(tri, l2)
    b0, b1, b2s = _split3(b)
    bl = _dot(last, b0) + _dot(last, b1) + _dot(last, b2s)

    zpad = jnp.zeros((c, GLA_QK), F32)
    kpad[0:c, :] = zpad
    bpad[0:c, :] = zpad
    vpad[0:c, :] = jnp.zeros((c, GLA_WIDTH), F32)
    kpad[c:c + bt, :] = k
    bpad[c:c + bt, :] = b
    vpad[c:c + bt, :] = v
    pos = lax.broadcasted_iota(I32, (bt, GLA_QK), 0) & (c - 1)
    seg = seg_ref[...]
    o_intra = jnp.zeros((bt, GLA_WIDTH), F32)
    for d in range(c):
        ks = kpad[c - d:c - d + bt, :]
        bs = bpad[c - d:c - d + bt, :]
        vs = vpad[c - d:c - d + bt, :]
        z = q * ks * jnp.exp(jnp.minimum(b - bs, 0.0))
        z = jnp.where(pos >= d, z, 0.0)
        o_intra = o_intra + _dot(z.astype(BF16), seg) * vs
    oi_scr[...] = o_intra

    qt_scr[...] = q * jnp.exp(b)
    kt_scr[...] = k * jnp.exp(bl - b)
    dec_scr[...] = jnp.exp(bl)
    bd = bd_ref[...]

    def step(ci, carry):
        r0 = pl.multiple_of(ci * c, c)
        qc = qt_scr[pl.ds(r0, c), :].astype(BF16)
        kc = kt_scr[pl.ds(r0, c), :].astype(BF16)
        vc = vpad[pl.ds(r0 + c, c), :].astype(BF16)
        st = st_scr[...]
        o_ref[pl.ds(r0, c), :] = oi_scr[pl.ds(r0, c), :] + _dot_nt(qc, st.astype(BF16))
        dec = dec_scr[pl.ds(r0, 1), :]
        st_scr[...] = st * dec + _dot_tn(vc, kc) * bd
        return carry

    lax.fori_loop(0, bt // c, step, 0)

    @pl.when(t == pl.num_programs(1) - 1)
    def _():
        sT_ref[0] = st_scr[...]


def _gla_consts(w_gla_a2, b_gla_a2):
    w2 = jnp.zeros((LANE, GLA_QK), F32).at[0:GLA_GATE_RANK].set(w_gla_a2)
    w2_hi = w2.astype(BF16)
    w2_lo = (w2 - w2_hi.astype(F32)).astype(BF16)
    w2s = jnp.stack([w2_hi, w2_lo])
    hq = jnp.arange(GLA_QK) // GLA_DK
    hv = jnp.arange(GLA_WIDTH) // GLA_DV
    seg = (hq[:, None] == hv[None, :]).astype(BF16)
    bd = (hv[:, None] == hq[None, :]).astype(F32)
    return w2s, b_gla_a2.reshape(1, GLA_QK), seg, bd


def _state_to_kernel(s):
    n = s.shape[0]
    eye = jnp.eye(GLA_HEADS, dtype=s.dtype)
    t = jnp.swapaxes(s, 2, 3)[:, :, :, None, :] * eye[None, :, None, :, None]
    return t.reshape(n, GLA_WIDTH, GLA_QK)


def _state_from_kernel(t):
    n = t.shape[0]
    t5 = t.reshape(n, GLA_HEADS, GLA_DV, GLA_HEADS, GLA_DK)
    return jnp.einsum("shehd->shde", t5)


def _gla(gla, g2, s0, consts, n_streams, t_len, bt, c):
    w2s, b2, seg, bd = consts
    nb = t_len // bt
    kern = functools.partial(_gla_kernel, bt=bt, c=c)
    const = lambda a: pl.BlockSpec(a.shape, lambda s, t: (0,) * a.ndim)
    o, sT = pl.pallas_call(
        kern,
        grid=(n_streams, nb),
        in_specs=[
            pl.BlockSpec((bt, 2 * GLA_QK + GLA_WIDTH), lambda s, t: (s * nb + t, 0)),
            pl.BlockSpec((bt, LANE), lambda s, t: (s * nb + t, 0)),
            const(w2s), const(b2), const(seg), const(bd),
            pl.BlockSpec((1, GLA_WIDTH, GLA_QK), lambda s, t: (s, 0, 0)),
        ],
        out_specs=[
            pl.BlockSpec((bt, GLA_WIDTH), lambda s, t: (s * nb + t, 0)),
            pl.BlockSpec((1, GLA_WIDTH, GLA_QK), lambda s, t: (s, 0, 0)),
        ],
        out_shape=[
            jax.ShapeDtypeStruct((n_streams * t_len, GLA_WIDTH), F32),
            jax.ShapeDtypeStruct((n_streams, GLA_WIDTH, GLA_QK), F32),
        ],
        scratch_shapes=[
            pltpu.VMEM((GLA_WIDTH, GLA_QK), F32),
            pltpu.VMEM((bt + c, GLA_QK), F32),
            pltpu.VMEM((bt + c, GLA_QK), F32),
            pltpu.VMEM((bt + c, GLA_WIDTH), F32),
            pltpu.VMEM((bt, GLA_QK), F32),
            pltpu.VMEM((bt, GLA_QK), F32),
            pltpu.VMEM((bt, GLA_QK), F32),
            pltpu.VMEM((bt, GLA_WIDTH), F32),
        ],
        compiler_params=_cparams(("arbitrary", "arbitrary")),
        name="gla",
    )(gla, g2, w2s, b2, seg, bd, s0)
    return o, sT


INT_MIN = -2 ** 31
NEG_INF = float("-inf")
KEY_NEG_INF = -2139095041
F32_LOWEST = -3.4028234663852886e38
IDX_SCALE = IDX_DIM ** -0.5 * IDX_HEADS ** -0.5


def _fill_head_weights(wb_scr, g2_ref, bq):
    for h in range(IDX_HEADS):
        c0 = GLA_GATE_RANK + h
        col = g2_ref[:, c0:c0 + 1] * IDX_SCALE
        wb_scr[h] = jnp.broadcast_to(col, (bq, LANE))


def _index_scores(iq_ref, wb_scr, iklo, ikhi, bk):
    parts = [None] * (bk // LANE)
    for p in range(IDX_HEADS // 2):
        iqp = iq_ref[:, p * LANE:(p + 1) * LANE]
        y0 = jnp.maximum(_dot_nt(iqp, iklo), 0.0)
        y1 = jnp.maximum(_dot_nt(iqp, ikhi), 0.0)
        w0 = wb_scr[2 * p]
        w1 = wb_scr[2 * p + 1]
        for c in range(bk // LANE):
            t = y0[:, c * LANE:(c + 1) * LANE] * w0 + y1[:, c * LANE:(c + 1) * LANE] * w1
            parts[c] = t if parts[c] is None else parts[c] + t
    return jnp.concatenate(parts, axis=1)


def _key_to_float(key):
    key = jnp.maximum(key, KEY_NEG_INF)
    return pltpu.bitcast(key ^ ((key >> 31) & 0x7FFFFFFF), F32)


def _kth_largest(count, total, topk, shape):
    def bit_body(i, carry):
        prefix, n_ge = carry
        cand_u = prefix | jnp.left_shift(jnp.int32(1), 31 - i)
        cnt = count(_key_to_float(cand_u ^ INT_MIN), False)
        ok = cnt >= topk
        return jnp.where(ok, cand_u, prefix), jnp.where(ok, cnt, n_ge)

    init = (jnp.zeros(shape, I32), jnp.zeros(shape, I32) + total)
    prefix, n_ge = lax.fori_loop(0, 32, bit_body, init)
    t = _key_to_float(prefix ^ INT_MIN)
    return t, n_ge, count(t, True)


def _count_rows(load_tile, n_tiles, cand, strict, bq, bk):
    cand_b = jnp.broadcast_to(cand, (bq, LANE))

    def body(kt, acc):
        s = load_tile(kt)
        for c in range(bk // LANE):
            sc = s[:, c * LANE:(c + 1) * LANE]
            acc = acc + jnp.where(sc > cand_b if strict else sc >= cand_b, 1, 0)
        return acc

    acc = lax.fori_loop(0, n_tiles, body, jnp.zeros((bq, LANE), I32))
    return jnp.sum(acc, axis=1, keepdims=True)


def _attend_tile(sel, q_ref, kv_tile, m_scr, l_scr, acc_scr):
    for h in range(DSA_HEADS):
        hs = slice(h * DSA_DH, (h + 1) * DSA_DH)
        kh, vh = kv_tile(h)
        lg = jnp.where(sel, _dot_nt(q_ref[:, hs], kh), NEG_BIG)
        m_old = m_scr[h]
        m_new = jnp.maximum(m_old, jnp.max(lg, axis=1, keepdims=True))
        alpha = jnp.exp2(m_old - m_new)
        p = jnp.exp2(lg - m_new)
        l_scr[h] = alpha * l_scr[h] + jnp.sum(p, axis=1, keepdims=True)
        acc_scr[:, hs] = alpha * acc_scr[:, hs] + _dot(p.astype(BF16), vh)
        m_scr[h] = m_new


def _select(s, t, n_gt, eq_before, topk, upper, ties):
    if not ties:
        return s >= jnp.maximum(t, F32_LOWEST), eq_before
    eq = s == t
    eq_f = jnp.where(eq, 1.0, 0.0).astype(BF16)
    rank = eq_before + _dot(eq_f, upper)
    need = (topk - n_gt).astype(F32)
    sel = ((s > t) | (eq & (rank < need))) & (s > NEG_INF)
    return sel, eq_before + jnp.sum(eq_f.astype(F32), axis=1, keepdims=True)


def _init_softmax(m_scr, l_scr, acc_scr):
    m_scr[...] = jnp.full(m_scr.shape, NEG_BIG, F32)
    l_scr[...] = jnp.zeros(l_scr.shape, F32)
    acc_scr[...] = jnp.zeros(acc_scr.shape, F32)


def _finish_softmax(o_ref, l_scr, acc_scr):
    for h in range(DSA_HEADS):
        hs = slice(h * DSA_DH, (h + 1) * DSA_DH)
        o_ref[:, hs] = (acc_scr[:, hs] / l_scr[h]).astype(o_ref.dtype)


def _upper_ones(n):
    r = jnp.arange(n)
    return (r[:, None] < r[None, :]).astype(BF16)


DSA_BQ = 256
DSA_BK = 1024
V_AHEAD = 3
V_RING = V_AHEAD + 1

def _count_cols(sc_scr, n_tiles, cand, strict, bq, bk):
    cand_b = jnp.broadcast_to(cand, (SUBLANE, bq))
    n_acc = 4

    def body(kt, accs):
        accs = list(accs)
        for r in range(bk // SUBLANE):
            s = sc_scr[kt, r * SUBLANE:(r + 1) * SUBLANE, :]
            accs[r % n_acc] = accs[r % n_acc] + jnp.where(s > cand_b if strict else s >= cand_b, 1, 0)
        return tuple(accs)

    zero = jnp.zeros((SUBLANE, bq), I32)
    accs = lax.fori_loop(0, n_tiles, body, (zero,) * n_acc)
    return jnp.sum((accs[0] + accs[1]) + (accs[2] + accs[3]), axis=0, keepdims=True)


def _max_min_cols(sc_scr, n_tiles, bq, bk):
    def body(kt, carry):
        mx, mn = carry
        for r in range(bk // SUBLANE):
            s = sc_scr[kt, r * SUBLANE:(r + 1) * SUBLANE, :]
            mx = jnp.maximum(mx, s)
            mn = jnp.minimum(mn, jnp.where(s > NEG_INF, s, -NEG_INF))
        return mx, mn

    init = (jnp.full((SUBLANE, bq), NEG_INF, F32), jnp.full((SUBLANE, bq), -NEG_INF, F32))
    mx, mn = lax.fori_loop(0, n_tiles, body, init)
    return jnp.max(mx, axis=0, keepdims=True), jnp.min(mn, axis=0, keepdims=True)


BISECT_WARMUP = 12
BISECT_CAP = 28


def _threshold(count, sc_scr, n_tiles, topk, bq, bk, res_f, res_i):
    zero = jnp.zeros((1, bq), F32)
    n_pos = count(zero, True)
    n_nn = count(zero, False)
    n_fin = count(jnp.full((1, bq), NEG_INF, F32), True)
    mx, mn = _max_min_cols(sc_scr, n_tiles, bq, bk)
    total = n_tiles * bk
    pos = n_pos >= topk
    at_zero = jnp.logical_not(pos) & (n_nn >= topk)
    none = n_fin < topk
    lo = jnp.where(pos | at_zero, zero, jnp.where(none, NEG_INF, mn))
    n_lo = jnp.where(pos | at_zero, n_nn, jnp.where(none, total, n_fin))
    hi = jnp.where(pos, mx * 2.0, zero)
    n_hi = jnp.where(pos, 0, jnp.where(at_zero, n_pos, n_nn))
    done = jnp.where(at_zero | none | (n_lo == topk), 1, 0)

    def movable(lo, hi, done):
        mid = lo + (hi - lo) * 0.5
        return (done == 0) & (mid > lo) & (mid < hi), mid

    def step(state):
        lo, hi, n_lo, n_hi, done = state
        ok, mid = movable(lo, hi, done)
        cnt = count(mid, False)
        up = ok & (cnt >= topk)
        down = ok & (cnt < topk)
        lo = jnp.where(up, mid, lo)
        n_lo = jnp.where(up, cnt, n_lo)
        hi = jnp.where(down, mid, hi)
        n_hi = jnp.where(down, cnt, n_hi)
        return lo, hi, n_lo, n_hi, jnp.where(n_lo == topk, 1, done)

    def any_movable(state):
        return jnp.max(jnp.where(movable(state[0], state[1], state[4])[0], 1, 0))

    state = lax.fori_loop(0, BISECT_WARMUP, lambda i, s: step(s), (lo, hi, n_lo, n_hi, done))

    def cond(c):
        return (c[0] < BISECT_CAP) & (c[1] > 0)

    def body(c):
        s = step(step(c[2]))
        return c[0] + 2, any_movable(s), s

    _, _, state = lax.while_loop(cond, body, (jnp.int32(BISECT_WARMUP), any_movable(state), state))
    lo, hi, n_lo, n_hi, done = state
    res_f[0:1, :] = lo
    res_i[0:1, :] = n_lo
    res_i[1:2, :] = n_hi

    @pl.when(jnp.min(done) == 0)
    def _():
        t, n_ge, n_gt = _kth_largest(count, total, topk, (1, bq))
        res_f[0:1, :] = t
        res_i[0:1, :] = n_ge
        res_i[1:2, :] = n_gt

    return res_f[0:1, :], res_i[0:1, :], res_i[1:2, :]


def _dsa_prompt_t_kernel(qt_ref, iqt_ref, g2t_ref, ik_ref, lower_ref, k_ref, vt_hbm,
                         o_ref, sc_scr, vbuf, sem, acc_scr, p_scr, res_f, res_i,
                         *, bq, bk, topk):
    qb = pl.program_id(0)
    q0 = qb * bq
    n_tiles = (q0 + bq + bk - 1) // bk
    sub = LANE

    def tile_scores(kt, masked):
        for s in range(bk // sub):
            ik_s = ik_ref[kt, s * sub:(s + 1) * sub, 0:IDX_DIM]
            acc = None
            for h in range(IDX_HEADS):
                y = _dot(ik_s, iqt_ref[h * IDX_DIM:(h + 1) * IDX_DIM, :])
                w = g2t_ref[GLA_GATE_RANK + h:GLA_GATE_RANK + h + 1, :] * IDX_SCALE
                t = jnp.maximum(y, 0.0) * w
                acc = t if acc is None else acc + t
            if masked:
                kpos = kt * bk + s * sub + lax.broadcasted_iota(I32, (sub, bq), 0)
                qpos = q0 + lax.broadcasted_iota(I32, (sub, bq), 1)
                shift = CHUNK.bit_length() - 1
                acc = jnp.where((kpos >> shift) <= (qpos >> shift), acc, NEG_INF)
            sc_scr[kt, s * sub:(s + 1) * sub, :] = acc

    def score_body(kt, carry):
        tile_scores(kt, False)
        return carry

    lax.fori_loop(0, n_tiles - 1, score_body, 0)
    tile_scores(n_tiles - 1, True)

    count = lambda cand, strict: _count_cols(sc_scr, n_tiles, cand, strict, bq, bk)
    t, n_ge, n_gt = _threshold(count, sc_scr, n_tiles, topk, bq, bk, res_f, res_i)
    def v_copy(kt):
        slot = lax.rem(kt, V_RING)
        return pltpu.make_async_copy(vt_hbm.at[kt], vbuf.at[slot], sem.at[slot])

    def v_prefetch(kt):
        @pl.when(kt < n_tiles)
        def _():
            v_copy(kt).start()

    for j in range(V_AHEAD):
        v_prefetch(j)

    tied = (n_ge > topk) & (t > NEG_INF)

    @pl.when(jnp.max(jnp.where(tied, 1, 0)) > 0)
    def _():
        need = (topk - n_gt).astype(F32)

        def drop_body(kt, eq_before):
            s = sc_scr[kt]
            eq = (s == t) & tied
            eq_b = jnp.where(eq, 1.0, 0.0).astype(BF16)
            rank = eq_before + _dot(lower_ref[...], eq_b)
            sc_scr[kt] = jnp.where(eq & (rank >= need), NEG_INF, s)
            return eq_before + jnp.sum(eq_b.astype(F32), axis=0, keepdims=True)

        lax.fori_loop(0, n_tiles, drop_body, jnp.zeros((1, bq), F32))

    thr = jnp.maximum(t, F32_LOWEST)

    def logits(kt, h):
        hs = slice(h * DSA_DH, (h + 1) * DSA_DH)
        return jnp.where(sc_scr[kt] >= thr, _dot(k_ref[kt, :, hs], qt_ref[hs, :]), NEG_BIG)

    def max_body(kt, ms):
        return tuple(jnp.maximum(ms[h], jnp.max(logits(kt, h), axis=0, keepdims=True))
                     for h in range(DSA_HEADS))

    ms = lax.fori_loop(0, n_tiles, max_body, (jnp.full((1, bq), NEG_BIG, F32),) * DSA_HEADS)

    acc_scr[...] = jnp.zeros(acc_scr.shape, F32)

    def sum_body(kt, ls):
        v_prefetch(kt + V_AHEAD)
        new_ls = []
        for h in range(DSA_HEADS):
            p = jnp.exp2(logits(kt, h) - ms[h])
            new_ls.append(ls[h] + jnp.sum(p, axis=0, keepdims=True))
            p_scr[h] = p.astype(BF16)
        v_copy(kt).wait()
        slot = lax.rem(kt, V_RING)
        for h in range(DSA_HEADS):
            hs = slice(h * DSA_DH, (h + 1) * DSA_DH)
            acc_scr[hs, :] += _dot(vbuf[slot, hs, :], p_scr[h])
        return tuple(new_ls)

    ls = lax.fori_loop(0, n_tiles, sum_body, (jnp.zeros((1, bq), F32),) * DSA_HEADS)
    for h in range(DSA_HEADS):
        hs = slice(h * DSA_DH, (h + 1) * DSA_DH)
        acc_scr[hs, :] = acc_scr[hs, :] / ls[h]
    o_ref[...] = jnp.transpose(acc_scr[...]).astype(o_ref.dtype)


def _dsa_prompt_t(qt, iqt, g2t, ikb, kb, vt3, bq, bk):
    t_len = qt.shape[1]
    n_kt = t_len // bk
    assert vt3.shape == (n_kt, DSA_WIDTH, bk) and bk % bq == 0
    topk = min(IDX_TOPK_MAX, t_len // 4)
    r = jnp.arange(bk)
    lower = (r[None, :] < r[:, None]).astype(BF16)
    kern = functools.partial(_dsa_prompt_t_kernel, bq=bq, bk=bk, topk=topk)
    col = lambda h: pl.BlockSpec((h, bq), lambda i: (0, i))
    ik3 = ikb.reshape(n_kt, bk, LANE)
    k3 = kb.reshape(n_kt, bk, DSA_WIDTH)
    return pl.pallas_call(
        kern,
        grid=(t_len // bq,),
        in_specs=[col(DSA_WIDTH), col(IDX_HEADS * IDX_DIM), col(LANE),
                  pl.BlockSpec(ik3.shape, lambda i: (0, 0, 0)),
                  pl.BlockSpec(lower.shape, lambda i: (0, 0)),
                  pl.BlockSpec(k3.shape, lambda i: (0, 0, 0)),
                  pl.BlockSpec(memory_space=pl.ANY)],
        out_specs=pl.BlockSpec((bq, DSA_WIDTH), lambda i: (i, 0)),
        out_shape=jax.ShapeDtypeStruct((t_len, DSA_WIDTH), BF16),
        scratch_shapes=[
            pltpu.VMEM((n_kt, bk, bq), F32),
            pltpu.VMEM((V_RING, DSA_WIDTH, bk), BF16),
            pltpu.SemaphoreType.DMA((V_RING,)),
            pltpu.VMEM((DSA_WIDTH, bq), F32),
            pltpu.VMEM((DSA_HEADS, bk, bq), BF16),
            pltpu.VMEM((SUBLANE, bq), F32),
            pltpu.VMEM((SUBLANE, bq), I32),
        ],
        compiler_params=_cparams(("arbitrary",)),
        name="dsa_prompt_t",
    )(qt, iqt, g2t, ik3, lower, k3, vt3)


def _dsa_sample_kernel(q_ref, iq_ref, g2_ref, ck_ref, cv_ref, cilo_ref, cihi_ref,
                       nk_ref, nv_ref, nilo_ref, nihi_ref, upper_ref, o_ref,
                       sc_scr, wb_scr, m_scr, l_scr, acc_scr, pk_scr, pv_scr, plo_scr, phi_scr,
                       *, bq, bk, past, topk):
    n_cache = past // bk
    sub = bk // LANE
    _fill_head_weights(wb_scr, g2_ref, bq)
    pk_scr[...] = jnp.zeros(pk_scr.shape, BF16)
    pv_scr[...] = jnp.zeros(pv_scr.shape, BF16)
    plo_scr[...] = jnp.zeros(plo_scr.shape, BF16)
    phi_scr[...] = jnp.zeros(phi_scr.shape, BF16)
    pk_scr[0:bq, :] = nk_ref[...]
    pv_scr[0:bq, :] = nv_ref[...]
    plo_scr[0:bq, :] = nilo_ref[...]
    phi_scr[0:bq, :] = nihi_ref[...]

    for kt in range(n_cache):
        s = _index_scores(iq_ref, wb_scr, cilo_ref[0, kt * bk:(kt + 1) * bk, :],
                          cihi_ref[0, kt * bk:(kt + 1) * bk, :], bk)
        for c in range(sub):
            sc_scr[kt * sub + c] = s[:, c * LANE:(c + 1) * LANE]
    s = _index_scores(iq_ref, wb_scr, plo_scr[...], phi_scr[...], LANE)
    real = lax.broadcasted_iota(I32, (bq, LANE), 1) < bq
    sc_scr[n_cache * sub] = jnp.where(real, s, NEG_INF)

    n_sub = n_cache * sub + 1
    count = lambda cand, strict: _count_rows(lambda i: sc_scr[i], n_sub, cand, strict, bq, LANE)
    t, n_ge, n_gt = _kth_largest(count, n_sub * LANE, topk, (bq, 1))
    tie_rows = jnp.where((n_ge > topk) & (t > NEG_INF), 1, 0)
    any_tie = jnp.max(tie_rows) > 0

    def attend(ties):
        _init_softmax(m_scr, l_scr, acc_scr)
        eq_before = jnp.zeros((bq, 1), F32)
        for kt in range(n_cache):
            keys = jnp.concatenate([sc_scr[kt * sub + c] for c in range(sub)], axis=1)
            upper = upper_ref[...] if ties else None
            sel, eq_before = _select(keys, t, n_gt, eq_before, topk, upper, ties)
            kv_tile = lambda h, kt=kt: (
                ck_ref[0, pl.ds(kt * bk * DSA_HEADS + h, bk, stride=DSA_HEADS), :].astype(BF16),
                cv_ref[0, pl.ds(kt * bk * DSA_HEADS + h, bk, stride=DSA_HEADS), :].astype(BF16))
            _attend_tile(sel, q_ref, kv_tile, m_scr, l_scr, acc_scr)
        upper = upper_ref[0:LANE, 0:LANE] if ties else None
        sel, eq_before = _select(sc_scr[n_cache * sub], t, n_gt, eq_before, topk, upper, ties)
        kv_tile = lambda h: (pk_scr[:, h * DSA_DH:(h + 1) * DSA_DH],
                             pv_scr[:, h * DSA_DH:(h + 1) * DSA_DH])
        _attend_tile(sel, q_ref, kv_tile, m_scr, l_scr, acc_scr)
        _finish_softmax(o_ref, l_scr, acc_scr)

    @pl.when(jnp.logical_not(any_tie))
    def _():
        attend(False)

    @pl.when(any_tie)
    def _():
        attend(True)


def _dsa_sample(q, iq, g2, cache_k, cache_v, cache_ilo, cache_ihi, kb, vb, iklo, ikhi, bq, bk):
    n_streams, past = cache_k.shape[0:2]
    cache_k = cache_k.reshape(n_streams, past * DSA_HEADS, DSA_DH)
    cache_v = cache_v.reshape(n_streams, past * DSA_HEADS, DSA_DH)
    assert past % CHUNK == 0 and bq <= CHUNK and past % bk == 0
    topk = min(IDX_TOPK_MAX, (past + bq) // 4)
    upper = _upper_ones(bk)
    kern = functools.partial(_dsa_sample_kernel, bq=bq, bk=bk, past=past, topk=topk)
    row = lambda w: pl.BlockSpec((bq, w), lambda s: (s, 0))
    cache = lambda w: pl.BlockSpec((1, past, w), lambda s: (s, 0, 0))
    cache_kv = pl.BlockSpec((1, past * DSA_HEADS, DSA_DH), lambda s: (s, 0, 0))
    n_sub = past // LANE + 1
    return pl.pallas_call(
        kern,
        grid=(n_streams,),
        in_specs=[row(DSA_WIDTH), row(IDX_HEADS * IDX_DIM), row(LANE),
                  cache_kv, cache_kv, cache(LANE), cache(LANE),
                  row(DSA_WIDTH), row(DSA_WIDTH), row(LANE), row(LANE),
                  pl.BlockSpec(upper.shape, lambda s: (0, 0))],
        out_specs=row(DSA_WIDTH),
        out_shape=jax.ShapeDtypeStruct((n_streams * bq, DSA_WIDTH), BF16),
        scratch_shapes=[
            pltpu.VMEM((n_sub, bq, LANE), F32),
            pltpu.VMEM((IDX_HEADS, bq, LANE), F32),
            pltpu.VMEM((DSA_HEADS, bq, 1), F32),
            pltpu.VMEM((DSA_HEADS, bq, 1), F32),
            pltpu.VMEM((bq, DSA_WIDTH), F32),
            pltpu.VMEM((LANE, DSA_WIDTH), BF16),
            pltpu.VMEM((LANE, DSA_WIDTH), BF16),
            pltpu.VMEM((LANE, LANE), BF16),
            pltpu.VMEM((LANE, LANE), BF16),
        ],
        compiler_params=_cparams(("parallel",)),
        name="dsa_sample",
    )(q, iq, g2, cache_k, cache_v, cache_ilo, cache_ihi, kb, vb, iklo, ikhi, upper)


def _mix_kernel(og_ref, gg_ref, od_ref, x_ref, gnw_ref, wo_ref, nfw_ref, rw_ref, rb_ref,
                h1_ref, xn_ref, re_ref, rg_ref, cnt_ref):
    g = gg_ref[...].astype(F32)
    gate = g / (1.0 + jnp.exp(-g))
    acc = _dot(od_ref[...], wo_ref[GLA_WIDTH:GLA_WIDTH + DSA_WIDTH, :])
    for h in range(GLA_HEADS):
        hs = slice(h * GLA_DV, (h + 1) * GLA_DV)
        oh = og_ref[:, hs]
        ms = jnp.mean(oh * oh, axis=-1, keepdims=True)
        a = oh * lax.rsqrt(ms + EPS) * gnw_ref[...] * gate[:, hs]
        acc = acc + _dot(a.astype(BF16), wo_ref[hs, :])
    h1 = x_ref[...] + acc
    h1_ref[...] = h1
    ms = jnp.mean(h1 * h1, axis=-1, keepdims=True)
    xn = h1 * lax.rsqrt(ms + EPS) * nfw_ref[...]
    _rows_to_tiles(xn_ref, xn)
    xn_hi = xn.astype(BF16)
    xn_lo = (xn - xn_hi.astype(F32)).astype(BF16)
    logits = (_dot(xn_hi, rw_ref[0]) + _dot(xn_hi, rw_ref[1]) + _dot(xn_lo, rw_ref[0])
              + rb_ref[...])
    lane = lax.broadcasted_iota(I32, logits.shape, 1)
    lane_f = lane.astype(F32)
    work = logits
    e_out = jnp.zeros(logits.shape, I32)
    tops, hots = [], []
    for k in range(TOP_K):
        mx = jnp.max(work, axis=1, keepdims=True)
        idx = jnp.min(jnp.where(work == mx, lane_f, float(LANE)), axis=1, keepdims=True)
        idx = idx.astype(I32)
        hot = lane == idx
        e_out = jnp.where(lane == k, idx, e_out)
        tops.append(mx)
        hots.append(hot)
        work = jnp.where(hot, NEG_BIG, work)
    ex = [jnp.exp(v - tops[0]) for v in tops]
    den = ex[0] + ex[1] + ex[2] + ex[3]
    g_out = jnp.zeros(logits.shape, F32)
    for k in range(TOP_K):
        g_out = jnp.where(lane == k, ex[k] / den, g_out)
    rg_ref[...] = g_out
    bm = logits.shape[0]
    earlier = (lax.broadcasted_iota(I32, (bm, bm), 1) < lax.broadcasted_iota(I32, (bm, bm), 0))
    earlier = jnp.where(earlier, 1.0, 0.0).astype(BF16)
    run = jnp.zeros((1, LANE), F32)
    for k in range(TOP_K):
        hot_f = jnp.where(hots[k], 1.0, 0.0)
        before = _dot(earlier, hot_f.astype(BF16)) + run
        rank = jnp.sum(jnp.where(hots[k], before, 0.0), axis=1, keepdims=True).astype(I32)
        e_out = jnp.where(lane == TOP_K + k, rank, e_out)
        run = run + jnp.sum(hot_f, axis=0, keepdims=True)
    re_ref[...] = e_out
    cnt_ref[0] = jnp.broadcast_to(run, (SUBLANE, LANE)).astype(I32)


def _mix_consts(gla_norm_w, w_out, norm_ffn_w, router_w, router_b):
    rw = jnp.zeros((D_MODEL, LANE), F32).at[:, 0:N_EXPERTS].set(router_w)
    rw_hi = rw.astype(BF16)
    rw_lo = (rw - rw_hi.astype(F32)).astype(BF16)
    rb = jnp.full((1, LANE), NEG_BIG, F32).at[0, 0:N_EXPERTS].set(router_b)
    return (gla_norm_w.reshape(1, GLA_DV), w_out.astype(BF16), norm_ffn_w.reshape(1, D_MODEL),
            jnp.stack([rw_hi, rw_lo]), rb)


def _mix(og, gla, od, x2d, consts, bm):
    n = x2d.shape[0]
    gnw, wo, nfw, rws, rb = consts
    row = lambda w: pl.BlockSpec((bm, w), lambda i: (i, 0))
    const = lambda a: pl.BlockSpec(a.shape, lambda i: (0,) * a.ndim)
    gg_col = (2 * GLA_QK + GLA_WIDTH) // GLA_WIDTH
    return pl.pallas_call(
        _mix_kernel,
        grid=(n // bm,),
        in_specs=[row(GLA_WIDTH), pl.BlockSpec((bm, GLA_WIDTH), lambda i: (i, gg_col)),
                  row(DSA_WIDTH), row(D_MODEL),
                  const(gnw), const(wo), const(nfw), const(rws), const(rb)],
        out_specs=[row(D_MODEL), pl.BlockSpec((bm * ROW_TILE, LANE), lambda i: (i, 0)),
                   row(LANE), row(LANE), pl.BlockSpec((1, SUBLANE, LANE), lambda i: (i, 0, 0))],
        out_shape=[jax.ShapeDtypeStruct((n, D_MODEL), F32),
                   jax.ShapeDtypeStruct((n * ROW_TILE, LANE), F32),
                   jax.ShapeDtypeStruct((n, LANE), I32), jax.ShapeDtypeStruct((n, LANE), F32),
                   jax.ShapeDtypeStruct((n // bm, SUBLANE, LANE), I32)],
        compiler_params=_cparams(("parallel",)),
        name="mix",
    )(og, gla, od, x2d, gnw, wo, nfw, rws, rb)


MOE_ROWS = 256
DMA_UNROLL = 8
TOKEN_BLOCK = 512


def _route_tables(counts, br, n_pairs):
    c = counts[:, 0, :]
    tot = jnp.sum(c, axis=0)
    padded = (tot + br - 1) // br * br
    pad_end = jnp.cumsum(padded)
    pad_start = pad_end - padded
    base = pad_start[None, :] + jnp.cumsum(c, axis=0) - c
    base = jnp.broadcast_to(base[:, None, :], counts.shape).astype(I32)
    nb = -(-(n_pairs + N_EXPERTS * (br - 1)) // br)
    n_used = (pad_end[N_EXPERTS - 1] // br).astype(I32)
    blk = jnp.arange(nb, dtype=I32)
    last = jnp.minimum(blk, n_used - 1) * br
    block_e = jnp.sum((pad_end[None, 0:N_EXPERTS] <= last[:, None]).astype(I32), axis=1)
    ends = pad_end[0:N_EXPERTS].astype(I32)
    tots = jnp.concatenate([tot[0:N_EXPERTS].astype(I32), n_used.reshape(1)])
    return base, jnp.minimum(block_e, N_EXPERTS - 1), n_used.reshape(1), ends, tots


def _pos_kernel(re_ref, base_ref, pos_ref):
    re = re_ref[...]
    lane = lax.broadcasted_iota(I32, re.shape, 1)
    base_row = base_ref[0, 0:1, :].astype(F32)
    p = jnp.zeros(re.shape, F32)
    for k in range(TOP_K):
        hot = lane == re[:, k:k + 1]
        first = jnp.sum(jnp.where(hot, base_row, 0.0), axis=1, keepdims=True)
        p = jnp.where(lane == k, first + re[:, TOP_K + k:TOP_K + k + 1].astype(F32), p)
    pos_ref[0] = jnp.transpose(p)[0:SUBLANE, :].astype(I32) * ROW_TILE


def _positions(re, base, bm):
    n = re.shape[0]
    return pl.pallas_call(
        _pos_kernel,
        grid=(n // bm,),
        in_specs=[pl.BlockSpec((bm, LANE), lambda i: (i, 0)),
                  pl.BlockSpec((1, SUBLANE, LANE), lambda i: (i, 0, 0))],
        out_specs=pl.BlockSpec((1, SUBLANE, bm), lambda i: (i, 0, 0)),
        out_shape=jax.ShapeDtypeStruct((n // bm, SUBLANE, bm), I32),
        compiler_params=_cparams(("parallel",)),
        name="positions",
    )(re, base)


def _dispatch_kernel(end_ref, tot_ref, pos_ref, xa_ref, xb_ref, xr_hbm, zero_scr, sem,
                     *, bt, br, nsteps_a):
    i = pl.program_id(0)

    @pl.when(i == 0)
    def _():
        zero_scr[...] = jnp.zeros(zero_scr.shape, F32)
        n_used = tot_ref[N_EXPERTS]
        nb = xr_hbm.shape[0] // (br * ROW_TILE)

        def fills():
            for e in range(N_EXPERTS):
                yield tot_ref[e] > 0, end_ref[e] - br
            for j in range(N_EXPERTS):
                yield n_used + j < nb, jnp.minimum(n_used + j, nb - 1) * br

        for act in ("start", "wait"):
            for cond, row0 in fills():
                @pl.when(cond)
                def _(row0=row0, act=act):
                    dst = xr_hbm.at[pl.ds(pl.multiple_of(row0 * ROW_TILE, ROW_TILE), br * ROW_TILE)]
                    c = pltpu.make_async_copy(zero_scr, dst, sem.at[1])
                    c.start() if act == "start" else c.wait()

    def scatter(x_ref):
        for k in range(TOP_K):
            def body(r, c, k=k):
                src = x_ref.at[pl.ds(pl.multiple_of(r * ROW_TILE, ROW_TILE), ROW_TILE)]
                dst = xr_hbm.at[pl.ds(pl.multiple_of(pos_ref[0, k, r], ROW_TILE), ROW_TILE)]
                pltpu.make_async_copy(src, dst, sem.at[0]).start()
                return c
            lax.fori_loop(0, bt, body, 0, unroll=DMA_UNROLL)

    @pl.when(i < nsteps_a)
    def _():
        scatter(xa_ref)

    @pl.when(i >= nsteps_a)
    def _():
        scatter(xb_ref)

    done = xr_hbm.at[pl.ds(0, TOP_K * bt * ROW_TILE)]
    pltpu.make_async_copy(done, done, sem.at[0]).wait()


def _dispatch(xn_a, xn_b, pos_t, pad_end, tot, n_rows, bt, br):
    na, nb_ = xn_a.shape[0] // (bt * ROW_TILE), xn_b.shape[0] // (bt * ROW_TILE)
    kern = functools.partial(_dispatch_kernel, bt=bt, br=br, nsteps_a=na)
    grid_spec = pltpu.PrefetchScalarGridSpec(
        num_scalar_prefetch=2,
        grid=(na + nb_,),
        in_specs=[
            pl.BlockSpec((1, SUBLANE, bt), lambda i, e, t: (i, 0, 0), memory_space=pltpu.SMEM),
            pl.BlockSpec((bt * ROW_TILE, LANE), lambda i, e, t: (jnp.minimum(i, na - 1), 0)),
            pl.BlockSpec((bt * ROW_TILE, LANE), lambda i, e, t: (jnp.maximum(i - na, 0), 0)),
        ],
        out_specs=pl.BlockSpec(memory_space=pl.ANY),
        scratch_shapes=[pltpu.VMEM((br * ROW_TILE, LANE), F32), pltpu.SemaphoreType.DMA((2,))],
    )
    return pl.pallas_call(
        kern,
        grid_spec=grid_spec,
        out_shape=jax.ShapeDtypeStruct((n_rows * ROW_TILE, LANE), F32),
        compiler_params=_cparams(("arbitrary",)),
        name="dispatch",
    )(pad_end, tot, pos_t, xn_a, xn_b)


def _rows_to_tiles(ref, x):
    n = x.shape[0]
    for j in range(ROW_TILE):
        ref[pl.ds(j, n, stride=ROW_TILE), :] = x[:, j * LANE:(j + 1) * LANE]


def _tile_chunk(ref, j, n, row0=0):
    return ref[pl.ds(row0 * ROW_TILE + j, n, stride=ROW_TILE), :]


def _rows_from_tiles(ref, n):
    return jnp.concatenate([_tile_chunk(ref, j, n) for j in range(ROW_TILE)], axis=1)


def _moe_kernel(be_ref, nu_ref, x_ref, wgu_ref, bgu_ref, wd_ref, bdn_ref,
                y_ref, wgu_bf, wd_bf):
    i = pl.program_id(0)
    n_used = nu_ref[0]

    @pl.when(i < n_used)
    def _():
        e = be_ref[i]
        prev = be_ref[jnp.maximum(i - 1, 0)]

        @pl.when((i == 0) | (e != prev))
        def _():
            wgu_bf[...] = wgu_ref[0].astype(BF16)
            wd_bf[...] = wd_ref[0].astype(BF16)

        xb = _rows_from_tiles(x_ref, x_ref.shape[0] // ROW_TILE).astype(BF16)
        gu = _dot(xb, wgu_bf[...]) + bgu_ref[0]
        g_lin = jnp.minimum(gu[:, 0:D_FF], SWIGLU_LIMIT)
        u_lin = jnp.clip(gu[:, D_FF:2 * D_FF], -SWIGLU_LIMIT, SWIGLU_LIMIT)
        act = g_lin / (1.0 + jnp.exp(-SWIGLU_ALPHA * g_lin)) * (u_lin + 1.0)
        y = _dot(act.astype(BF16), wd_bf[...]) + bdn_ref[0]
        _rows_to_tiles(y_ref, y)

    @pl.when(i >= n_used)
    def _():
        y_ref[...] = jnp.zeros(y_ref.shape, y_ref.dtype)


def _moe(x_rows, block_e, n_used, w_gu, b_gu, w_down, b_down, br):
    nb = x_rows.shape[0] // (br * ROW_TILE)
    grid_spec = pltpu.PrefetchScalarGridSpec(
        num_scalar_prefetch=2,
        grid=(nb,),
        in_specs=[
            pl.BlockSpec((br * ROW_TILE, LANE),
                         lambda i, be, nu: (jnp.minimum(i, nu[0] - 1), 0)),
            pl.BlockSpec((1, D_MODEL, 2 * D_FF), lambda i, be, nu: (be[i], 0, 0)),
            pl.BlockSpec((1, 1, 2 * D_FF), lambda i, be, nu: (be[i], 0, 0)),
            pl.BlockSpec((1, D_FF, D_MODEL), lambda i, be, nu: (be[i], 0, 0)),
            pl.BlockSpec((1, 1, D_MODEL), lambda i, be, nu: (be[i], 0, 0)),
        ],
        out_specs=pl.BlockSpec((br * ROW_TILE, LANE), lambda i, be, nu: (i, 0)),
        scratch_shapes=[
            pltpu.VMEM((D_MODEL, 2 * D_FF), BF16),
            pltpu.VMEM((D_FF, D_MODEL), BF16),
        ],
    )
    return pl.pallas_call(
        _moe_kernel,
        grid_spec=grid_spec,
        out_shape=jax.ShapeDtypeStruct((nb * br * ROW_TILE, LANE), F32),
        compiler_params=_cparams(("arbitrary",)),
        name="moe",
    )(block_e, n_used, x_rows, w_gu, b_gu.reshape(N_EXPERTS, 1, 2 * D_FF),
      w_down, b_down.reshape(N_EXPERTS, 1, D_MODEL))


def _combine_kernel(pos_ref, posn_ref, y_hbm, h1_ref, rg_ref, fw_ref, o_ref, yg, sem, *, bt):
    i = pl.program_id(0)
    nsteps = pl.num_programs(0)
    slot = lax.rem(i, 2)

    def issue(pref, s):
        for k in range(TOP_K):
            def body(r, c, k=k):
                src = y_hbm.at[pl.ds(pl.multiple_of(pref[0, k, r], ROW_TILE), ROW_TILE)]
                row0 = pl.multiple_of((k * bt + r) * ROW_TILE, ROW_TILE)
                pltpu.make_async_copy(src, yg.at[s, pl.ds(row0, ROW_TILE)], sem.at[s]).start()
                return c
            lax.fori_loop(0, bt, body, 0, unroll=DMA_UNROLL)

    @pl.when(i == 0)
    def _():
        issue(pos_ref, 0)

    @pl.when(i + 1 < nsteps)
    def _():
        issue(posn_ref, 1 - slot)

    pltpu.make_async_copy(yg.at[slot], yg.at[slot], sem.at[slot]).wait()
    gates = [jnp.broadcast_to(rg_ref[:, k:k + 1], (bt, LANE)) for k in range(TOP_K)]
    ss = jnp.zeros((bt, LANE), F32)
    rows = yg.at[slot]
    for j in range(ROW_TILE):
        js = slice(j * LANE, (j + 1) * LANE)
        a = h1_ref[:, js]
        for k in range(TOP_K):
            a = a + gates[k] * _tile_chunk(rows, j, bt, k * bt)
        ss = ss + a * a
        o_ref[:, js] = a
    ms = jnp.sum(ss, axis=-1, keepdims=True) * (1.0 / D_MODEL)
    o_ref[...] = o_ref[...] * lax.rsqrt(ms + EPS) * fw_ref[...]


def _combine(y_rows, pos_t, h1, rg, final_w, bt):
    n = h1.shape[0]
    nsteps = n // bt
    kern = functools.partial(_combine_kernel, bt=bt)
    row = lambda w: pl.BlockSpec((bt, w), lambda i: (i, 0))
    return pl.pallas_call(
        kern,
        grid=(nsteps,),
        in_specs=[
            pl.BlockSpec((1, SUBLANE, bt), lambda i: (i, 0, 0), memory_space=pltpu.SMEM),
            pl.BlockSpec((1, SUBLANE, bt), lambda i: (jnp.minimum(i + 1, nsteps - 1), 0, 0),
                         memory_space=pltpu.SMEM),
            pl.BlockSpec(memory_space=pl.ANY),
            row(D_MODEL), row(LANE),
            pl.BlockSpec((1, D_MODEL), lambda i: (0, 0)),
        ],
        out_specs=row(D_MODEL),
        out_shape=jax.ShapeDtypeStruct((n, D_MODEL), F32),
        scratch_shapes=[pltpu.VMEM((2, TOP_K * bt * ROW_TILE, LANE), F32),
                        pltpu.SemaphoreType.DMA((2,))],
        compiler_params=_cparams(("arbitrary",)),
        name="combine",
    )(pos_t, pos_t, y_rows, h1, rg, final_w.reshape(1, D_MODEL))


def kernel(x_prompt, x_sample, cache_k, cache_v, cache_k_idx, state_gla, norm_mix_w, w_in,
           w_gla_a2, b_gla_a2, gla_norm_w, w_out, norm_ffn_w, router_w, router_b, w_gu, b_gu,
           w_down, b_down, norm_final_w):
    b_p, t_p, _ = x_prompt.shape
    b_s, t_s, _ = x_sample.shape
    past = cache_k.shape[2]
    assert b_p == 1 and norm_mix_w.shape[0] == 1
    l = 0
    pw = _proj_weights(norm_mix_w[l], w_in[l])
    gconsts = _gla_consts(w_gla_a2[l], b_gla_a2[l])

    pp = _project_t(x_prompt.reshape(t_p, D_MODEL), _proj_t_weights(norm_mix_w[l], w_in[l]), DSA_BK)
    gla_p, kf_p, vf_p, kb_p, ikf_p, ikb_p, g2_p, qt_p, iqt_p, vt3_p, g2t_p = pp
    s0 = jnp.zeros((1, GLA_WIDTH, GLA_QK), F32)
    og_p, st_p = _gla(gla_p, g2_p, s0, gconsts, 1, t_p, 512, 16)
    od_p = _dsa_prompt_t(qt_p, iqt_p, g2t_p, ikb_p, kb_p, vt3_p, DSA_BQ, DSA_BK)

    ps = _project(x_sample.reshape(b_s * t_s, D_MODEL), pw, 512)
    gla_s, q_s, kf_s, vf_s, kb_s, vb_s, iq_s, iklo_s, ikhi_s, ikf_s, g2_s = ps
    og_s, st_s = _gla(gla_s, g2_s, _state_to_kernel(state_gla[l]), gconsts, b_s, t_s, t_s, 16)
    zc = jnp.zeros((b_s, past, IDX_DIM), BF16)
    cik = cache_k_idx[l].astype(BF16)
    cilo = jnp.concatenate([cik, zc], axis=2)
    cihi = jnp.concatenate([zc, cik], axis=2)
    od_s = _dsa_sample(q_s, iq_s, g2_s, cache_k[l], cache_v[l], cilo, cihi,
                       kb_s, vb_s, iklo_s, ikhi_s, t_s, 512)

    mconsts = _mix_consts(gla_norm_w[l], w_out[l], norm_ffn_w[l], router_w[l], router_b[l])
    n_s = b_s * t_s
    tb = TOKEN_BLOCK
    h1_p, xn_p, re_p, rg_p, cnt_p = _mix(og_p, gla_p, od_p, x_prompt.reshape(t_p, D_MODEL), mconsts, tb)
    h1_s, xn_s, re_s, rg_s, cnt_s = _mix(og_s, gla_s, od_s, x_sample.reshape(n_s, D_MODEL), mconsts, tb)
    n_pairs = (t_p + n_s) * TOP_K
    base, block_e, n_used, ends, tots = _route_tables(
        jnp.concatenate([cnt_p, cnt_s], axis=0), MOE_ROWS, n_pairs)
    n_rows = block_e.shape[0] * MOE_ROWS
    pos_p = _positions(re_p, base[0:t_p // tb], tb)
    pos_s = _positions(re_s, base[t_p // tb:], tb)
    x_rows = _dispatch(xn_p, xn_s, jnp.concatenate([pos_p, pos_s], axis=0), ends, tots,
                       n_rows, tb, MOE_ROWS)
    y_rows = _moe(x_rows, block_e, n_used, w_gu[l], b_gu[l], w_down[l], b_down[l], MOE_ROWS)
    y_p = _combine(y_rows, pos_p, h1_p, rg_p, norm_final_w, tb)
    y_s = _combine(y_rows, pos_s, h1_s, rg_s, norm_final_w, tb)
    y_p = y_p.reshape(x_prompt.shape)
    y_s = y_s.reshape(x_sample.shape)
    return (y_p, y_s,
            kf_p.reshape(1, 1, t_p, DSA_HEADS, DSA_DH), vf_p.reshape(1, 1, t_p, DSA_HEADS, DSA_DH),
            ikf_p.reshape(1, 1, t_p, IDX_DIM), _state_from_kernel(st_p)[None],
            kf_s.reshape(1, b_s, t_s, DSA_HEADS, DSA_DH), vf_s.reshape(1, b_s, t_s, DSA_HEADS, DSA_DH),
            ikf_s.reshape(1, b_s, t_s, IDX_DIM), _state_from_kernel(st_s)[None])
```

```python
import functools

import jax
import jax.numpy as jnp
from jax import lax
from jax.experimental import pallas as pl
from jax.experimental.pallas import tpu as pltpu

F32 = jnp.float32
BF16 = jnp.bfloat16
I32 = jnp.int32

D_MODEL = 1024
CHUNK = 64
GLA_HEADS = 4
GLA_DK = 64
GLA_DV = 128
GLA_QK = GLA_HEADS * GLA_DK
GLA_WIDTH = GLA_HEADS * GLA_DV
GLA_GATE_RANK = 16
GLA_GATE_TAU = 16.0
DSA_HEADS = 4
DSA_DH = 128
DSA_WIDTH = DSA_HEADS * DSA_DH
IDX_HEADS = 8
IDX_DIM = 64
IDX_TOPK_MAX = 256
N_EXPERTS = 32
TOP_K = 4
D_FF = 1024
SWIGLU_ALPHA = 1.702
SWIGLU_LIMIT = 7.0
EPS = 1e-6

_OFF_GLA = 0
_W_GLA = 2 * GLA_QK + 2 * GLA_WIDTH
_OFF_GA = _OFF_GLA + _W_GLA
_OFF_DSA = _OFF_GA + GLA_GATE_RANK
_W_DSA = 3 * DSA_WIDTH + IDX_HEADS * IDX_DIM
_OFF_IK = _OFF_DSA + _W_DSA
_OFF_IW = _OFF_IK + IDX_DIM
LANE = 128
SUBLANE = 8
ROW_TILE = D_MODEL // LANE
assert ROW_TILE == SUBLANE
LOG2E = 1.4426950408889634
NEG_BIG = -1e30
VMEM_LIMIT = 56 * 1024 * 1024


def _cparams(sem):
    return pltpu.CompilerParams(dimension_semantics=sem, vmem_limit_bytes=VMEM_LIMIT)


def _dot(a, b):
    return jnp.dot(a, b, preferred_element_type=F32)


def _dot_nt(a, b):
    return lax.dot_general(a, b, (((1,), (1,)), ((), ())), preferred_element_type=F32)


def _dot_tn(a, b):
    return lax.dot_general(a, b, (((0,), (0,)), ((), ())), preferred_element_type=F32)


def _split3(x):
    hi = x.astype(BF16)
    r1 = x - hi.astype(F32)
    mid = r1.astype(BF16)
    lo = (r1 - mid.astype(F32)).astype(BF16)
    return hi, mid, lo


def _proj_kernel(x_ref, nw_ref, wg_ref, wd_ref, ws_ref,
                 gla_ref, q_ref, kf_ref, vf_ref, kb_ref, vb_ref, iq_ref,
                 iklo_ref, ikhi_ref, ikf_ref, g2_ref):
    x = x_ref[...]
    ms = jnp.mean(x * x, axis=-1, keepdims=True)
    xn = (x * lax.rsqrt(ms + EPS) * nw_ref[...]).astype(BF16)
    gla_ref[...] = _dot(xn, wg_ref[...]).astype(BF16)
    W = DSA_WIDTH
    dq = _dot(xn, wd_ref[:, 0:W])
    q_ref[...] = (dq * (DSA_DH ** -0.5 * LOG2E)).astype(BF16)
    dk = _dot(xn, wd_ref[:, W:2 * W])
    kf_ref[...] = dk
    kb_ref[...] = dk.astype(BF16)
    dv = _dot(xn, wd_ref[:, 2 * W:3 * W])
    vf_ref[...] = dv
    vb_ref[...] = dv.astype(BF16)
    iq_ref[...] = _dot(xn, wd_ref[:, 3 * W:4 * W]).astype(BF16)
    sm = _dot(xn, ws_ref[...])
    iklo_ref[...] = sm[:, 0:LANE].astype(BF16)
    ikhi_ref[...] = sm[:, LANE:2 * LANE].astype(BF16)
    ikf_ref[...] = sm[:, 0:IDX_DIM]
    g2_ref[...] = sm[:, 2 * LANE:3 * LANE]


def _proj_weights(norm_w, w_in):
    wg = w_in[:, _OFF_GLA:_OFF_GLA + _W_GLA].astype(BF16)
    wd = w_in[:, _OFF_DSA:_OFF_DSA + _W_DSA].astype(BF16)
    ik = w_in[:, _OFF_IK:_OFF_IK + IDX_DIM]
    z64 = jnp.zeros((D_MODEL, IDX_DIM), F32)
    ga = w_in[:, _OFF_GA:_OFF_GA + GLA_GATE_RANK]
    iw = w_in[:, _OFF_IW:_OFF_IW + IDX_HEADS]
    zpad = jnp.zeros((D_MODEL, LANE - GLA_GATE_RANK - IDX_HEADS), F32)
    ws = jnp.concatenate([ik, z64, z64, ik, ga, iw, zpad], axis=1).astype(BF16)
    return norm_w.reshape(1, D_MODEL), wg, wd, ws


def _project(x2d, pw, bm):
    n = x2d.shape[0]
    nw, wg, wd, ws = pw
    row = lambda w: pl.BlockSpec((bm, w), lambda i: (i, 0))
    full = lambda a: pl.BlockSpec(a.shape, lambda i: (0, 0))
    outs = [(_W_GLA, BF16), (DSA_WIDTH, BF16), (DSA_WIDTH, F32), (DSA_WIDTH, F32),
            (DSA_WIDTH, BF16), (DSA_WIDTH, BF16), (IDX_HEADS * IDX_DIM, BF16),
            (LANE, BF16), (LANE, BF16), (IDX_DIM, F32), (LANE, F32)]
    return pl.pallas_call(
        _proj_kernel,
        grid=(n // bm,),
        in_specs=[row(D_MODEL), full(nw), full(wg), full(wd), full(ws)],
        out_specs=[row(w) for w, _ in outs],
        out_shape=[jax.ShapeDtypeStruct((n, w), dt) for w, dt in outs],
        compiler_params=_cparams(("parallel",)),
        name="proj",
    )(x2d, nw, wg, wd, ws)


def _proj_t_kernel(x_ref, nw_ref, wg_ref, wkv_ref, ws_ref, wt_ref, wst_ref,
                   gla_ref, kf_ref, vf_ref, kb_ref, ikf_ref, ikb_ref, g2_ref,
                   qt_ref, iqt_ref, vt_ref, g2t_ref):
    x = x_ref[...]
    ms = jnp.mean(x * x, axis=-1, keepdims=True)
    xn = (x * lax.rsqrt(ms + EPS) * nw_ref[...]).astype(BF16)
    gla_ref[...] = _dot(xn, wg_ref[...]).astype(BF16)
    W = DSA_WIDTH
    dk = _dot(xn, wkv_ref[:, 0:W])
    dv = _dot(xn, wkv_ref[:, W:2 * W])
    kb_ref[...] = dk.astype(BF16)
    for h in range(DSA_HEADS):
        kf_ref[:, h, :] = dk[:, h * DSA_DH:(h + 1) * DSA_DH]
        vf_ref[:, h, :] = dv[:, h * DSA_DH:(h + 1) * DSA_DH]
    sm = _dot(xn, ws_ref[...])
    ikb_ref[...] = sm[:, 0:LANE].astype(BF16)
    ikf_ref[...] = sm[:, 0:IDX_DIM]
    g2_ref[...] = sm[:, LANE:2 * LANE]
    qt_ref[...] = (_dot_nt(wt_ref[0:W, :], xn) * (DSA_DH ** -0.5 * LOG2E)).astype(BF16)
    vt_ref[0] = _dot_nt(wt_ref[W:2 * W, :], xn).astype(BF16)
    iqt_ref[...] = _dot_nt(wt_ref[2 * W:3 * W, :], xn).astype(BF16)
    g2t_ref[...] = _dot_nt(wst_ref[...], xn)


def _proj_t_weights(norm_w, w_in):
    wg = w_in[:, _OFF_GLA:_OFF_GLA + _W_GLA].astype(BF16)
    W = DSA_WIDTH
    wq = w_in[:, _OFF_DSA:_OFF_DSA + W]
    wkv = w_in[:, _OFF_DSA + W:_OFF_DSA + 3 * W]
    wv = w_in[:, _OFF_DSA + 2 * W:_OFF_DSA + 3 * W]
    wiq = w_in[:, _OFF_DSA + 3 * W:_OFF_DSA + 4 * W]
    ik = w_in[:, _OFF_IK:_OFF_IK + IDX_DIM]
    z64 = jnp.zeros((D_MODEL, IDX_DIM), F32)
    ga = w_in[:, _OFF_GA:_OFF_GA + GLA_GATE_RANK]
    iw = w_in[:, _OFF_IW:_OFF_IW + IDX_HEADS]
    zpad = jnp.zeros((D_MODEL, LANE - GLA_GATE_RANK - IDX_HEADS), F32)
    g2w = jnp.concatenate([ga, iw, zpad], axis=1)
    ws = jnp.concatenate([ik, z64, g2w], axis=1).astype(BF16)
    wt = jnp.concatenate([wq, wv, wiq], axis=1).T.astype(BF16)
    return norm_w.reshape(1, D_MODEL), wg, wkv.astype(BF16), ws, wt, g2w.T.astype(BF16)


def _project_t(x2d, pw, bm):
    n = x2d.shape[0]
    row = lambda w: pl.BlockSpec((bm, w), lambda i: (i, 0))
    row3 = pl.BlockSpec((bm, DSA_HEADS, DSA_DH), lambda i: (i, 0, 0))
    col = lambda h: pl.BlockSpec((h, bm), lambda i: (0, i))
    full = lambda a: pl.BlockSpec(a.shape, lambda i: (0, 0))
    out_specs = [row(_W_GLA), row3, row3, row(DSA_WIDTH), row(IDX_DIM), row(LANE), row(LANE),
                 col(DSA_WIDTH), col(IDX_HEADS * IDX_DIM),
                 pl.BlockSpec((1, DSA_WIDTH, bm), lambda i: (i, 0, 0)), col(LANE)]
    out_shape = [jax.ShapeDtypeStruct((n, _W_GLA), BF16),
                 jax.ShapeDtypeStruct((n, DSA_HEADS, DSA_DH), F32),
                 jax.ShapeDtypeStruct((n, DSA_HEADS, DSA_DH), F32),
                 jax.ShapeDtypeStruct((n, DSA_WIDTH), BF16),
                 jax.ShapeDtypeStruct((n, IDX_DIM), F32),
                 jax.ShapeDtypeStruct((n, LANE), BF16),
                 jax.ShapeDtypeStruct((n, LANE), F32),
                 jax.ShapeDtypeStruct((DSA_WIDTH, n), BF16),
                 jax.ShapeDtypeStruct((IDX_HEADS * IDX_DIM, n), BF16),
                 jax.ShapeDtypeStruct((n // bm, DSA_WIDTH, bm), BF16),
                 jax.ShapeDtypeStruct((LANE, n), F32)]
    return pl.pallas_call(
        _proj_t_kernel,
        grid=(n // bm,),
        in_specs=[row(D_MODEL)] + [full(a) for a in pw],
        out_specs=out_specs,
        out_shape=out_shape,
        compiler_params=_cparams(("parallel",)),
        name="proj_t",
    )(x2d, *pw)


def _log_sigmoid(x):
    return jnp.minimum(x, 0.0) - jnp.log1p(jnp.exp(-jnp.abs(x)))


def _gla_kernel(gla_ref, g2_ref, w2_ref, b2_ref, seg_ref, bd_ref, s0_ref,
                o_ref, sT_ref,
                st_scr, kpad, bpad, vpad, qt_scr, kt_scr, dec_scr, oi_scr,
                *, bt, c):
    t = pl.program_id(1)

    @pl.when(t == 0)
    def _():
        st_scr[...] = s0_ref[0]

    q = gla_ref[:, 0:GLA_QK].astype(F32) * (GLA_DK ** -0.5)
    k = gla_ref[:, GLA_QK:2 * GLA_QK].astype(F32)
    vb = gla_ref[:, 2 * GLA_QK:2 * GLA_QK + GLA_WIDTH]
    v = vb.astype(F32)

    ga = g2_ref[...]
    ga_hi = ga.astype(BF16)
    ga_lo = (ga - ga_hi.astype(F32)).astype(BF16)
    logit = (_dot(ga_hi, w2_ref[0]) + _dot(ga_hi, w2_ref[1]) + _dot(ga_lo, w2_ref[0])
             + b2_ref[...])
    lg = _log_sigmoid(logit) * (1.0 / GLA_GATE_TAU)

    shift = c.bit_length() - 1
    row = lax.broadcasted_iota(I32, (bt, bt), 0)
    col = lax.broadcasted_iota(I32, (bt, bt), 1)
    same = (row >> shift) == (col >> shift)
    tri = jnp.where(same & (col <= row), 1.0, 0.0).astype(BF16)
    last = jnp.where(same & ((col & (c - 1)) == c - 1), 1.0, 0.0).astype(BF16)
    l0, l1, l2 = _split3(lg)
    b = _dot(tri, l0) + _dot(tri, l1) + _dot(tri, l2)
    b0, b1, b2s = _split3(b)
    bl = _dot(last, b0) + _dot(last, b1) + _dot(last, b2s)

    zpad = jnp.zeros((c, GLA_QK), F32)
    kpad[0:c, :] = zpad
    bpad[0:c, :] = zpad
    vpad[0:c, :] = jnp.zeros((c, GLA_WIDTH), F32)
    kpad[c:c + bt, :] = k
    bpad[c:c + bt, :] = b
    vpad[c:c + bt, :] = v
    pos = lax.broadcasted_iota(I32, (bt, GLA_QK), 0) & (c - 1)
    seg = seg_ref[...]
    o_intra = jnp.zeros((bt, GLA_WIDTH), F32)
    for d in range(c):
        ks = kpad[c - d:c - d + bt, :]
        bs = bpad[c - d:c - d + bt, :]
        vs = vpad[c - d:c - d + bt, :]
        z = q * ks * jnp.exp(jnp.minimum(b - bs, 0.0))
        z = jnp.where(pos >= d, z, 0.0)
        o_intra = o_intra + _dot(z.astype(BF16), seg) * vs
    oi_scr[...] = o_intra

    qt_scr[...] = q * jnp.exp(b)
    kt_scr[...] = k * jnp.exp(bl - b)
    dec_scr[...] = jnp.exp(bl)
    bd = bd_ref[...]

    def step(ci, carry):
        r0 = pl.multiple_of(ci * c, c)
        qc = qt_scr[pl.ds(r0, c), :].astype(BF16)
        kc = kt_scr[pl.ds(r0, c), :].astype(BF16)
        vc = vpad[pl.ds(r0 + c, c), :].astype(BF16)
        st = st_scr[...]
        o_ref[pl.ds(r0, c), :] = oi_scr[pl.ds(r0, c), :] + _dot_nt(qc, st.astype(BF16))
        dec = dec_scr[pl.ds(r0, 1), :]
        st_scr[...] = st * dec + _dot_tn(vc, kc) * bd
        return carry

    lax.fori_loop(0, bt // c, step, 0)

    @pl.when(t == pl.num_programs(1) - 1)
    def _():
        sT_ref[0] = st_scr[...]


def _gla_consts(w_gla_a2, b_gla_a2):
    w2 = jnp.zeros((LANE, GLA_QK), F32).at[0:GLA_GATE_RANK].set(w_gla_a2)
    w2_hi = w2.astype(BF16)
    w2_lo = (w2 - w2_hi.astype(F32)).astype(BF16)
    w2s = jnp.stack([w2_hi, w2_lo])
    hq = jnp.arange(GLA_QK) // GLA_DK
    hv = jnp.arange(GLA_WIDTH) // GLA_DV
    seg = (hq[:, None] == hv[None, :]).astype(BF16)
    bd = (hv[:, None] == hq[None, :]).astype(F32)
    return w2s, b_gla_a2.reshape(1, GLA_QK), seg, bd


def _state_to_kernel(s):
    n = s.shape[0]
    eye = jnp.eye(GLA_HEADS, dtype=s.dtype)
    t = jnp.swapaxes(s, 2, 3)[:, :, :, None, :] * eye[None, :, None, :, None]
    return t.reshape(n, GLA_WIDTH, GLA_QK)


def _state_from_kernel(t):
    n = t.shape[0]
    t5 = t.reshape(n, GLA_HEADS, GLA_DV, GLA_HEADS, GLA_DK)
    return jnp.einsum("shehd->shde", t5)


def _gla(gla, g2, s0, consts, n_streams, t_len, bt, c):
    w2s, b2, seg, bd = consts
    nb = t_len // bt
    kern = functools.partial(_gla_kernel, bt=bt, c=c)
    const = lambda a: pl.BlockSpec(a.shape, lambda s, t: (0,) * a.ndim)
    o, sT = pl.pallas_call(
        kern,
        grid=(n_streams, nb),
        in_specs=[
            pl.BlockSpec((bt, 2 * GLA_QK + GLA_WIDTH), lambda s, t: (s * nb + t, 0)),
            pl.BlockSpec((bt, LANE), lambda s, t: (s * nb + t, 0)),
            const(w2s), const(b2), const(seg), const(bd),
            pl.BlockSpec((1, GLA_WIDTH, GLA_QK), lambda s, t: (s, 0, 0)),
        ],
        out_specs=[
            pl.BlockSpec((bt, GLA_WIDTH), lambda s, t: (s * nb + t, 0)),
            pl.BlockSpec((1, GLA_WIDTH, GLA_QK), lambda s, t: (s, 0, 0)),
        ],
        out_shape=[
            jax.ShapeDtypeStruct((n_streams * t_len, GLA_WIDTH), F32),
            jax.ShapeDtypeStruct((n_streams, GLA_WIDTH, GLA_QK), F32),
        ],
        scratch_shapes=[
            pltpu.VMEM((GLA_WIDTH, GLA_QK), F32),
            pltpu.VMEM((bt + c, GLA_QK), F32),
            pltpu.VMEM((bt + c, GLA_QK), F32),
            pltpu.VMEM((bt + c, GLA_WIDTH), F32),
            pltpu.VMEM((bt, GLA_QK), F32),
            pltpu.VMEM((bt, GLA_QK), F32),
            pltpu.VMEM((bt, GLA_QK), F32),
            pltpu.VMEM((bt, GLA_WIDTH), F32),
        ],
        compiler_params=_cparams(("arbitrary", "arbitrary")),
        name="gla",
    )(gla, g2, w2s, b2, seg, bd, s0)
    return o, sT


INT_MIN = -2 ** 31
NEG_INF = float("-inf")
KEY_NEG_INF = -2139095041
F32_LOWEST = -3.4028234663852886e38
IDX_SCALE = IDX_DIM ** -0.5 * IDX_HEADS ** -0.5


def _fill_head_weights(wb_scr, g2_ref, bq):
    for h in range(IDX_HEADS):
        c0 = GLA_GATE_RANK + h
        col = g2_ref[:, c0:c0 + 1] * IDX_SCALE
        wb_scr[h] = jnp.broadcast_to(col, (bq, LANE))


def _index_scores(iq_ref, wb_scr, iklo, ikhi, bk):
    parts = [None] * (bk // LANE)
    for p in range(IDX_HEADS // 2):
        iqp = iq_ref[:, p * LANE:(p + 1) * LANE]
        y0 = jnp.maximum(_dot_nt(iqp, iklo), 0.0)
        y1 = jnp.maximum(_dot_nt(iqp, ikhi), 0.0)
        w0 = wb_scr[2 * p]
        w1 = wb_scr[2 * p + 1]
        for c in range(bk // LANE):
            t = y0[:, c * LANE:(c + 1) * LANE] * w0 + y1[:, c * LANE:(c + 1) * LANE] * w1
            parts[c] = t if parts[c] is None else parts[c] + t
    return jnp.concatenate(parts, axis=1)


def _key_to_float(key):
    key = jnp.maximum(key, KEY_NEG_INF)
    return pltpu.bitcast(key ^ ((key >> 31) & 0x7FFFFFFF), F32)


def _kth_largest(count, total, topk, shape):
    def bit_body(i, carry):
        prefix, n_ge = carry
        cand_u = prefix | jnp.left_shift(jnp.int32(1), 31 - i)
        cnt = count(_key_to_float(cand_u ^ INT_MIN), False)
        ok = cnt >= topk
        return jnp.where(ok, cand_u, prefix), jnp.where(ok, cnt, n_ge)

    init = (jnp.zeros(shape, I32), jnp.zeros(shape, I32) + total)
    prefix, n_ge = lax.fori_loop(0, 32, bit_body, init)
    t = _key_to_float(prefix ^ INT_MIN)
    return t, n_ge, count(t, True)


def _count_rows(load_tile, n_tiles, cand, strict, bq, bk):
    cand_b = jnp.broadcast_to(cand, (bq, LANE))

    def body(kt, acc):
        s = load_tile(kt)
        for c in range(bk // LANE):
            sc = s[:, c * LANE:(c + 1) * LANE]
            acc = acc + jnp.where(sc > cand_b if strict else sc >= cand_b, 1, 0)
        return acc

    acc = lax.fori_loop(0, n_tiles, body, jnp.zeros((bq, LANE), I32))
    return jnp.sum(acc, axis=1, keepdims=True)


def _attend_tile(sel, q_ref, kv_tile, m_scr, l_scr, acc_scr):
    for h in range(DSA_HEADS):
        hs = slice(h * DSA_DH, (h + 1) * DSA_DH)
        kh, vh = kv_tile(h)
        lg = jnp.where(sel, _dot_nt(q_ref[:, hs], kh), NEG_BIG)
        m_old = m_scr[h]
        m_new = jnp.maximum(m_old, jnp.max(lg, axis=1, keepdims=True))
        alpha = jnp.exp2(m_old - m_new)
        p = jnp.exp2(lg - m_new)
        l_scr[h] = alpha * l_scr[h] + jnp.sum(p, axis=1, keepdims=True)
        acc_scr[:, hs] = alpha * acc_scr[:, hs] + _dot(p.astype(BF16), vh)
        m_scr[h] = m_new


def _select(s, t, n_gt, eq_before, topk, upper, ties):
    if not ties:
        return s >= jnp.maximum(t, F32_LOWEST), eq_before
    eq = s == t
    eq_f = jnp.where(eq, 1.0, 0.0).astype(BF16)
    rank = eq_before + _dot(eq_f, upper)
    need = (topk - n_gt).astype(F32)
    sel = ((s > t) | (eq & (rank < need))) & (s > NEG_INF)
    return sel, eq_before + jnp.sum(eq_f.astype(F32), axis=1, keepdims=True)


def _init_softmax(m_scr, l_scr, acc_scr):
    m_scr[...] = jnp.full(m_scr.shape, NEG_BIG, F32)
    l_scr[...] = jnp.zeros(l_scr.shape, F32)
    acc_scr[...] = jnp.zeros(acc_scr.shape, F32)


def _finish_softmax(o_ref, l_scr, acc_scr):
    for h in range(DSA_HEADS):
        hs = slice(h * DSA_DH, (h + 1) * DSA_DH)
        o_ref[:, hs] = (acc_scr[:, hs] / l_scr[h]).astype(o_ref.dtype)


def _upper_ones(n):
    r = jnp.arange(n)
    return (r[:, None] < r[None, :]).astype(BF16)


DSA_BQ = 256
DSA_BK = 1024
V_AHEAD = 3
V_RING = V_AHEAD + 1

def _count_cols(sc_scr, n_tiles, cand, strict, bq, bk):
    cand_b = jnp.broadcast_to(cand, (SUBLANE, bq))
    n_acc = 4

    def body(kt, accs):
        accs = list(accs)
        for r in range(bk // SUBLANE):
            s = sc_scr[kt, r * SUBLANE:(r + 1) * SUBLANE, :]
            accs[r % n_acc] = accs[r % n_acc] + jnp.where(s > cand_b if strict else s >= cand_b, 1, 0)
        return tuple(accs)

    zero = jnp.zeros((SUBLANE, bq), I32)
    accs = lax.fori_loop(0, n_tiles, body, (zero,) * n_acc)
    return jnp.sum((accs[0] + accs[1]) + (accs[2] + accs[3]), axis=0, keepdims=True)


def _max_min_cols(sc_scr, n_tiles, bq, bk):
    def body(kt, carry):
        mx, mn = carry
        for r in range(bk // SUBLANE):
            s = sc_scr[kt, r * SUBLANE:(r + 1) * SUBLANE, :]
            mx = jnp.maximum(mx, s)
            mn = jnp.minimum(mn, jnp.where(s > NEG_INF, s, -NEG_INF))
        return mx, mn

    init = (jnp.full((SUBLANE, bq), NEG_INF, F32), jnp.full((SUBLANE, bq), -NEG_INF, F32))
    mx, mn = lax.fori_loop(0, n_tiles, body, init)
    return jnp.max(mx, axis=0, keepdims=True), jnp.min(mn, axis=0, keepdims=True)


BISECT_WARMUP = 12
BISECT_CAP = 28


def _threshold(count, sc_scr, n_tiles, topk, bq, bk, res_f, res_i):
    zero = jnp.zeros((1, bq), F32)
    n_pos = count(zero, True)
    n_nn = count(zero, False)
    n_fin = count(jnp.full((1, bq), NEG_INF, F32), True)
    mx, mn = _max_min_cols(sc_scr, n_tiles, bq, bk)
    total = n_tiles * bk
    pos = n_pos >= topk
    at_zero = jnp.logical_not(pos) & (n_nn >= topk)
    none = n_fin < topk
    lo = jnp.where(pos | at_zero, zero, jnp.where(none, NEG_INF, mn))
    n_lo = jnp.where(pos | at_zero, n_nn, jnp.where(none, total, n_fin))
    hi = jnp.where(pos, mx * 2.0, zero)
    n_hi = jnp.where(pos, 0, jnp.where(at_zero, n_pos, n_nn))
    done = jnp.where(at_zero | none | (n_lo == topk), 1, 0)

    def movable(lo, hi, done):
        mid = lo + (hi - lo) * 0.5
        return (done == 0) & (mid > lo) & (mid < hi), mid

    def step(state):
        lo, hi, n_lo, n_hi, done = state
        ok, mid = movable(lo, hi, done)
        cnt = count(mid, False)
        up = ok & (cnt >= topk)
        down = ok & (cnt < topk)
        lo = jnp.where(up, mid, lo)
        n_lo = jnp.where(up, cnt, n_lo)
        hi = jnp.where(down, mid, hi)
        n_hi = jnp.where(down, cnt, n_hi)
        return lo, hi, n_lo, n_hi, jnp.where(n_lo == topk, 1, done)

    def any_movable(state):
        return jnp.max(jnp.where(movable(state[0], state[1], state[4])[0], 1, 0))

    state = lax.fori_loop(0, BISECT_WARMUP, lambda i, s: step(s), (lo, hi, n_lo, n_hi, done))

    def cond(c):
        return (c[0] < BISECT_CAP) & (c[1] > 0)

    def body(c):
        s = step(step(c[2]))
        return c[0] + 2, any_movable(s), s

    _, _, state = lax.while_loop(cond, body, (jnp.int32(BISECT_WARMUP), any_movable(state), state))
    lo, hi, n_lo, n_hi, done = state
    res_f[0:1, :] = lo
    res_i[0:1, :] = n_lo
    res_i[1:2, :] = n_hi

    @pl.when(jnp.min(done) == 0)
    def _():
        t, n_ge, n_gt = _kth_largest(count, total, topk, (1, bq))
        res_f[0:1, :] = t
        res_i[0:1, :] = n_ge
        res_i[1:2, :] = n_gt

    return res_f[0:1, :], res_i[0:1, :], res_i[1:2, :]


def _dsa_prompt_t_kernel(qt_ref, iqt_ref, g2t_ref, ik_ref, lower_ref, k_ref, vt_hbm,
                         o_ref, sc_scr, vbuf, sem, acc_scr, p_scr, res_f, res_i,
                         *, bq, bk, topk):
    qb = pl.program_id(0)
    q0 = qb * bq
    n_tiles = (q0 + bq + bk - 1) // bk
    sub = LANE

    def tile_scores(kt, masked):
        for s in range(bk // sub):
            ik_s = ik_ref[kt, s * sub:(s + 1) * sub, 0:IDX_DIM]
            acc = None
            for h in range(IDX_HEADS):
                y = _dot(ik_s, iqt_ref[h * IDX_DIM:(h + 1) * IDX_DIM, :])
                w = g2t_ref[GLA_GATE_RANK + h:GLA_GATE_RANK + h + 1, :] * IDX_SCALE
                t = jnp.maximum(y, 0.0) * w
                acc = t if acc is None else acc + t
            if masked:
                kpos = kt * bk + s * sub + lax.broadcasted_iota(I32, (sub, bq), 0)
                qpos = q0 + lax.broadcasted_iota(I32, (sub, bq), 1)
                shift = CHUNK.bit_length() - 1
                acc = jnp.where((kpos >> shift) <= (qpos >> shift), acc, NEG_INF)
            sc_scr[kt, s * sub:(s + 1) * sub, :] = acc

    def score_body(kt, carry):
        tile_scores(kt, False)
        return carry

    lax.fori_loop(0, n_tiles - 1, score_body, 0)
    tile_scores(n_tiles - 1, True)

    count = lambda cand, strict: _count_cols(sc_scr, n_tiles, cand, strict, bq, bk)
    t, n_ge, n_gt = _threshold(count, sc_scr, n_tiles, topk, bq, bk, res_f, res_i)
    def v_copy(kt):
        slot = lax.rem(kt, V_RING)
        return pltpu.make_async_copy(vt_hbm.at[kt], vbuf.at[slot], sem.at[slot])

    def v_prefetch(kt):
        @pl.when(kt < n_tiles)
        def _():
            v_copy(kt).start()

    for j in range(V_AHEAD):
        v_prefetch(j)

    tied = (n_ge > topk) & (t > NEG_INF)

    @pl.when(jnp.max(jnp.where(tied, 1, 0)) > 0)
    def _():
        need = (topk - n_gt).astype(F32)

        def drop_body(kt, eq_before):
            s = sc_scr[kt]
            eq = (s == t) & tied
            eq_b = jnp.where(eq, 1.0, 0.0).astype(BF16)
            rank = eq_before + _dot(lower_ref[...], eq_b)
            sc_scr[kt] = jnp.where(eq & (rank >= need), NEG_INF, s)
            return eq_before + jnp.sum(eq_b.astype(F32), axis=0, keepdims=True)

        lax.fori_loop(0, n_tiles, drop_body, jnp.zeros((1, bq), F32))

    thr = jnp.maximum(t, F32_LOWEST)

    def logits(kt, h):
        hs = slice(h * DSA_DH, (h + 1) * DSA_DH)
        return jnp.where(sc_scr[kt] >= thr, _dot(k_ref[kt, :, hs], qt_ref[hs, :]), NEG_BIG)

    def max_body(kt, ms):
        return tuple(jnp.maximum(ms[h], jnp.max(logits(kt, h), axis=0, keepdims=True))
                     for h in range(DSA_HEADS))

    ms = lax.fori_loop(0, n_tiles, max_body, (jnp.full((1, bq), NEG_BIG, F32),) * DSA_HEADS)

    acc_scr[...] = jnp.zeros(acc_scr.shape, F32)

    def sum_body(kt, ls):
        v_prefetch(kt + V_AHEAD)
        new_ls = []
        for h in range(DSA_HEADS):
            p = jnp.exp2(logits(kt, h) - ms[h])
            new_ls.append(ls[h] + jnp.sum(p, axis=0, keepdims=True))
            p_scr[h] = p.astype(BF16)
        v_copy(kt).wait()
        slot = lax.rem(kt, V_RING)
        for h in range(DSA_HEADS):
            hs = slice(h * DSA_DH, (h + 1) * DSA_DH)
            acc_scr[hs, :] += _dot(vbuf[slot, hs, :], p_scr[h])
        return tuple(new_ls)

    ls = lax.fori_loop(0, n_tiles, sum_body, (jnp.zeros((1, bq), F32),) * DSA_HEADS)
    for h in range(DSA_HEADS):
        hs = slice(h * DSA_DH, (h + 1) * DSA_DH)
        acc_scr[hs, :] = acc_scr[hs, :] / ls[h]
    o_ref[...] = jnp.transpose(acc_scr[...]).astype(o_ref.dtype)


def _dsa_prompt_t(qt, iqt, g2t, ikb, kb, vt3, bq, bk):
    t_len = qt.shape[1]
    n_kt = t_len // bk
    assert vt3.shape == (n_kt, DSA_WIDTH, bk) and bk % bq == 0
    topk = min(IDX_TOPK_MAX, t_len // 4)
    r = jnp.arange(bk)
    lower = (r[None, :] < r[:, None]).astype(BF16)
    kern = functools.partial(_dsa_prompt_t_kernel, bq=bq, bk=bk, topk=topk)
    col = lambda h: pl.BlockSpec((h, bq), lambda i: (0, i))
    ik3 = ikb.reshape(n_kt, bk, LANE)
    k3 = kb.reshape(n_kt, bk, DSA_WIDTH)
    return pl.pallas_call(
        kern,
        grid=(t_len // bq,),
        in_specs=[col(DSA_WIDTH), col(IDX_HEADS * IDX_DIM), col(LANE),
                  pl.BlockSpec(ik3.shape, lambda i: (0, 0, 0)),
                  pl.BlockSpec(lower.shape, lambda i: (0, 0)),
                  pl.BlockSpec(k3.shape, lambda i: (0, 0, 0)),
                  pl.BlockSpec(memory_space=pl.ANY)],
        out_specs=pl.BlockSpec((bq, DSA_WIDTH), lambda i: (i, 0)),
        out_shape=jax.ShapeDtypeStruct((t_len, DSA_WIDTH), BF16),
        scratch_shapes=[
            pltpu.VMEM((n_kt, bk, bq), F32),
            pltpu.VMEM((V_RING, DSA_WIDTH, bk), BF16),
            pltpu.SemaphoreType.DMA((V_RING,)),
            pltpu.VMEM((DSA_WIDTH, bq), F32),
            pltpu.VMEM((DSA_HEADS, bk, bq), BF16),
            pltpu.VMEM((SUBLANE, bq), F32),
            pltpu.VMEM((SUBLANE, bq), I32),
        ],
        compiler_params=_cparams(("arbitrary",)),
        name="dsa_prompt_t",
    )(qt, iqt, g2t, ik3, lower, k3, vt3)


def _dsa_sample_kernel(q_ref, iq_ref, g2_ref, ck_ref, cv_ref, cilo_ref, cihi_ref,
                       nk_ref, nv_ref, nilo_ref, nihi_ref, upper_ref, o_ref,
                       sc_scr, wb_scr, m_scr, l_scr, acc_scr, pk_scr, pv_scr, plo_scr, phi_scr,
                       *, bq, bk, past, topk):
    n_cache = past // bk
    sub = bk // LANE
    _fill_head_weights(wb_scr, g2_ref, bq)
    pk_scr[...] = jnp.zeros(pk_scr.shape, BF16)
    pv_scr[...] = jnp.zeros(pv_scr.shape, BF16)
    plo_scr[...] = jnp.zeros(plo_scr.shape, BF16)
    phi_scr[...] = jnp.zeros(phi_scr.shape, BF16)
    pk_scr[0:bq, :] = nk_ref[...]
    pv_scr[0:bq, :] = nv_ref[...]
    plo_scr[0:bq, :] = nilo_ref[...]
    phi_scr[0:bq, :] = nihi_ref[...]

    for kt in range(n_cache):
        s = _index_scores(iq_ref, wb_scr, cilo_ref[0, kt * bk:(kt + 1) * bk, :],
                          cihi_ref[0, kt * bk:(kt + 1) * bk, :], bk)
        for c in range(sub):
            sc_scr[kt * sub + c] = s[:, c * LANE:(c + 1) * LANE]
    s = _index_scores(iq_ref, wb_scr, plo_scr[...], phi_scr[...], LANE)
    real = lax.broadcasted_iota(I32, (bq, LANE), 1) < bq
    sc_scr[n_cache * sub] = jnp.where(real, s, NEG_INF)

    n_sub = n_cache * sub + 1
    count = lambda cand, strict: _count_rows(lambda i: sc_scr[i], n_sub, cand, strict, bq, LANE)
    t, n_ge, n_gt = _kth_largest(count, n_sub * LANE, topk, (bq, 1))
    tie_rows = jnp.where((n_ge > topk) & (t > NEG_INF), 1, 0)
    any_tie = jnp.max(tie_rows) > 0

    def attend(ties):
        _init_softmax(m_scr, l_scr, acc_scr)
        eq_before = jnp.zeros((bq, 1), F32)
        for kt in range(n_cache):
            keys = jnp.concatenate([sc_scr[kt * sub + c] for c in range(sub)], axis=1)
            upper = upper_ref[...] if ties else None
            sel, eq_before = _select(keys, t, n_gt, eq_before, topk, upper, ties)
            kv_tile = lambda h, kt=kt: (
                ck_ref[0, pl.ds(kt * bk * DSA_HEADS + h, bk, stride=DSA_HEADS), :].astype(BF16),
                cv_ref[0, pl.ds(kt * bk * DSA_HEADS + h, bk, stride=DSA_HEADS), :].astype(BF16))
            _attend_tile(sel, q_ref, kv_tile, m_scr, l_scr, acc_scr)
        upper = upper_ref[0:LANE, 0:LANE] if ties else None
        sel, eq_before = _select(sc_scr[n_cache * sub], t, n_gt, eq_before, topk, upper, ties)
        kv_tile = lambda h: (pk_scr[:, h * DSA_DH:(h + 1) * DSA_DH],
                             pv_scr[:, h * DSA_DH:(h + 1) * DSA_DH])
        _attend_tile(sel, q_ref, kv_tile, m_scr, l_scr, acc_scr)
        _finish_softmax(o_ref, l_scr, acc_scr)

    @pl.when(jnp.logical_not(any_tie))
    def _():
        attend(False)

    @pl.when(any_tie)
    def _():
        attend(True)


def _dsa_sample(q, iq, g2, cache_k, cache_v, cache_ilo, cache_ihi, kb, vb, iklo, ikhi, bq, bk):
    n_streams, past = cache_k.shape[0:2]
    cache_k = cache_k.reshape(n_streams, past * DSA_HEADS, DSA_DH)
    cache_v = cache_v.reshape(n_streams, past * DSA_HEADS, DSA_DH)
    assert past % CHUNK == 0 and bq <= CHUNK and past % bk == 0
    topk = min(IDX_TOPK_MAX, (past + bq) // 4)
    upper = _upper_ones(bk)
    kern = functools.partial(_dsa_sample_kernel, bq=bq, bk=bk, past=past, topk=topk)
    row = lambda w: pl.BlockSpec((bq, w), lambda s: (s, 0))
    cache = lambda w: pl.BlockSpec((1, past, w), lambda s: (s, 0, 0))
    cache_kv = pl.BlockSpec((1, past * DSA_HEADS, DSA_DH), lambda s: (s, 0, 0))
    n_sub = past // LANE + 1
    return pl.pallas_call(
        kern,
        grid=(n_streams,),
        in_specs=[row(DSA_WIDTH), row(IDX_HEADS * IDX_DIM), row(LANE),
                  cache_kv, cache_kv, cache(LANE), cache(LANE),
                  row(DSA_WIDTH), row(DSA_WIDTH), row(LANE), row(LANE),
                  pl.BlockSpec(upper.shape, lambda s: (0, 0))],
        out_specs=row(DSA_WIDTH),
        out_shape=jax.ShapeDtypeStruct((n_streams * bq, DSA_WIDTH), BF16),
        scratch_shapes=[
            pltpu.VMEM((n_sub, bq, LANE), F32),
            pltpu.VMEM((IDX_HEADS, bq, LANE), F32),
            pltpu.VMEM((DSA_HEADS, bq, 1), F32),
            pltpu.VMEM((DSA_HEADS, bq, 1), F32),
            pltpu.VMEM((bq, DSA_WIDTH), F32),
            pltpu.VMEM((LANE, DSA_WIDTH), BF16),
            pltpu.VMEM((LANE, DSA_WIDTH), BF16),
            pltpu.VMEM((LANE, LANE), BF16),
            pltpu.VMEM((LANE, LANE), BF16),
        ],
        compiler_params=_cparams(("parallel",)),
        name="dsa_sample",
    )(q, iq, g2, cache_k, cache_v, cache_ilo, cache_ihi, kb, vb, iklo, ikhi, upper)


def _mix_kernel(og_ref, gg_ref, od_ref, x_ref, gnw_ref, wo_ref, nfw_ref, rw_ref, rb_ref,
                h1_ref, xn_ref, re_ref, rg_ref, cnt_ref):
    g = gg_ref[...].astype(F32)
    gate = g / (1.0 + jnp.exp(-g))
    acc = _dot(od_ref[...], wo_ref[GLA_WIDTH:GLA_WIDTH + DSA_WIDTH, :])
    for h in range(GLA_HEADS):
        hs = slice(h * GLA_DV, (h + 1) * GLA_DV)
        oh = og_ref[:, hs]
        ms = jnp.mean(oh * oh, axis=-1, keepdims=True)
        a = oh * lax.rsqrt(ms + EPS) * gnw_ref[...] * gate[:, hs]
        acc = acc + _dot(a.astype(BF16), wo_ref[hs, :])
    h1 = x_ref[...] + acc
    h1_ref[...] = h1
    ms = jnp.mean(h1 * h1, axis=-1, keepdims=True)
    xn = h1 * lax.rsqrt(ms + EPS) * nfw_ref[...]
    _rows_to_tiles(xn_ref, xn)
    xn_hi = xn.astype(BF16)
    xn_lo = (xn - xn_hi.astype(F32)).astype(BF16)
    logits = (_dot(xn_hi, rw_ref[0]) + _dot(xn_hi, rw_ref[1]) + _dot(xn_lo, rw_ref[0])
              + rb_ref[...])
    lane = lax.broadcasted_iota(I32, logits.shape, 1)
    lane_f = lane.astype(F32)
    work = logits
    e_out = jnp.zeros(logits.shape, I32)
    tops, hots = [], []
    for k in range(TOP_K):
        mx = jnp.max(work, axis=1, keepdims=True)
        idx = jnp.min(jnp.where(work == mx, lane_f, float(LANE)), axis=1, keepdims=True)
        idx = idx.astype(I32)
        hot = lane == idx
        e_out = jnp.where(lane == k, idx, e_out)
        tops.append(mx)
        hots.append(hot)
        work = jnp.where(hot, NEG_BIG, work)
    ex = [jnp.exp(v - tops[0]) for v in tops]
    den = ex[0] + ex[1] + ex[2] + ex[3]
    g_out = jnp.zeros(logits.shape, F32)
    for k in range(TOP_K):
        g_out = jnp.where(lane == k, ex[k] / den, g_out)
    rg_ref[...] = g_out
    bm = logits.shape[0]
    earlier = (lax.broadcasted_iota(I32, (bm, bm), 1) < lax.broadcasted_iota(I32, (bm, bm), 0))
    earlier = jnp.where(earlier, 1.0, 0.0).astype(BF16)
    run = jnp.zeros((1, LANE), F32)
    for k in range(TOP_K):
        hot_f = jnp.where(hots[k], 1.0, 0.0)
        before = _dot(earlier, hot_f.astype(BF16)) + run
        rank = jnp.sum(jnp.where(hots[k], before, 0.0), axis=1, keepdims=True).astype(I32)
        e_out = jnp.where(lane == TOP_K + k, rank, e_out)
        run = run + jnp.sum(hot_f, axis=0, keepdims=True)
    re_ref[...] = e_out
    cnt_ref[0] = jnp.broadcast_to(run, (SUBLANE, LANE)).astype(I32)


def _mix_consts(gla_norm_w, w_out, norm_ffn_w, router_w, router_b):
    rw = jnp.zeros((D_MODEL, LANE), F32).at[:, 0:N_EXPERTS].set(router_w)
    rw_hi = rw.astype(BF16)
    rw_lo = (rw - rw_hi.astype(F32)).astype(BF16)
    rb = jnp.full((1, LANE), NEG_BIG, F32).at[0, 0:N_EXPERTS].set(router_b)
    return (gla_norm_w.reshape(1, GLA_DV), w_out.astype(BF16), norm_ffn_w.reshape(1, D_MODEL),
            jnp.stack([rw_hi, rw_lo]), rb)


def _mix(og, gla, od, x2d, consts, bm):
    n = x2d.shape[0]
    gnw, wo, nfw, rws, rb = consts
    row = lambda w: pl.BlockSpec((bm, w), lambda i: (i, 0))
    const = lambda a: pl.BlockSpec(a.shape, lambda i: (0,) * a.ndim)
    gg_col = (2 * GLA_QK + GLA_WIDTH) // GLA_WIDTH
    return pl.pallas_call(
        _mix_kernel,
        grid=(n // bm,),
        in_specs=[row(GLA_WIDTH), pl.BlockSpec((bm, GLA_WIDTH), lambda i: (i, gg_col)),
                  row(DSA_WIDTH), row(D_MODEL),
                  const(gnw), const(wo), const(nfw), const(rws), const(rb)],
        out_specs=[row(D_MODEL), pl.BlockSpec((bm * ROW_TILE, LANE), lambda i: (i, 0)),
                   row(LANE), row(LANE), pl.BlockSpec((1, SUBLANE, LANE), lambda i: (i, 0, 0))],
        out_shape=[jax.ShapeDtypeStruct((n, D_MODEL), F32),
                   jax.ShapeDtypeStruct((n * ROW_TILE, LANE), F32),
                   jax.ShapeDtypeStruct((n, LANE), I32), jax.ShapeDtypeStruct((n, LANE), F32),
                   jax.ShapeDtypeStruct((n // bm, SUBLANE, LANE), I32)],
        compiler_params=_cparams(("parallel",)),
        name="mix",
    )(og, gla, od, x2d, gnw, wo, nfw, rws, rb)


MOE_ROWS = 256
DMA_UNROLL = 8
TOKEN_BLOCK = 512


def _route_tables(counts, br, n_pairs):
    c = counts[:, 0, :]
    tot = jnp.sum(c, axis=0)
    padded = (tot + br - 1) // br * br
    pad_end = jnp.cumsum(padded)
    pad_start = pad_end - padded
    base = pad_start[None, :] + jnp.cumsum(c, axis=0) - c
    base = jnp.broadcast_to(base[:, None, :], counts.shape).astype(I32)
    nb = -(-(n_pairs + N_EXPERTS * (br - 1)) // br)
    n_used = (pad_end[N_EXPERTS - 1] // br).astype(I32)
    blk = jnp.arange(nb, dtype=I32)
    last = jnp.minimum(blk, n_used - 1) * br
    block_e = jnp.sum((pad_end[None, 0:N_EXPERTS] <= last[:, None]).astype(I32), axis=1)
    ends = pad_end[0:N_EXPERTS].astype(I32)
    tots = jnp.concatenate([tot[0:N_EXPERTS].astype(I32), n_used.reshape(1)])
    return base, jnp.minimum(block_e, N_EXPERTS - 1), n_used.reshape(1), ends, tots


def _pos_kernel(re_ref, base_ref, pos_ref):
    re = re_ref[...]
    lane = lax.broadcasted_iota(I32, re.shape, 1)
    base_row = base_ref[0, 0:1, :].astype(F32)
    p = jnp.zeros(re.shape, F32)
    for k in range(TOP_K):
        hot = lane == re[:, k:k + 1]
        first = jnp.sum(jnp.where(hot, base_row, 0.0), axis=1, keepdims=True)
        p = jnp.where(lane == k, first + re[:, TOP_K + k:TOP_K + k + 1].astype(F32), p)
    pos_ref[0] = jnp.transpose(p)[0:SUBLANE, :].astype(I32) * ROW_TILE


def _positions(re, base, bm):
    n = re.shape[0]
    return pl.pallas_call(
        _pos_kernel,
        grid=(n // bm,),
        in_specs=[pl.BlockSpec((bm, LANE), lambda i: (i, 0)),
                  pl.BlockSpec((1, SUBLANE, LANE), lambda i: (i, 0, 0))],
        out_specs=pl.BlockSpec((1, SUBLANE, bm), lambda i: (i, 0, 0)),
        out_shape=jax.ShapeDtypeStruct((n // bm, SUBLANE, bm), I32),
        compiler_params=_cparams(("parallel",)),
        name="positions",
    )(re, base)


def _dispatch_kernel(end_ref, tot_ref, pos_ref, xa_ref, xb_ref, xr_hbm, zero_scr, sem,
                     *, bt, br, nsteps_a):
    i = pl.program_id(0)

    @pl.when(i == 0)
    def _():
        zero_scr[...] = jnp.zeros(zero_scr.shape, F32)
        n_used = tot_ref[N_EXPERTS]
        nb = xr_hbm.shape[0] // (br * ROW_TILE)

        def fills():
            for e in range(N_EXPERTS):
                yield tot_ref[e] > 0, end_ref[e] - br
            for j in range(N_EXPERTS):
                yield n_used + j < nb, jnp.minimum(n_used + j, nb - 1) * br

        for act in ("start", "wait"):
            for cond, row0 in fills():
                @pl.when(cond)
                def _(row0=row0, act=act):
                    dst = xr_hbm.at[pl.ds(pl.multiple_of(row0 * ROW_TILE, ROW_TILE), br * ROW_TILE)]
                    c = pltpu.make_async_copy(zero_scr, dst, sem.at[1])
                    c.start() if act == "start" else c.wait()

    def scatter(x_ref):
        for k in range(TOP_K):
            def body(r2, c, k=k):
                for pr in range(2):
                    r = r2 * 2 + pr
                    src = x_ref.at[pl.ds(pl.multiple_of(r * ROW_TILE, ROW_TILE), ROW_TILE)]
                    dst = xr_hbm.at[pl.ds(pl.multiple_of(pos_ref[0, k, r], ROW_TILE), ROW_TILE)]
                    pltpu.make_async_copy(src, dst, sem.at[0]).start(priority=pr)
                return c
            lax.fori_loop(0, bt // 2, body, 0, unroll=DMA_UNROLL // 2)

    @pl.when(i < nsteps_a)
    def _():
        scatter(xa_ref)

    @pl.when(i >= nsteps_a)
    def _():
        scatter(xb_ref)

    done = xr_hbm.at[pl.ds(0, TOP_K * bt * ROW_TILE)]
    pltpu.make_async_copy(done, done, sem.at[0]).wait()


def _dispatch(xn_a, xn_b, pos_t, pad_end, tot, n_rows, bt, br):
    na, nb_ = xn_a.shape[0] // (bt * ROW_TILE), xn_b.shape[0] // (bt * ROW_TILE)
    kern = functools.partial(_dispatch_kernel, bt=bt, br=br, nsteps_a=na)
    grid_spec = pltpu.PrefetchScalarGridSpec(
        num_scalar_prefetch=2,
        grid=(na + nb_,),
        in_specs=[
            pl.BlockSpec((1, SUBLANE, bt), lambda i, e, t: (i, 0, 0), memory_space=pltpu.SMEM),
            pl.BlockSpec((bt * ROW_TILE, LANE), lambda i, e, t: (jnp.minimum(i, na - 1), 0)),
            pl.BlockSpec((bt * ROW_TILE, LANE), lambda i, e, t: (jnp.maximum(i - na, 0), 0)),
        ],
        out_specs=pl.BlockSpec(memory_space=pl.ANY),
        scratch_shapes=[pltpu.VMEM((br * ROW_TILE, LANE), F32), pltpu.SemaphoreType.DMA((2,))],
    )
    return pl.pallas_call(
        kern,
        grid_spec=grid_spec,
        out_shape=jax.ShapeDtypeStruct((n_rows * ROW_TILE, LANE), F32),
        compiler_params=_cparams(("arbitrary",)),
        name="dispatch",
    )(pad_end, tot, pos_t, xn_a, xn_b)


def _rows_to_tiles(ref, x):
    n = x.shape[0]
    for j in range(ROW_TILE):
        ref[pl.ds(j, n, stride=ROW_TILE), :] = x[:, j * LANE:(j + 1) * LANE]


def _tile_chunk(ref, j, n, row0=0):
    return ref[pl.ds(row0 * ROW_TILE + j, n, stride=ROW_TILE), :]


def _rows_from_tiles(ref, n):
    return jnp.concatenate([_tile_chunk(ref, j, n) for j in range(ROW_TILE)], axis=1)


def _moe_kernel(be_ref, nu_ref, x_ref, wgu_ref, bgu_ref, wd_ref, bdn_ref,
                y_ref, wgu_bf, wd_bf):
    i = pl.program_id(0)
    n_used = nu_ref[0]

    @pl.when(i < n_used)
    def _():
        e = be_ref[i]
        prev = be_ref[jnp.maximum(i - 1, 0)]

        @pl.when((i == 0) | (e != prev))
        def _():
            wgu_bf[...] = wgu_ref[0].astype(BF16)
            wd_bf[...] = wd_ref[0].astype(BF16)

        xb = _rows_from_tiles(x_ref, x_ref.shape[0] // ROW_TILE).astype(BF16)
        gu = _dot(xb, wgu_bf[...]) + bgu_ref[0]
        g_lin = jnp.minimum(gu[:, 0:D_FF], SWIGLU_LIMIT)
        u_lin = jnp.clip(gu[:, D_FF:2 * D_FF], -SWIGLU_LIMIT, SWIGLU_LIMIT)
        act = g_lin / (1.0 + jnp.exp(-SWIGLU_ALPHA * g_lin)) * (u_lin + 1.0)
        y = _dot(act.astype(BF16), wd_bf[...]) + bdn_ref[0]
        _rows_to_tiles(y_ref, y)

    @pl.when(i >= n_used)
    def _():
        y_ref[...] = jnp.zeros(y_ref.shape, y_ref.dtype)


def _moe(x_rows, block_e, n_used, w_gu, b_gu, w_down, b_down, br):
    nb = x_rows.shape[0] // (br * ROW_TILE)
    grid_spec = pltpu.PrefetchScalarGridSpec(
        num_scalar_prefetch=2,
        grid=(nb,),
        in_specs=[
            pl.BlockSpec((br * ROW_TILE, LANE),
                         lambda i, be, nu: (jnp.minimum(i, nu[0] - 1), 0)),
            pl.BlockSpec((1, D_MODEL, 2 * D_FF), lambda i, be, nu: (be[i], 0, 0)),
            pl.BlockSpec((1, 1, 2 * D_FF), lambda i, be, nu: (be[i], 0, 0)),
            pl.BlockSpec((1, D_FF, D_MODEL), lambda i, be, nu: (be[i], 0, 0)),
            pl.BlockSpec((1, 1, D_MODEL), lambda i, be, nu: (be[i], 0, 0)),
        ],
        out_specs=pl.BlockSpec((br * ROW_TILE, LANE), lambda i, be, nu: (i, 0)),
        scratch_shapes=[
            pltpu.VMEM((D_MODEL, 2 * D_FF), BF16),
            pltpu.VMEM((D_FF, D_MODEL), BF16),
        ],
    )
    return pl.pallas_call(
        _moe_kernel,
        grid_spec=grid_spec,
        out_shape=jax.ShapeDtypeStruct((nb * br * ROW_TILE, LANE), F32),
        compiler_params=_cparams(("arbitrary",)),
        name="moe",
    )(block_e, n_used, x_rows, w_gu, b_gu.reshape(N_EXPERTS, 1, 2 * D_FF),
      w_down, b_down.reshape(N_EXPERTS, 1, D_MODEL))


def _combine_kernel(pos_ref, posn_ref, y_hbm, h1_ref, rg_ref, fw_ref, o_ref, yg, sem, *, bt):
    i = pl.program_id(0)
    nsteps = pl.num_programs(0)
    slot = lax.rem(i, 2)

    def issue(pref, s):
        for k in range(TOP_K):
            def body(r2, c, k=k):
                for pr in range(2):
                    r = r2 * 2 + pr
                    src = y_hbm.at[pl.ds(pl.multiple_of(pref[0, k, r], ROW_TILE), ROW_TILE)]
                    row0 = pl.multiple_of((k * bt + r) * ROW_TILE, ROW_TILE)
                    pltpu.make_async_copy(src, yg.at[s, pl.ds(row0, ROW_TILE)],
                                          sem.at[s]).start(priority=pr)
                return c
            lax.fori_loop(0, bt // 2, body, 0, unroll=DMA_UNROLL // 2)

    @pl.when(i == 0)
    def _():
        issue(pos_ref, 0)

    @pl.when(i + 1 < nsteps)
    def _():
        issue(posn_ref, 1 - slot)

    pltpu.make_async_copy(yg.at[slot], yg.at[slot], sem.at[slot]).wait()
    gates = [jnp.broadcast_to(rg_ref[:, k:k + 1], (bt, LANE)) for k in range(TOP_K)]
    ss = jnp.zeros((bt, LANE), F32)
    rows = yg.at[slot]
    for j in range(ROW_TILE):
        js = slice(j * LANE, (j + 1) * LANE)
        a = h1_ref[:, js]
        for k in range(TOP_K):
            a = a + gates[k] * _tile_chunk(rows, j, bt, k * bt)
        ss = ss + a * a
        o_ref[:, js] = a
    ms = jnp.sum(ss, axis=-1, keepdims=True) * (1.0 / D_MODEL)
    o_ref[...] = o_ref[...] * lax.rsqrt(ms + EPS) * fw_ref[...]


def _combine(y_rows, pos_t, h1, rg, final_w, bt):
    n = h1.shape[0]
    nsteps = n // bt
    kern = functools.partial(_combine_kernel, bt=bt)
    row = lambda w: pl.BlockSpec((bt, w), lambda i: (i, 0))
    return pl.pallas_call(
        kern,
        grid=(nsteps,),
        in_specs=[
            pl.BlockSpec((1, SUBLANE, bt), lambda i: (i, 0, 0), memory_space=pltpu.SMEM),
            pl.BlockSpec((1, SUBLANE, bt), lambda i: (jnp.minimum(i + 1, nsteps - 1), 0, 0),
                         memory_space=pltpu.SMEM),
            pl.BlockSpec(memory_space=pl.ANY),
            row(D_MODEL), row(LANE),
            pl.BlockSpec((1, D_MODEL), lambda i: (0, 0)),
        ],
        out_specs=row(D_MODEL),
        out_shape=jax.ShapeDtypeStruct((n, D_MODEL), F32),
        scratch_shapes=[pltpu.VMEM((2, TOP_K * bt * ROW_TILE, LANE), F32),
                        pltpu.SemaphoreType.DMA((2,))],
        compiler_params=_cparams(("arbitrary",)),
        name="combine",
    )(pos_t, pos_t, y_rows, h1, rg, final_w.reshape(1, D_MODEL))


def kernel(x_prompt, x_sample, cache_k, cache_v, cache_k_idx, state_gla, norm_mix_w, w_in,
           w_gla_a2, b_gla_a2, gla_norm_w, w_out, norm_ffn_w, router_w, router_b, w_gu, b_gu,
           w_down, b_down, norm_final_w):
    b_p, t_p, _ = x_prompt.shape
    b_s, t_s, _ = x_sample.shape
    past = cache_k.shape[2]
    assert b_p == 1 and norm_mix_w.shape[0] == 1
    l = 0
    pw = _proj_weights(norm_mix_w[l], w_in[l])
    gconsts = _gla_consts(w_gla_a2[l], b_gla_a2[l])

    pp = _project_t(x_prompt.reshape(t_p, D_MODEL), _proj_t_weights(norm_mix_w[l], w_in[l]), DSA_BK)
    gla_p, kf_p, vf_p, kb_p, ikf_p, ikb_p, g2_p, qt_p, iqt_p, vt3_p, g2t_p = pp
    s0 = jnp.zeros((1, GLA_WIDTH, GLA_QK), F32)
    og_p, st_p = _gla(gla_p, g2_p, s0, gconsts, 1, t_p, 512, 16)
    od_p = _dsa_prompt_t(qt_p, iqt_p, g2t_p, ikb_p, kb_p, vt3_p, DSA_BQ, DSA_BK)

    ps = _project(x_sample.reshape(b_s * t_s, D_MODEL), pw, 512)
    gla_s, q_s, kf_s, vf_s, kb_s, vb_s, iq_s, iklo_s, ikhi_s, ikf_s, g2_s = ps
    og_s, st_s = _gla(gla_s, g2_s, _state_to_kernel(state_gla[l]), gconsts, b_s, t_s, t_s, 16)
    zc = jnp.zeros((b_s, past, IDX_DIM), BF16)
    cik = cache_k_idx[l].astype(BF16)
    cilo = jnp.concatenate([cik, zc], axis=2)
    cihi = jnp.concatenate([zc, cik], axis=2)
    od_s = _dsa_sample(q_s, iq_s, g2_s, cache_k[l], cache_v[l], cilo, cihi,
                       kb_s, vb_s, iklo_s, ikhi_s, t_s, 512)

    mconsts = _mix_consts(gla_norm_w[l], w_out[l], norm_ffn_w[l], router_w[l], router_b[l])
    n_s = b_s * t_s
    tb = TOKEN_BLOCK
    h1_p, xn_p, re_p, rg_p, cnt_p = _mix(og_p, gla_p, od_p, x_prompt.reshape(t_p, D_MODEL), mconsts, tb)
    h1_s, xn_s, re_s, rg_s, cnt_s = _mix(og_s, gla_s, od_s, x_sample.reshape(n_s, D_MODEL), mconsts, tb)
    n_pairs = (t_p + n_s) * TOP_K
    base, block_e, n_used, ends, tots = _route_tables(
        jnp.concatenate([cnt_p, cnt_s], axis=0), MOE_ROWS, n_pairs)
    n_rows = block_e.shape[0] * MOE_ROWS
    pos_p = _positions(re_p, base[0:t_p // tb], tb)
    pos_s = _positions(re_s, base[t_p // tb:], tb)
    x_rows = _dispatch(xn_p, xn_s, jnp.concatenate([pos_p, pos_s], axis=0), ends, tots,
                       n_rows, tb, MOE_ROWS)
    y_rows = _moe(x_rows, block_e, n_used, w_gu[l], b_gu[l], w_down[l], b_down[l], MOE_ROWS)
    y_p = _combine(y_rows, pos_p, h1_p, rg_p, norm_final_w, tb)
    y_s = _combine(y_rows, pos_s, h1_s, rg_s, norm_final_w, tb)
    y_p = y_p.reshape(x_prompt.shape)
    y_s = y_s.reshape(x_sample.shape)
    return (y_p, y_s,
            kf_p.reshape(1, 1, t_p, DSA_HEADS, DSA_DH), vf_p.reshape(1, 1, t_p, DSA_HEADS, DSA_DH),
            ikf_p.reshape(1, 1, t_p, IDX_DIM), _state_from_kernel(st_p)[None],
            kf_s.reshape(1, b_s, t_s, DSA_HEADS, DSA_DH), vf_s.reshape(1, b_s, t_s, DSA_HEADS, DSA_DH),
            ikf_s.reshape(1, b_s, t_s, IDX_DIM), _state_from_kernel(st_s)[None])
```
